```python
import jax, jax.numpy as jnp
from jax import lax
import numpy as np

D_MODEL = 1024
BATCH = 4
SEQ = 8192
DEPTH = 2

HEAD_DIM = 64
N_HEADS_FOX = 4
DIL_PATTERNS = ((128, 1), (512, 4), (2048, 16))
N_DIL_GROUPS = 3
HEADS_PER_DIL = 2
N_HEADS_DIL = N_DIL_GROUPS * HEADS_PER_DIL
N_HEADS_MOBA = 6
N_HEADS = N_HEADS_FOX + N_HEADS_DIL + N_HEADS_MOBA
MIX_WIDTH = N_HEADS * HEAD_DIM
MOBA_BLOCK = 256
MOBA_TOPK = 3
MOBA_Q_CHUNK = 32
Q_BLOCK = 128
PLE_DIM = 256
D_FF = -(-8 * D_MODEL // (3 * 256)) * 256
ROPE_THETA = 10000.0
RMS_EPS = 1e-6
NEG_INF = -1e30
N_GATE_COLS = 3 * D_MODEL
W_IN_COLS = 3 * MIX_WIDTH + N_HEADS_FOX + N_GATE_COLS

kernel_name = "hybrid_fox_dilated_moba_block"


def rms_norm(x, g):
    xf = x.astype(jnp.float32)
    var = jnp.mean(xf * xf, axis=-1, keepdims=True)
    return (xf * lax.rsqrt(var + RMS_EPS) * g.astype(jnp.float32)).astype(x.dtype)


def rope_tables(seq):
    inv = 1.0 / (ROPE_THETA ** (jnp.arange(0, HEAD_DIM, 2, dtype=jnp.float32) / HEAD_DIM))
    ang = jnp.arange(seq, dtype=jnp.float32)[:, None] * inv[None, :]
    return jnp.cos(ang), jnp.sin(ang)


def apply_rope(x, cos, sin):
    xf = x.astype(jnp.float32)
    x1, x2 = jnp.split(xf, 2, axis=-1)
    c = cos[None, :, None, :]
    s = sin[None, :, None, :]
    return jnp.concatenate([x1 * c - x2 * s, x2 * c + x1 * s], axis=-1).astype(x.dtype)


def fox_attention(q, k, v, f_logit):
    B, S, H, D = q.shape
    scale = D ** -0.5
    c = jnp.cumsum(jax.nn.log_sigmoid(f_logit.astype(jnp.float32)), axis=1).transpose(0, 2, 1)
    kh = k.transpose(0, 2, 1, 3)
    vh = v.transpose(0, 2, 1, 3)
    nq = S // Q_BLOCK
    qb = q.reshape(B, nq, Q_BLOCK, H, D).transpose(1, 0, 3, 2, 4)
    cb = c.reshape(B, H, nq, Q_BLOCK).transpose(2, 0, 1, 3)
    key_pos = jnp.arange(S)

    def block(args):
        q_blk, c_blk, i = args
        s = jnp.einsum('bhqd,bhkd->bhqk', q_blk, kh).astype(jnp.float32) * scale
        s = s + c_blk[..., None] - c[:, :, None, :]
        qpos = i * Q_BLOCK + jnp.arange(Q_BLOCK)
        s = jnp.where(key_pos[None, :] <= qpos[:, None], s, NEG_INF)
        p = jax.nn.softmax(s, axis=-1).astype(v.dtype)
        return jnp.einsum('bhqk,bhkd->bhqd', p, vh)

    out = lax.map(block, (qb, cb, jnp.arange(nq)))
    return out.transpose(1, 0, 3, 2, 4).reshape(B, S, H * D)


def dilated_attention(q, k, v):
    B, S, _, D = q.shape
    G, Hg = N_DIL_GROUPS, HEADS_PER_DIL
    scale = D ** -0.5
    pad = max(w for w, _ in DIL_PATTERNS)
    qg = q.reshape(B, S, G, Hg, D)
    pad_spec = ((0, 0), (pad, 0), (0, 0), (0, 0), (0, 0))
    kg = jnp.pad(k.reshape(B, S, G, Hg, D), pad_spec)
    vg = jnp.pad(v.reshape(B, S, G, Hg, D), pad_spec)
    k_groups = [kg[:, :, g] for g in range(G)]
    v_groups = [vg[:, :, g] for g in range(G)]
    nq = S // Q_BLOCK

    def block(i):
        t0 = i * Q_BLOCK
        q_blk = lax.dynamic_slice_in_dim(qg, t0, Q_BLOCK, axis=1)
        qpos = t0 + jnp.arange(Q_BLOCK)
        outs, lses = [], []
        for g, (window, dil) in enumerate(DIL_PATTERNS):
            offs = dil * jnp.arange(window // dil + 1)
            kpos = qpos[:, None] - offs[None, :]
            k_sel = jnp.take(k_groups[g], kpos + pad, axis=1)
            v_sel = jnp.take(v_groups[g], kpos + pad, axis=1)
            s = jnp.einsum('bqhd,bqnhd->bqhn', q_blk[:, :, g], k_sel).astype(jnp.float32) * scale
            s = jnp.where((kpos >= 0)[None, :, None, :], s, NEG_INF)
            m = jnp.max(s, axis=-1, keepdims=True)
            e = jnp.exp(s - m)
            den = jnp.sum(e, axis=-1, keepdims=True)
            o = jnp.einsum('bqhn,bqnhd->bqhd', (e / den).astype(v.dtype), v_sel)
            outs.append(o)
            lses.append((m + jnp.log(den))[..., 0])
        wts = jax.nn.softmax(jnp.stack(lses, axis=0), axis=0)
        o = jnp.sum(wts[..., None] * jnp.stack(outs, axis=0).astype(jnp.float32), axis=0)
        return o.astype(v.dtype).reshape(B, Q_BLOCK, Hg * D)

    out = lax.map(block, jnp.arange(nq))
    return out.transpose(1, 0, 2, 3).reshape(B, S, Hg * D)


def moba_attention(q, k, v):
    B, S, H, D = q.shape
    L = MOBA_BLOCK
    scale = D ** -0.5
    nb = -(-S // L)
    s_pad = nb * L
    k_p = jnp.pad(k, ((0, 0), (0, s_pad - S), (0, 0), (0, 0)))
    v_p = jnp.pad(v, ((0, 0), (0, s_pad - S), (0, 0), (0, 0)))
    kb = k_p.reshape(B, nb, L, H, D).transpose(0, 3, 1, 2, 4)
    vb = v_p.reshape(B, nb, L, H, D).transpose(0, 3, 1, 2, 4)
    k_mean = jnp.mean(kb.astype(jnp.float32), axis=3).astype(k.dtype)
    n_sel = min(MOBA_TOPK, nb)
    b_idx = jnp.arange(B)[:, None, None, None]
    h_idx = jnp.arange(H)[None, :, None, None]
    nq = S // MOBA_Q_CHUNK

    def chunk(i):
        t0 = i * MOBA_Q_CHUNK
        q_c = lax.dynamic_slice_in_dim(q, t0, MOBA_Q_CHUNK, axis=1).transpose(0, 2, 1, 3)
        qpos = t0 + jnp.arange(MOBA_Q_CHUNK)
        own = t0 // L
        gate = jnp.einsum('bhqd,bhnd->bhqn', q_c, k_mean).astype(jnp.float32)
        gate = jnp.where(jnp.arange(nb) < own, gate, NEG_INF)
        _, sel = lax.top_k(gate, n_sel)
        sel_valid = sel < own
        k_sel = kb[b_idx, h_idx, sel]
        v_sel = vb[b_idx, h_idx, sel]
        s_sel = jnp.einsum('bhqd,bhqnld->bhqnl', q_c, k_sel).astype(jnp.float32) * scale
        s_sel = jnp.where(sel_valid[..., None], s_sel, NEG_INF)
        k_own = lax.dynamic_slice_in_dim(kb, own, 1, axis=2)[:, :, 0]
        v_own = lax.dynamic_slice_in_dim(vb, own, 1, axis=2)[:, :, 0]
        s_own = jnp.einsum('bhqd,bhld->bhql', q_c, k_own).astype(jnp.float32) * scale
        own_pos = own * L + jnp.arange(L)
        s_own = jnp.where(own_pos[None, :] <= qpos[:, None], s_own, NEG_INF)
        s_all = jnp.concatenate([s_sel.reshape(B, H, MOBA_Q_CHUNK, n_sel * L), s_own], axis=-1)
        p = jax.nn.softmax(s_all, axis=-1).astype(v.dtype)
        p_sel = p[..., : n_sel * L].reshape(B, H, MOBA_Q_CHUNK, n_sel, L)
        p_own = p[..., n_sel * L:]
        o = jnp.einsum('bhqnl,bhqnld->bhqd', p_sel, v_sel) + jnp.einsum('bhql,bhld->bhqd', p_own, v_own)
        return o.transpose(0, 2, 1, 3).reshape(B, MOBA_Q_CHUNK, H * D)

    out = lax.map(chunk, jnp.arange(nq))
    return out.transpose(1, 0, 2, 3).reshape(B, S, H * D)


def setup_inputs(seed: int = 0) -> dict:
    key = jax.random.key(seed)
    ks = jax.random.split(key, 20)
    f32 = jnp.float32

    def w(k, shape, fan_in):
        return jax.random.normal(k, shape, f32) * fan_in ** -0.5

    def gain(k):
        return 1.0 + 0.05 * jax.random.normal(k, (DEPTH, D_MODEL), f32)

    return {
        "x": jax.random.normal(ks[0], (BATCH, SEQ, D_MODEL), f32),
        "p": jax.random.normal(ks[1], (DEPTH, BATCH, SEQ, PLE_DIM), f32),
        "g_mix_pre": gain(ks[2]),
        "w_in": w(ks[3], (DEPTH, D_MODEL, W_IN_COLS), D_MODEL),
        "b_f": 2.0 + 0.5 * jax.random.normal(ks[4], (DEPTH, N_HEADS_FOX), f32),
        "w_br_a": w(ks[5], (DEPTH, N_HEADS_FOX * HEAD_DIM, D_MODEL), N_HEADS_FOX * HEAD_DIM),
        "w_br_b": w(ks[6], (DEPTH, HEADS_PER_DIL * HEAD_DIM, D_MODEL), HEADS_PER_DIL * HEAD_DIM),
        "w_br_c": w(ks[7], (DEPTH, N_HEADS_MOBA * HEAD_DIM, D_MODEL), N_HEADS_MOBA * HEAD_DIM),
        "w_out": w(ks[8], (DEPTH, D_MODEL, D_MODEL), D_MODEL),
        "g_mix_post": gain(ks[9]),
        "g_ffn_pre": gain(ks[10]),
        "w_ffn_gate": w(ks[11], (DEPTH, D_MODEL, D_FF), D_MODEL),
        "w_ffn_up": w(ks[12], (DEPTH, D_MODEL, D_FF), D_MODEL),
        "w_ffn_down": w(ks[13], (DEPTH, D_FF, D_MODEL), D_FF),
        "g_ffn_post": gain(ks[14]),
        "w_ple": w(ks[15], (DEPTH, PLE_DIM, D_MODEL), PLE_DIM),
        "w_ple_gate": w(ks[16], (DEPTH, D_MODEL, D_MODEL), D_MODEL),
        "g_ple_post": gain(ks[17]),
    }


def reference(x, p, g_mix_pre, w_in, b_f, w_br_a, w_br_b, w_br_c, w_out, g_mix_post,
              g_ffn_pre, w_ffn_gate, w_ffn_up, w_ffn_down, g_ffn_post, w_ple, w_ple_gate, g_ple_post):
    B, S, _ = x.shape
    cos, sin = rope_tables(S)
    cos = cos.astype(jnp.float32)
    sin = sin.astype(jnp.float32)
    for i in range(DEPTH):
        h = rms_norm(x, g_mix_pre[i])
        proj = jnp.einsum('bsd,dc->bsc', h, w_in[i])
        q = proj[..., :MIX_WIDTH].reshape(B, S, N_HEADS, HEAD_DIM)
        k = proj[..., MIX_WIDTH:2 * MIX_WIDTH].reshape(B, S, N_HEADS, HEAD_DIM)
        v = proj[..., 2 * MIX_WIDTH:3 * MIX_WIDTH].reshape(B, S, N_HEADS, HEAD_DIM)
        f_logit = proj[..., 3 * MIX_WIDTH:3 * MIX_WIDTH + N_HEADS_FOX] + b_f[i]
        gates = jax.nn.sigmoid(proj[..., 3 * MIX_WIDTH + N_HEADS_FOX:])
        g_a = gates[..., :D_MODEL]
        g_b = gates[..., D_MODEL:2 * D_MODEL]
        g_c = gates[..., 2 * D_MODEL:]
        q_r = apply_rope(q[:, :, N_HEADS_FOX:], cos, sin)
        k_r = apply_rope(k[:, :, N_HEADS_FOX:], cos, sin)
        y_a = fox_attention(q[:, :, :N_HEADS_FOX], k[:, :, :N_HEADS_FOX], v[:, :, :N_HEADS_FOX], f_logit)
        y_b = dilated_attention(q_r[:, :, :N_HEADS_DIL], k_r[:, :, :N_HEADS_DIL],
                                v[:, :, N_HEADS_FOX:N_HEADS_FOX + N_HEADS_DIL])
        y_c = moba_attention(q_r[:, :, N_HEADS_DIL:], k_r[:, :, N_HEADS_DIL:],
                             v[:, :, N_HEADS_FOX + N_HEADS_DIL:])
        merged = (g_a * jnp.einsum('bsc,cd->bsd', y_a, w_br_a[i])
                  + g_b * jnp.einsum('bsc,cd->bsd', y_b, w_br_b[i])
                  + g_c * jnp.einsum('bsc,cd->bsd', y_c, w_br_c[i]))
        x = x + rms_norm(jnp.einsum('bsd,de->bse', merged, w_out[i]), g_mix_post[i])
        h = rms_norm(x, g_ffn_pre[i])
        ff = jax.nn.silu(jnp.einsum('bsd,df->bsf', h, w_ffn_gate[i])) * jnp.einsum('bsd,df->bsf', h, w_ffn_up[i])
        x = x + rms_norm(jnp.einsum('bsf,fd->bsd', ff, w_ffn_down[i]), g_ffn_post[i])
        ple = jnp.einsum('bse,ed->bsd', p[i], w_ple[i]) * jax.nn.sigmoid(jnp.einsum('bsd,de->bse', x, w_ple_gate[i]))
        x = x + rms_norm(ple, g_ple_post[i])
    return x
```

```python
import functools

import numpy as np
import jax
import jax.numpy as jnp
from jax import lax
from jax.experimental import pallas as pl
from jax.experimental.pallas import tpu as pltpu

D_MODEL = 1024
HEAD_DIM = 64
N_HEADS_FOX = 4
DIL_PATTERNS = ((128, 1), (512, 4), (2048, 16))
N_HEADS_MOBA = 6
MIX_WIDTH = 1024
MOBA_BLOCK = 256
MOBA_TOPK = 3
PLE_DIM = 256
D_FF = 2816
ROPE_THETA = 10000.0
RMS_EPS = 1e-6
NEG_INF = -1e30
SCALE = HEAD_DIM ** -0.5

LANES = 128
HEADS_PER_LANE_BLOCK = LANES // HEAD_DIM
FOX_COLS = N_HEADS_FOX * HEAD_DIM
DIL_COLS = len(DIL_PATTERNS) * 2 * HEAD_DIM
MOBA_COLS = N_HEADS_MOBA * HEAD_DIM
N_GATE_COLS = 3 * D_MODEL
FORGET_PIECES = 3

TM = 512
TQ = 256
TK = 256
VMEM_LIMIT = 56 * 1024 * 1024

F32 = jnp.float32
BF16 = jnp.bfloat16


def _dot(a, b):
    return jnp.dot(a, b, preferred_element_type=F32)


def _dot_nt(a, b):
    return lax.dot_general(a, b, (((1,), (1,)), ((), ())), preferred_element_type=F32)


def _rms(x, g):
    var = jnp.mean(x * x, axis=-1, keepdims=True)
    return x * lax.rsqrt(var + RMS_EPS) * g


def _split3(x):
    hi = x.astype(BF16)
    r1 = x - hi.astype(F32)
    mid = r1.astype(BF16)
    lo = (r1 - mid.astype(F32)).astype(BF16)
    return hi, mid, lo


def _resident(shape):
    nd = len(shape)
    return pl.BlockSpec(shape, lambda *_: (0,) * nd, pipeline_mode=pl.Buffered(1))


def _params(*sem):
    return pltpu.CompilerParams(dimension_semantics=sem, vmem_limit_bytes=VMEM_LIMIT)


def _proj_kernel(x_ref, g_ref, wq_ref, wk_ref, wv_ref, wf_ref, bf_ref, wg_ref, cos_ref, sin_ref,
                 q_ref, k_ref, v_ref, kx_ref, gates_ref, kmean_ref, carry_ref):
    i = pl.program_id(1)
    hb = _rms(x_ref[0], g_ref[...]).astype(BF16)
    cos = cos_ref[...]
    sin = sin_ref[...]
    lane = lax.broadcasted_iota(jnp.int32, (TM, LANES), 1)
    first_half = (lane % HEAD_DIM) < (HEAD_DIM // 2)

    def rope(t):
        rot = jnp.where(first_half, pltpu.roll(t, LANES - HEAD_DIM // 2, 1),
                        pltpu.roll(t, HEAD_DIM // 2, 1))
        return t * cos + rot * sin

    n_blk = MIX_WIDTH // LANES
    first_rope_blk = FOX_COLS // LANES
    first_moba_blk = (FOX_COLS + DIL_COLS) // LANES

    q = _dot(hb, wq_ref[...])
    for c in range(n_blk):
        t = q[:, c * LANES:(c + 1) * LANES]
        if c >= first_rope_blk:
            t = rope(t)
        q_ref[0, :, c * LANES:(c + 1) * LANES] = (t * SCALE).astype(BF16)

    k = _dot(hb, wk_ref[...])
    for c in range(n_blk):
        t = k[:, c * LANES:(c + 1) * LANES]
        if c >= first_rope_blk:
            t = rope(t)
        k_ref[0, :, c * LANES:(c + 1) * LANES] = t.astype(BF16)
        if c >= first_moba_blk:
            cm = c - first_moba_blk
            for r in range(TM // MOBA_BLOCK):
                kmean_ref[0, 0, r:r + 1, cm * LANES:(cm + 1) * LANES] = jnp.mean(
                    t[r * MOBA_BLOCK:(r + 1) * MOBA_BLOCK], axis=0, keepdims=True)

    v_ref[0] = _dot(hb, wv_ref[...]).astype(BF16)

    f = _dot(hb, wf_ref[...]) + bf_ref[...]
    ls = jnp.minimum(f, 0.0) - jnp.log1p(jnp.exp(-jnp.abs(f)))
    ls = jnp.where(lane < N_HEADS_FOX * FORGET_PIECES, ls, 0.0)
    row = lax.broadcasted_iota(jnp.int32, (TM, TM), 0)
    col = lax.broadcasted_iota(jnp.int32, (TM, TM), 1)
    tri = jnp.where(col <= row, 1.0, 0.0).astype(BF16)
    hi, mid, lo = _split3(ls)
    cs = _dot(tri, hi) + _dot(tri, mid) + _dot(tri, lo)

    @pl.when(i == 0)
    def _():
        carry_ref[...] = jnp.zeros_like(carry_ref)

    cs = cs + carry_ref[...]
    carry_ref[...] = cs[TM - 1:TM, :]
    hi, mid, lo = _split3(-cs)
    piece = lane % FORGET_PIECES
    kx_ref[0] = jnp.where(piece == 0, hi, jnp.where(piece == 1, mid, lo))

    for c in range(N_GATE_COLS // D_MODEL):
        z = _dot(hb, wg_ref[:, c * D_MODEL:(c + 1) * D_MODEL])
        gates_ref[0, :, c * D_MODEL:(c + 1) * D_MODEL] = jax.nn.sigmoid(z).astype(BF16)


def _proj(x, g, wq, wk, wv, wf, bf, wg, cos_t, sin_t):
    B, S, D = x.shape
    n_t = S // TM
    tok = lambda w: pl.BlockSpec((1, TM, w), lambda b, i: (b, i, 0))
    return pl.pallas_call(
        _proj_kernel,
        grid=(B, n_t),
        in_specs=[tok(D), _resident((1, D)), _resident((D, MIX_WIDTH)), _resident((D, MIX_WIDTH)),
                  _resident((D, MIX_WIDTH)), _resident((D, LANES)), _resident((1, LANES)),
                  _resident((D, N_GATE_COLS)),
                  pl.BlockSpec((TM, LANES), lambda b, i: (i, 0)),
                  pl.BlockSpec((TM, LANES), lambda b, i: (i, 0))],
        out_specs=[tok(MIX_WIDTH), tok(MIX_WIDTH), tok(MIX_WIDTH), tok(LANES), tok(N_GATE_COLS),
                   pl.BlockSpec((1, 1, TM // MOBA_BLOCK, MOBA_COLS), lambda b, i: (b, i, 0, 0))],
        out_shape=[jax.ShapeDtypeStruct((B, S, MIX_WIDTH), BF16)] * 3
        + [jax.ShapeDtypeStruct((B, S, LANES), BF16),
           jax.ShapeDtypeStruct((B, S, N_GATE_COLS), BF16),
           jax.ShapeDtypeStruct((B, n_t, TM // MOBA_BLOCK, MOBA_COLS), F32)],
        scratch_shapes=[pltpu.VMEM((1, LANES), F32)],
        compiler_params=_params("arbitrary", "arbitrary"),
        name="proj",
    )(x, g, wq, wk, wv, wf, bf, wg, cos_t, sin_t)


def _flash(q_aug, k_aug, v2, m, l, acc, bias=None, mask=None):
    s = _dot_nt(q_aug, k_aug)
    if bias is not None:
        s = s + bias
    if mask is not None:
        s = jnp.where(mask, s, NEG_INF)
    m_new = jnp.maximum(m, jnp.max(s, axis=1, keepdims=True))
    alpha = jnp.exp(m - m_new)
    p = jnp.exp(s - m_new)
    l_new = alpha * l + jnp.sum(p, axis=1, keepdims=True)
    acc_new = alpha * acc + _dot(p.astype(BF16), v2)
    return m_new, l_new, acc_new


def _flash_init():
    return (jnp.full((TQ, 1), NEG_INF, F32), jnp.zeros((TQ, 1), F32), jnp.zeros((TQ, LANES), F32))


def _head_lanes(lane, hl):
    return (lane >= hl * HEAD_DIM) & (lane < (hl + 1) * HEAD_DIM)


def _causal_mask():
    r = lax.broadcasted_iota(jnp.int32, (TQ, TK), 0)
    c = lax.broadcasted_iota(jnp.int32, (TQ, TK), 1)
    return c <= r


def _fox_kernel(q_ref, k_ref, kx_ref, v_ref, o_ref):
    p = pl.program_id(1)
    i = pl.program_id(2)
    q2 = q_ref[0]
    lane = lax.broadcasted_iota(jnp.int32, (TQ, LANES), 1)
    q_aug = []
    for hl in range(HEADS_PER_LANE_BLOCK):
        first = FORGET_PIECES * (HEADS_PER_LANE_BLOCK * p + hl)
        ones = jnp.where((lane >= first) & (lane < first + FORGET_PIECES), 1.0, 0.0).astype(BF16)
        qm = jnp.where(_head_lanes(lane, hl), q2, jnp.zeros_like(q2))
        q_aug.append(jnp.concatenate([qm, ones], axis=1))

    def step(j, state, mask):
        off = pl.multiple_of(j * TK, TK)
        k_aug = jnp.concatenate([k_ref[0, pl.ds(off, TK), :], kx_ref[0, pl.ds(off, TK), :]], axis=1)
        v2 = v_ref[0, pl.ds(off, TK), :]
        return tuple(_flash(q_aug[hl], k_aug, v2, *state[hl], mask=mask)
                     for hl in range(HEADS_PER_LANE_BLOCK))

    state = step(i, (_flash_init(), _flash_init()), _causal_mask())
    state = lax.fori_loop(0, i, lambda j, st: step(j, st, None), state)
    (_, l0, a0), (_, l1, a1) = state
    o_ref[0] = jnp.where(lane < HEAD_DIM, a0 / l0, a1 / l1).astype(BF16)


def _fox(q, k, kx, v):
    B, S, _ = q.shape
    n_pairs = FOX_COLS // LANES
    seq = lambda sel: pl.BlockSpec((1, S, LANES), sel)
    return pl.pallas_call(
        _fox_kernel,
        grid=(B, n_pairs, S // TQ),
        in_specs=[pl.BlockSpec((1, TQ, LANES), lambda b, p, i: (b, i, p)),
                  seq(lambda b, p, i: (b, 0, p)),
                  seq(lambda b, p, i: (b, 0, 0)),
                  seq(lambda b, p, i: (b, 0, p))],
        out_specs=pl.BlockSpec((1, TQ, LANES), lambda b, p, i: (b, i, p)),
        out_shape=jax.ShapeDtypeStruct((B, S, FOX_COLS), BF16),
        compiler_params=_params("parallel", "parallel", "arbitrary"),
        name="fox",
    )(q, k, kx, v)


def _dilated_tables():
    ql = np.arange(TQ)[:, None]
    kl = np.arange(TK)[None, :]
    tabs, index = [], {}
    for g, (window, dil) in enumerate(DIL_PATTERNS):
        last = window // TK if window % TK == 0 else window // TK + 1
        for d in range(last + 1):
            rel = ql + d * TQ - kl
            valid = (rel >= 0) & (rel <= window) & (rel % dil == 0)
            tab = np.where(valid, 0.0, NEG_INF).astype(np.float32)
            for n, t in enumerate(tabs):
                if np.array_equal(t, tab):
                    index[(g, d)] = n
                    break
            else:
                index[(g, d)] = len(tabs)
                tabs.append(tab)
    return np.stack(tabs), index


_DIL_TABLES, _DIL_INDEX = _dilated_tables()
_DIL_LAST = tuple(max(d for (g, d) in _DIL_INDEX if g == gg) for gg in range(len(DIL_PATTERNS)))


def _dilated_kernel(*refs):
    n_g = len(DIL_PATTERNS)
    q_refs, k_refs, v_refs = refs[:n_g], refs[n_g:2 * n_g], refs[2 * n_g:3 * n_g]
    tab_ref, o_ref, m_sc, l_sc, a_sc = refs[3 * n_g:]
    i = pl.program_id(1)
    lane = lax.broadcasted_iota(jnp.int32, (TQ, LANES), 1)

    for hl in range(HEADS_PER_LANE_BLOCK):
        m_sc[hl], l_sc[hl], a_sc[hl] = _flash_init()

    def block(g, d_blk, tab):
        off = pl.multiple_of((i - d_blk) * TK, TK)
        k2 = k_refs[g][0, pl.ds(off, TK), :]
        v2 = v_refs[g][0, pl.ds(off, TK), :]
        q2 = q_refs[g][0]
        bias = tab_ref[tab]
        for hl in range(HEADS_PER_LANE_BLOCK):
            qm = jnp.where(_head_lanes(lane, hl), q2, jnp.zeros_like(q2))
            m_sc[hl], l_sc[hl], a_sc[hl] = _flash(qm, k2, v2, m_sc[hl], l_sc[hl], a_sc[hl], bias=bias)

    for g in range(n_g):
        block(g, 0, _DIL_INDEX[(g, 0)])
    for g in range(n_g):
        last = _DIL_LAST[g]
        mids = [d for d in range(1, last) if _DIL_INDEX[(g, d)] == _DIL_INDEX[(g, 1)]]
        assert mids == list(range(1, last))
        if mids:
            mid_tab = _DIL_INDEX[(g, 1)]

            def mid_body(d, carry, g=g, mid_tab=mid_tab):
                block(g, d, mid_tab)
                return carry

            lax.fori_loop(1, jnp.minimum(i, last - 1) + 1, mid_body, 0)

        @pl.when(i >= last)
        def _(g=g, last=last):
            block(g, last, _DIL_INDEX[(g, last)])

    o_ref[0] = jnp.where(lane < HEAD_DIM, a_sc[0] / l_sc[0], a_sc[1] / l_sc[1]).astype(BF16)


def _dilated(q, k, v):
    B, S, _ = q.shape
    n_g = len(DIL_PATTERNS)
    first = FOX_COLS // LANES
    q_specs = [pl.BlockSpec((1, TQ, LANES), functools.partial(lambda b, i, c: (b, i, c), c=first + g))
               for g in range(n_g)]
    kv_specs = [pl.BlockSpec((1, S, LANES), functools.partial(lambda b, i, c: (b, 0, c), c=first + g))
                for g in range(n_g)]
    n_tab = _DIL_TABLES.shape[0]
    return pl.pallas_call(
        _dilated_kernel,
        grid=(B, S // TQ),
        in_specs=q_specs + kv_specs + kv_specs + [_resident((n_tab, TQ, TK))],
        out_specs=pl.BlockSpec((1, TQ, LANES), lambda b, i: (b, i, 0)),
        out_shape=jax.ShapeDtypeStruct((B, S, LANES), BF16),
        scratch_shapes=[pltpu.VMEM((HEADS_PER_LANE_BLOCK, TQ, 1), F32),
                        pltpu.VMEM((HEADS_PER_LANE_BLOCK, TQ, 1), F32),
                        pltpu.VMEM((HEADS_PER_LANE_BLOCK, TQ, LANES), F32)],
        compiler_params=_params("parallel", "arbitrary"),
        name="dilated",
    )(*([q] * n_g + [k] * n_g + [v] * n_g + [jnp.asarray(_DIL_TABLES)]))


def _moba_kernel(q_ref, k_ref, v_ref, km_ref, o_ref):
    i = pl.program_id(2)
    q2 = q_ref[0]
    km = km_ref[0]
    lane = lax.broadcasted_iota(jnp.int32, (TQ, LANES), 1)
    past = lane < i
    q_m, q_aug = [], []
    for hl in range(HEADS_PER_LANE_BLOCK):
        qm = jnp.where(_head_lanes(lane, hl), q2, jnp.zeros_like(q2))
        gate = jnp.where(past, _dot_nt(qm, km), NEG_INF)
        sel = jnp.zeros((TQ, LANES), jnp.bool_)
        for _ in range(MOBA_TOPK):
            top = jnp.max(gate, axis=1, keepdims=True)
            idx = jnp.min(jnp.where(gate == top, lane, LANES), axis=1, keepdims=True)
            hit = lane == idx
            sel = sel | hit
            gate = jnp.where(hit, -jnp.inf, gate)
        sel_bias = jnp.where(sel & past, 0.0, NEG_INF).astype(BF16)
        q_m.append(qm)
        q_aug.append(jnp.concatenate([qm, sel_bias], axis=1))

    own = pl.multiple_of(i * TK, TK)
    k_own = k_ref[0, pl.ds(own, TK), :]
    v_own = v_ref[0, pl.ds(own, TK), :]
    causal = _causal_mask()
    state = tuple(_flash(q_m[hl], k_own, v_own, *_flash_init(), mask=causal)
                  for hl in range(HEADS_PER_LANE_BLOCK))

    lane_k = lax.broadcasted_iota(jnp.int32, (TK, LANES), 1)

    def body(j, st):
        off = pl.multiple_of(j * TK, TK)
        onehot = jnp.where(lane_k == j, 1.0, 0.0).astype(BF16)
        k_aug = jnp.concatenate([k_ref[0, pl.ds(off, TK), :], onehot], axis=1)
        v2 = v_ref[0, pl.ds(off, TK), :]
        return tuple(_flash(q_aug[hl], k_aug, v2, *st[hl]) for hl in range(HEADS_PER_LANE_BLOCK))

    state = lax.fori_loop(0, i, body, state)
    (_, l0, a0), (_, l1, a1) = state
    o_ref[0] = jnp.where(lane < HEAD_DIM, a0 / l0, a1 / l1).astype(BF16)


def _moba(q, k, v, km):
    B, S, _ = q.shape
    assert TQ == MOBA_BLOCK and TK == MOBA_BLOCK and S // MOBA_BLOCK <= LANES
    n_pairs = MOBA_COLS // LANES
    first = (FOX_COLS + DIL_COLS) // LANES
    return pl.pallas_call(
        _moba_kernel,
        grid=(B, n_pairs, S // TQ),
        in_specs=[pl.BlockSpec((1, TQ, LANES), lambda b, p, i: (b, i, first + p)),
                  pl.BlockSpec((1, S, LANES), lambda b, p, i: (b, 0, first + p)),
                  pl.BlockSpec((1, S, LANES), lambda b, p, i: (b, 0, first + p)),
                  pl.BlockSpec((1, LANES, LANES), lambda b, p, i: (b, 0, p))],
        out_specs=pl.BlockSpec((1, TQ, LANES), lambda b, p, i: (b, i, p)),
        out_shape=jax.ShapeDtypeStruct((B, S, MOBA_COLS), BF16),
        compiler_params=_params("parallel", "parallel", "arbitrary"),
        name="moba",
    )(q, k, v, km)


def _merge_kernel(ya_ref, yb_ref, yc_ref, gates_ref, x_ref, wa_ref, wb_ref, wc_ref, wo_ref, g_ref, o_ref):
    merged = None
    for c, (y_ref, w_ref) in enumerate(((ya_ref, wa_ref), (yb_ref, wb_ref), (yc_ref, wc_ref))):
        gate = gates_ref[:, c * D_MODEL:(c + 1) * D_MODEL].astype(F32)
        term = gate * _dot(y_ref[...], w_ref[...])
        merged = term if merged is None else merged + term
    out = _dot(merged.astype(BF16), wo_ref[...])
    o_ref[...] = x_ref[...] + _rms(out, g_ref[...])


def _merge(ya, yb, yc, gates, x, wa, wb, wc, wo, g):
    T, D = x.shape
    tok = lambda w: pl.BlockSpec((TM, w), lambda i: (i, 0))
    return pl.pallas_call(
        _merge_kernel,
        grid=(T // TM,),
        in_specs=[tok(FOX_COLS), tok(LANES), tok(MOBA_COLS), tok(N_GATE_COLS), tok(D),
                  _resident(wa.shape), _resident(wb.shape), _resident(wc.shape), _resident(wo.shape),
                  _resident((1, D))],
        out_specs=tok(D),
        out_shape=jax.ShapeDtypeStruct((T, D), F32),
        compiler_params=_params("parallel"),
        name="merge",
    )(ya, yb, yc, gates, x, wa, wb, wc, wo, g)


FF_CHUNK = D_FF // 2


def _ffn_kernel(x_ref, p_ref, g_pre_ref, wgate_ref, wup_ref, wdown_ref, g_post_ref,
                wple_ref, wpg_ref, g_ple_ref, o_ref):
    x = x_ref[...]
    hb = _rms(x, g_pre_ref[...]).astype(BF16)
    down = None
    for c in range(D_FF // FF_CHUNK):
        cols = slice(c * FF_CHUNK, (c + 1) * FF_CHUNK)
        gate = _dot(hb, wgate_ref[:, cols])
        up = _dot(hb, wup_ref[:, cols])
        ff = (gate * jax.nn.sigmoid(gate) * up).astype(BF16)
        part = _dot(ff, wdown_ref[cols, :])
        down = part if down is None else down + part
    x = x + _rms(down, g_post_ref[...])
    ple = _dot(p_ref[...].astype(BF16), wple_ref[...]) * jax.nn.sigmoid(_dot(x.astype(BF16), wpg_ref[...]))
    o_ref[...] = x + _rms(ple, g_ple_ref[...])


def _ffn(x, p, g_pre, wgate, wup, wdown, g_post, wple, wpg, g_ple):
    T, D = x.shape
    tok = lambda w: pl.BlockSpec((TM, w), lambda i: (i, 0))
    vec = _resident((1, D))
    return pl.pallas_call(
        _ffn_kernel,
        grid=(T // TM,),
        in_specs=[tok(D), tok(PLE_DIM), vec, _resident(wgate.shape), _resident(wup.shape),
                  _resident(wdown.shape), vec, _resident(wple.shape), _resident(wpg.shape), vec],
        out_specs=tok(D),
        out_shape=jax.ShapeDtypeStruct((T, D), F32),
        compiler_params=_params("parallel"),
        name="ffn",
    )(x, p, g_pre, wgate, wup, wdown, g_post, wple, wpg, g_ple)


def _rope_tables(seq):
    inv = 1.0 / (ROPE_THETA ** (jnp.arange(0, HEAD_DIM, 2, dtype=F32) / HEAD_DIM))
    ang = jnp.arange(seq, dtype=F32)[:, None] * inv[None, :]
    cos, sin = jnp.cos(ang), jnp.sin(ang)
    reps = LANES // HEAD_DIM
    cos_t = jnp.tile(jnp.concatenate([cos, cos], axis=1), (1, reps))
    sin_t = jnp.tile(jnp.concatenate([-sin, sin], axis=1), (1, reps))
    return cos_t, sin_t


def kernel(x, p, g_mix_pre, w_in, b_f, w_br_a, w_br_b, w_br_c, w_out, g_mix_post, g_ffn_pre,
           w_ffn_gate, w_ffn_up, w_ffn_down, g_ffn_post, w_ple, w_ple_gate, g_ple_post):
    B, S, D = x.shape
    depth = w_in.shape[0]
    cos_t, sin_t = _rope_tables(S)
    n_f = N_HEADS_FOX * FORGET_PIECES
    row = lambda g: g.reshape(1, -1)
    for i in range(depth):
        w = w_in[i]
        wq = w[:, :MIX_WIDTH].astype(BF16)
        wk = w[:, MIX_WIDTH:2 * MIX_WIDTH].astype(BF16)
        wv = w[:, 2 * MIX_WIDTH:3 * MIX_WIDTH].astype(BF16)
        f0 = 3 * MIX_WIDTH
        wf = jnp.pad(jnp.repeat(w[:, f0:f0 + N_HEADS_FOX], FORGET_PIECES, axis=1),
                     ((0, 0), (0, LANES - n_f))).astype(BF16)
        bf = jnp.pad(jnp.repeat(b_f[i], FORGET_PIECES), (0, LANES - n_f)).reshape(1, LANES)
        wg = w[:, f0 + N_HEADS_FOX:].astype(BF16)

        q, k, v, kx, gates, kmean = _proj(x, row(g_mix_pre[i]), wq, wk, wv, wf, bf, wg, cos_t, sin_t)
        n_blocks = S // MOBA_BLOCK
        km = jnp.pad(kmean.reshape(B, n_blocks, MOBA_COLS).astype(BF16),
                     ((0, 0), (0, LANES - n_blocks), (0, 0)))
        y_a = _fox(q, k, kx, v)
        y_b = _dilated(q, k, v)
        y_c = _moba(q, k, v, km)

        T = B * S
        x2 = _merge(y_a.reshape(T, -1), y_b.reshape(T, -1), y_c.reshape(T, -1),
                    gates.reshape(T, -1), x.reshape(T, D),
                    w_br_a[i].astype(BF16), w_br_b[i].astype(BF16), w_br_c[i].astype(BF16),
                    w_out[i].astype(BF16), row(g_mix_post[i]))
        x3 = _ffn(x2, p[i].reshape(T, PLE_DIM), row(g_ffn_pre[i]),
                  w_ffn_gate[i].astype(BF16), w_ffn_up[i].astype(BF16), w_ffn_down[i].astype(BF16),
                  row(g_ffn_post[i]), w_ple[i].astype(BF16), w_ple_gate[i].astype(BF16),
                  row(g_ple_post[i]))
        x = x3.reshape(B, S, D)
    return x
```

```python
import functools
import math

import numpy as np
import jax
import jax.numpy as jnp
from jax import lax
from jax.experimental import pallas as pl
from jax.experimental.pallas import tpu as pltpu

D_MODEL = 1024
HEAD_DIM = 64
N_HEADS_FOX = 4
DIL_PATTERNS = ((128, 1), (512, 4), (2048, 16))
N_HEADS_MOBA = 6
MIX_WIDTH = 1024
MOBA_BLOCK = 256
MOBA_TOPK = 3
PLE_DIM = 256
D_FF = 2816
ROPE_THETA = 10000.0
RMS_EPS = 1e-6
NEG_INF = -1e30
LOG2E = math.log2(math.e)
Q_SCALE = HEAD_DIM ** -0.5 * LOG2E

LANES = 128
SUBLANES = 8
HEADS_PER_LANE_BLOCK = LANES // HEAD_DIM
N_LANE_BLOCKS = MIX_WIDTH // LANES
FOX_COLS = N_HEADS_FOX * HEAD_DIM
DIL_COLS = len(DIL_PATTERNS) * 2 * HEAD_DIM
MOBA_COLS = N_HEADS_MOBA * HEAD_DIM
N_GATE_COLS = 3 * D_MODEL
FORGET_PIECES = 3

TM = 512
TQ = 256
CHUNK = 2 * TQ
VMEM_LIMIT = 56 * 1024 * 1024

F32 = jnp.float32
BF16 = jnp.bfloat16


def _dot(a, b):
    return jnp.dot(a, b, preferred_element_type=F32)


def _dot_nt(a, b):
    return lax.dot_general(a, b, (((1,), (1,)), ((), ())), preferred_element_type=F32)


def _dot_tn(a, b):
    return lax.dot_general(a, b, (((0,), (0,)), ((), ())), preferred_element_type=F32)


def _rms(x, g):
    var = jnp.mean(x * x, axis=-1, keepdims=True)
    return x * lax.rsqrt(var + RMS_EPS) * g


def _split3(x):
    hi = x.astype(BF16).astype(F32)
    r1 = x - hi
    mid = r1.astype(BF16).astype(F32)
    lo = (r1 - mid).astype(BF16).astype(F32)
    return hi, mid, lo


def _resident(shape):
    nd = len(shape)
    return pl.BlockSpec(shape, lambda *_: (0,) * nd, pipeline_mode=pl.Buffered(1))


def _params(*sem):
    return pltpu.CompilerParams(dimension_semantics=sem, vmem_limit_bytes=VMEM_LIMIT)


def _proj_kernel(x_ref, g_ref, wq_ref, wk_ref, wv_ref, wvt_ref, wf_ref, bf_ref, wg_ref, cos_ref, sin_ref,
                 q_ref, k_ref, v_ref, vt_ref, kx_ref, gates_ref, kmean_ref, carry_ref):
    i = pl.program_id(1)
    hb = _rms(x_ref[0], g_ref[...]).astype(BF16)
    cos = cos_ref[...]
    sin = sin_ref[...]
    lane = lax.broadcasted_iota(jnp.int32, (TM, LANES), 1)
    first_half = (lane % HEAD_DIM) < (HEAD_DIM // 2)

    def rope(t):
        rot = jnp.where(first_half, pltpu.roll(t, LANES - HEAD_DIM // 2, 1),
                        pltpu.roll(t, HEAD_DIM // 2, 1))
        return t * cos + rot * sin

    first_rope_blk = FOX_COLS // LANES
    first_moba_blk = (FOX_COLS + DIL_COLS) // LANES

    q = _dot(hb, wq_ref[...])
    for c in range(N_LANE_BLOCKS):
        t = q[:, c * LANES:(c + 1) * LANES]
        if c >= first_rope_blk:
            t = rope(t)
        q_ref[0, :, c * LANES:(c + 1) * LANES] = (t * Q_SCALE).astype(BF16)

    k = _dot(hb, wk_ref[...])
    for c in range(N_LANE_BLOCKS):
        t = k[:, c * LANES:(c + 1) * LANES]
        if c >= first_rope_blk:
            t = rope(t)
        k_ref[0, :, c * LANES:(c + 1) * LANES] = t.astype(BF16)
        if c >= first_moba_blk:
            cm = c - first_moba_blk
            for r in range(TM // MOBA_BLOCK):
                kmean_ref[0, 0, r:r + 1, cm * LANES:(cm + 1) * LANES] = jnp.mean(
                    t[r * MOBA_BLOCK:(r + 1) * MOBA_BLOCK], axis=0, keepdims=True)

    v_ref[0] = _dot(hb, wv_ref[...]).astype(BF16)
    vt_ref[0] = _dot_nt(wvt_ref[...], hb).astype(BF16)

    f = _dot(hb, wf_ref[...]) + bf_ref[...]
    ls = jnp.minimum(f, 0.0) - jnp.log1p(jnp.exp(-jnp.abs(f)))
    ls = jnp.where(lane < N_HEADS_FOX, ls, 0.0)
    row = lax.broadcasted_iota(jnp.int32, (TM, TM), 0)
    col = lax.broadcasted_iota(jnp.int32, (TM, TM), 1)
    tri = jnp.where(col <= row, 1.0, 0.0).astype(BF16)
    hi, mid, lo = _split3(ls)
    cs = _dot(tri, hi.astype(BF16)) + _dot(tri, mid.astype(BF16)) + _dot(tri, lo.astype(BF16))

    @pl.when(i == 0)
    def _():
        carry_ref[...] = jnp.zeros_like(carry_ref)

    cs = cs + carry_ref[...]
    carry_ref[...] = cs[TM - 1:TM, :]
    hi, mid, lo = _split3(-LOG2E * cs)
    pieces = jnp.where(lane < N_HEADS_FOX, hi,
                       jnp.where(lane < 2 * N_HEADS_FOX, pltpu.roll(mid, N_HEADS_FOX, 1),
                                 pltpu.roll(lo, 2 * N_HEADS_FOX, 1)))
    kx_ref[0] = pieces.astype(BF16)

    for c in range(N_GATE_COLS // D_MODEL):
        z = _dot(hb, wg_ref[:, c * D_MODEL:(c + 1) * D_MODEL])
        gates_ref[0, :, c * D_MODEL:(c + 1) * D_MODEL] = jax.nn.sigmoid(z).astype(BF16)


def _proj(x, g, wq, wk, wv, wvt, wf, bf, wg, cos_t, sin_t):
    B, S, D = x.shape
    n_t = S // TM
    tok = lambda w: pl.BlockSpec((1, TM, w), lambda b, i: (b, i, 0))
    return pl.pallas_call(
        _proj_kernel,
        grid=(B, n_t),
        in_specs=[tok(D), _resident((1, D)), _resident((D, MIX_WIDTH)), _resident((D, MIX_WIDTH)),
                  _resident((D, MIX_WIDTH)), _resident((MIX_WIDTH, D)),
                  _resident((D, LANES)), _resident((1, LANES)), _resident((D, N_GATE_COLS)),
                  pl.BlockSpec((TM, LANES), lambda b, i: (i, 0)),
                  pl.BlockSpec((TM, LANES), lambda b, i: (i, 0))],
        out_specs=[tok(MIX_WIDTH), tok(MIX_WIDTH), tok(MIX_WIDTH),
                   pl.BlockSpec((1, MIX_WIDTH, TM), lambda b, i: (b, 0, i)),
                   tok(LANES), tok(N_GATE_COLS),
                   pl.BlockSpec((1, 1, TM // MOBA_BLOCK, MOBA_COLS), lambda b, i: (b, i, 0, 0))],
        out_shape=[jax.ShapeDtypeStruct((B, S, MIX_WIDTH), BF16)] * 3
        + [jax.ShapeDtypeStruct((B, MIX_WIDTH, S), BF16),
           jax.ShapeDtypeStruct((B, S, LANES), BF16),
           jax.ShapeDtypeStruct((B, S, N_GATE_COLS), BF16),
           jax.ShapeDtypeStruct((B, n_t, TM // MOBA_BLOCK, MOBA_COLS), F32)],
        scratch_shapes=[pltpu.VMEM((1, LANES), F32)],
        compiler_params=_params("arbitrary", "arbitrary"),
        name="proj",
    )(x, g, wq, wk, wv, wvt, wf, bf, wg, cos_t, sin_t)


def _reduce_keys(x, op, reduce):
    n = x.shape[0]
    while n > SUBLANES and n % (2 * SUBLANES) == 0:
        n //= 2
        x = op(x[:n], x[n:])
    return reduce(x, axis=0, keepdims=True)


def _softmax_stage(s, m, l):
    m_new = jnp.maximum(m, _reduce_keys(s, jnp.maximum, jnp.max))
    alpha = jnp.exp2(m - m_new)
    p = jnp.exp2(s - m_new)
    l_new = alpha * l + _reduce_keys(p, jnp.add, jnp.sum)
    return m_new, l_new, alpha, p.astype(BF16)


def _chunked_attention(i, scores, values, s_sc, p_sc):
    heads = range(HEADS_PER_LANE_BLOCK)
    n_full = (i * TQ) // CHUNK

    first = scores(0)
    for hl in heads:
        s_sc[hl] = first[hl]
        p_sc[hl] = jnp.zeros((CHUNK, TQ), BF16)

    def product(c):
        return [_dot(values(c, hl), p_sc[hl]) for hl in heads]

    def body(c, carry):
        m, l, alpha, acc = carry
        pv = product(jnp.maximum(c - 1, 0))
        s_next = scores(c + 1)
        stats = [_softmax_stage(s_sc[hl], m[hl], l[hl]) for hl in heads]
        for hl in heads:
            p_sc[hl] = stats[hl][3]
            s_sc[hl] = s_next[hl]
        acc = tuple(alpha[hl] * acc[hl] + pv[hl] for hl in heads)
        return (tuple(st[0] for st in stats), tuple(st[1] for st in stats),
                tuple(st[2] for st in stats), acc)

    row = lambda v: tuple(jnp.full((1, TQ), v, F32) for _ in heads)
    init = (row(NEG_INF), row(0.0), row(1.0), tuple(jnp.zeros((HEAD_DIM, TQ), F32) for _ in heads))
    m, l, alpha, acc = lax.fori_loop(0, n_full, body, init)

    pv = product(jnp.maximum(n_full - 1, 0))
    causal = _chunk_causal(n_full * CHUNK, i * TQ)
    out = []
    for hl in heads:
        _, l_h, a_h, p = _softmax_stage(jnp.where(causal, s_sc[hl], NEG_INF), m[hl], l[hl])
        acc_h = alpha[hl] * acc[hl] + pv[hl]
        out.append((l_h, a_h * acc_h + _dot(values(n_full, hl), p)))
    return out


def _query_t(q_ref):
    return q_ref[0].astype(F32).T


def _head_rows(q_t, hl):
    sub = lax.broadcasted_iota(jnp.int32, q_t.shape, 0)
    return jnp.where((sub >= hl * HEAD_DIM) & (sub < (hl + 1) * HEAD_DIM), q_t, 0.0)


def _finish(states):
    out_t = jnp.concatenate([acc / l for (l, acc) in states], axis=0)
    return out_t.T


def _chunk_scratch():
    return [pltpu.VMEM((HEADS_PER_LANE_BLOCK, CHUNK, TQ), F32),
            pltpu.VMEM((HEADS_PER_LANE_BLOCK, CHUNK, TQ), BF16)]


def _chunk_causal(first_key, first_query):
    kpos = first_key + lax.broadcasted_iota(jnp.int32, (CHUNK, TQ), 0)
    qpos = first_query + lax.broadcasted_iota(jnp.int32, (CHUNK, TQ), 1)
    return kpos <= qpos


def _fox_kernel(q_ref, k_ref, kx_ref, vt_ref, o_ref, s_sc, p_sc):
    p = pl.program_id(1)
    i = pl.program_id(2)
    q_t = _query_t(q_ref)
    sub = lax.broadcasted_iota(jnp.int32, (LANES, TQ), 0)
    q_aug = []
    for hl in range(HEADS_PER_LANE_BLOCK):
        head = HEADS_PER_LANE_BLOCK * p + hl
        ones = jnp.where((sub % N_HEADS_FOX == head) & (sub < FORGET_PIECES * N_HEADS_FOX), 1.0, 0.0)
        q_aug.append(jnp.concatenate([_head_rows(q_t, hl), ones], axis=0).astype(BF16))

    def scores(c):
        off = pl.multiple_of(c * CHUNK, CHUNK)
        k_aug = jnp.concatenate([k_ref[0, pl.ds(off, CHUNK), :], kx_ref[0, pl.ds(off, CHUNK), :]], axis=1)
        return tuple(_dot(k_aug, qa) for qa in q_aug)

    def values(c, hl):
        off = pl.multiple_of(c * CHUNK, CHUNK)
        return vt_ref[0, hl * HEAD_DIM:(hl + 1) * HEAD_DIM, pl.ds(off, CHUNK)]

    o_ref[0] = _finish(_chunked_attention(i, scores, values, s_sc, p_sc)).astype(BF16)


def _fox(q, k, kx, vt):
    B, S, _ = q.shape
    assert S % CHUNK == 0
    n_pairs = FOX_COLS // LANES
    return pl.pallas_call(
        _fox_kernel,
        grid=(B, n_pairs, S // TQ),
        in_specs=[pl.BlockSpec((1, TQ, LANES), lambda b, p, i: (b, i, p)),
                  pl.BlockSpec((1, S, LANES), lambda b, p, i: (b, 0, p)),
                  pl.BlockSpec((1, S, LANES), lambda b, p, i: (b, 0, 0)),
                  pl.BlockSpec((1, LANES, S), lambda b, p, i: (b, p, 0))],
        out_specs=pl.BlockSpec((1, TQ, LANES), lambda b, p, i: (b, i, p)),
        out_shape=jax.ShapeDtypeStruct((B, S, FOX_COLS), BF16),
        scratch_shapes=_chunk_scratch(),
        compiler_params=_params("parallel", "parallel", "arbitrary"),
        name="fox",
    )(q, k, kx, vt)


WINDOW_KEYS = DIL_PATTERNS[0][0] // DIL_PATTERNS[0][1]
assert all(w // d == WINDOW_KEYS for w, d in DIL_PATTERNS) and WINDOW_KEYS <= TQ


def _window_kernel(q_ref, kd_ref, kp_ref, vd_ref, vp_ref, o_ref, lse_ref):
    a = pl.program_id(2)
    q_t = _query_t(q_ref)
    kd, kp, vd, vp = kd_ref[0], kp_ref[0], vd_ref[0], vp_ref[0]
    kr = lax.broadcasted_iota(jnp.int32, (TQ, TQ), 0)
    qc = lax.broadcasted_iota(jnp.int32, (TQ, TQ), 1)
    valid_d = (kr <= qc) & (qc - kr <= WINDOW_KEYS)
    krp = lax.broadcasted_iota(jnp.int32, (WINDOW_KEYS, TQ), 0)
    qcp = lax.broadcasted_iota(jnp.int32, (WINDOW_KEYS, TQ), 1)
    valid_p = (krp >= qcp) & (a > 0)
    outs, lses = [], []
    q_m = [_head_rows(q_t, hl).astype(BF16) for hl in range(HEADS_PER_LANE_BLOCK)]
    raw = [(_dot(kd, qm), _dot(kp, qm)) for qm in q_m]
    for hl in range(HEADS_PER_LANE_BLOCK):
        s_d = jnp.where(valid_d, raw[hl][0], NEG_INF)
        s_p = jnp.where(valid_p, raw[hl][1], NEG_INF)
        m = jnp.maximum(_reduce_keys(s_d, jnp.maximum, jnp.max), _reduce_keys(s_p, jnp.maximum, jnp.max))
        p_d = jnp.exp2(s_d - m)
        p_p = jnp.exp2(s_p - m)
        l = _reduce_keys(p_d, jnp.add, jnp.sum) + _reduce_keys(p_p, jnp.add, jnp.sum)
        acc = _dot_tn(vd, p_d.astype(BF16)) + _dot_tn(vp, p_p.astype(BF16))
        outs.append(acc[hl * HEAD_DIM:(hl + 1) * HEAD_DIM] / l)
        lses.append(jnp.broadcast_to(m + jnp.log2(l), (HEAD_DIM, TQ)))
    o_ref[0] = jnp.concatenate(outs, axis=0).T
    lse_ref[0] = jnp.concatenate(lses, axis=0).T


def _window(q, k, v, group):
    B, S, _ = q.shape
    dil = DIL_PATTERNS[group][1]
    rows = S // dil
    assert rows % TQ == 0 and TQ % WINDOW_KEYS == 0
    blk = FOX_COLS // LANES + group
    view = lambda t: t.reshape(B, rows, dil * MIX_WIDTH)
    cur = pl.BlockSpec((1, TQ, LANES), lambda b, r, a: (b, a, r * N_LANE_BLOCKS + blk))
    per = TQ // WINDOW_KEYS
    prev = pl.BlockSpec((1, WINDOW_KEYS, LANES),
                        lambda b, r, a: (b, jnp.maximum(a * per - 1, 0), r * N_LANE_BLOCKS + blk))
    out = pl.BlockSpec((1, TQ, LANES), lambda b, r, a: (b, a, r))
    o, lse = pl.pallas_call(
        _window_kernel,
        grid=(B, dil, rows // TQ),
        in_specs=[cur, cur, prev, cur, prev],
        out_specs=[out, out],
        out_shape=[jax.ShapeDtypeStruct((B, rows, dil * LANES), F32)] * 2,
        compiler_params=_params("parallel", "parallel", "arbitrary"),
        name=f"window{group}",
    )(view(q), view(k), view(k), view(v), view(v))
    return o.reshape(B * S, LANES), lse.reshape(B * S, LANES)


def _moba_kernel(q_ref, k_ref, vt_ref, km_ref, o_ref, s_sc, p_sc):
    i = pl.program_id(2)
    q_t = _query_t(q_ref)
    km = km_ref[0]
    blk = lax.broadcasted_iota(jnp.int32, (LANES, TQ), 0)
    past = blk < i
    q_aug = []
    for hl in range(HEADS_PER_LANE_BLOCK):
        qm = _head_rows(q_t, hl)
        gate = jnp.where(past, _dot(km, qm.astype(BF16)), NEG_INF)
        sel = blk == i
        for _ in range(MOBA_TOPK):
            top = jnp.max(gate, axis=0, keepdims=True)
            idx = jnp.min(jnp.where(gate == top, blk, LANES), axis=0, keepdims=True)
            hit = blk == idx
            sel = sel | (hit & past)
            gate = jnp.where(hit, -jnp.inf, gate)
        sel_bias = jnp.where(sel, 0.0, NEG_INF)
        q_aug.append(jnp.concatenate([qm, sel_bias], axis=0).astype(BF16))

    row = lax.broadcasted_iota(jnp.int32, (CHUNK, LANES), 0)
    lane = lax.broadcasted_iota(jnp.int32, (CHUNK, LANES), 1)
    blocks_per_chunk = CHUNK // MOBA_BLOCK

    def scores(c):
        off = pl.multiple_of(c * CHUNK, CHUNK)
        onehot = jnp.where(lane == c * blocks_per_chunk + row // MOBA_BLOCK, 1.0, 0.0).astype(BF16)
        k_aug = jnp.concatenate([k_ref[0, pl.ds(off, CHUNK), :], onehot], axis=1)
        return tuple(_dot(k_aug, qa) for qa in q_aug)

    def values(c, hl):
        off = pl.multiple_of(c * CHUNK, CHUNK)
        return vt_ref[0, hl * HEAD_DIM:(hl + 1) * HEAD_DIM, pl.ds(off, CHUNK)]

    o_ref[0] = _finish(_chunked_attention(i, scores, values, s_sc, p_sc)).astype(BF16)


def _moba(q, k, vt, km):
    B, S, _ = q.shape
    assert TQ == MOBA_BLOCK and S % CHUNK == 0 and S // MOBA_BLOCK <= LANES
    n_pairs = MOBA_COLS // LANES
    first = (FOX_COLS + DIL_COLS) // LANES
    return pl.pallas_call(
        _moba_kernel,
        grid=(B, n_pairs, S // TQ),
        in_specs=[pl.BlockSpec((1, TQ, LANES), lambda b, p, i: (b, i, first + p)),
                  pl.BlockSpec((1, S, LANES), lambda b, p, i: (b, 0, first + p)),
                  pl.BlockSpec((1, LANES, S), lambda b, p, i: (b, first + p, 0)),
                  pl.BlockSpec((1, LANES, LANES), lambda b, p, i: (b, 0, p))],
        out_specs=pl.BlockSpec((1, TQ, LANES), lambda b, p, i: (b, i, p)),
        out_shape=jax.ShapeDtypeStruct((B, S, MOBA_COLS), BF16),
        scratch_shapes=_chunk_scratch(),
        compiler_params=_params("parallel", "parallel", "arbitrary"),
        name="moba",
    )(q, k, vt, km)


def _merge_kernel(ya_ref, o0_ref, o1_ref, o2_ref, l0_ref, l1_ref, l2_ref, yc_ref, gates_ref, x_ref,
                  wa_ref, wb_ref, wc_ref, wo_ref, g_ref, o_ref):
    lses = [l0_ref[...], l1_ref[...], l2_ref[...]]
    top = jnp.maximum(jnp.maximum(lses[0], lses[1]), lses[2])
    wts = [jnp.exp2(t - top) for t in lses]
    den = wts[0] + wts[1] + wts[2]
    yb = (wts[0] * o0_ref[...] + wts[1] * o1_ref[...] + wts[2] * o2_ref[...]) / den
    merged = None
    for c, (y, w_ref) in enumerate(((ya_ref[...], wa_ref), (yb.astype(BF16), wb_ref), (yc_ref[...], wc_ref))):
        gate = gates_ref[:, c * D_MODEL:(c + 1) * D_MODEL].astype(F32)
        term = gate * _dot(y, w_ref[...])
        merged = term if merged is None else merged + term
    out = _dot(merged.astype(BF16), wo_ref[...])
    o_ref[...] = x_ref[...] + _rms(out, g_ref[...])


def _merge(ya, ob, lb, yc, gates, x, wa, wb, wc, wo, g):
    T, D = x.shape
    tok = lambda w: pl.BlockSpec((TM, w), lambda i: (i, 0))
    return pl.pallas_call(
        _merge_kernel,
        grid=(T // TM,),
        in_specs=[tok(FOX_COLS)] + [tok(LANES)] * 6 + [tok(MOBA_COLS), tok(N_GATE_COLS), tok(D),
                  _resident(wa.shape), _resident(wb.shape), _resident(wc.shape), _resident(wo.shape),
                  _resident((1, D))],
        out_specs=tok(D),
        out_shape=jax.ShapeDtypeStruct((T, D), F32),
        compiler_params=_params("parallel"),
        name="merge",
    )(ya, *ob, *lb, yc, gates, x, wa, wb, wc, wo, g)


FF_CHUNK = D_FF // 2


def _ffn_kernel(x_ref, p_ref, g_pre_ref, wgate_ref, wup_ref, wdown_ref, g_post_ref,
                wple_ref, wpg_ref, g_ple_ref, o_ref):
    x = x_ref[...]
    hb = _rms(x, g_pre_ref[...]).astype(BF16)
    down = None
    for c in range(D_FF // FF_CHUNK):
        cols = slice(c * FF_CHUNK, (c + 1) * FF_CHUNK)
        gate = _dot(hb, wgate_ref[:, cols])
        up = _dot(hb, wup_ref[:, cols])
        ff = (gate * jax.nn.sigmoid(gate) * up).astype(BF16)
        part = _dot(ff, wdown_ref[cols, :])
        down = part if down is None else down + part
    x = x + _rms(down, g_post_ref[...])
    ple = _dot(p_ref[...].astype(BF16), wple_ref[...]) * jax.nn.sigmoid(_dot(x.astype(BF16), wpg_ref[...]))
    o_ref[...] = x + _rms(ple, g_ple_ref[...])


def _ffn(x, p, g_pre, wgate, wup, wdown, g_post, wple, wpg, g_ple):
    T, D = x.shape
    tok = lambda w: pl.BlockSpec((TM, w), lambda i: (i, 0))
    vec = _resident((1, D))
    return pl.pallas_call(
        _ffn_kernel,
        grid=(T // TM,),
        in_specs=[tok(D), tok(PLE_DIM), vec, _resident(wgate.shape), _resident(wup.shape),
                  _resident(wdown.shape), vec, _resident(wple.shape), _resident(wpg.shape), vec],
        out_specs=tok(D),
        out_shape=jax.ShapeDtypeStruct((T, D), F32),
        compiler_params=_params("parallel"),
        name="ffn",
    )(x, p, g_pre, wgate, wup, wdown, g_post, wple, wpg, g_ple)


def _rope_tables(seq):
    inv = 1.0 / (ROPE_THETA ** (jnp.arange(0, HEAD_DIM, 2, dtype=F32) / HEAD_DIM))
    ang = jnp.arange(seq, dtype=F32)[:, None] * inv[None, :]
    cos, sin = jnp.cos(ang), jnp.sin(ang)
    reps = LANES // HEAD_DIM
    cos_t = jnp.tile(jnp.concatenate([cos, cos], axis=1), (1, reps))
    sin_t = jnp.tile(jnp.concatenate([-sin, sin], axis=1), (1, reps))
    return cos_t, sin_t


def kernel(x, p, g_mix_pre, w_in, b_f, w_br_a, w_br_b, w_br_c, w_out, g_mix_post, g_ffn_pre,
           w_ffn_gate, w_ffn_up, w_ffn_down, g_ffn_post, w_ple, w_ple_gate, g_ple_post):
    B, S, D = x.shape
    T = B * S
    depth = w_in.shape[0]
    cos_t, sin_t = _rope_tables(S)
    row = lambda g: g.reshape(1, -1)
    f0 = 3 * MIX_WIDTH
    for i in range(depth):
        wq = w_in[i, :, :MIX_WIDTH].astype(BF16)
        wk = w_in[i, :, MIX_WIDTH:2 * MIX_WIDTH].astype(BF16)
        wv = w_in[i, :, 2 * MIX_WIDTH:f0].astype(BF16)
        wvt = wv.T
        wf = jnp.pad(w_in[i, :, f0:f0 + N_HEADS_FOX], ((0, 0), (0, LANES - N_HEADS_FOX))).astype(BF16)
        bf = jnp.pad(b_f[i], (0, LANES - N_HEADS_FOX)).reshape(1, LANES)
        wg = w_in[i, :, f0 + N_HEADS_FOX:].astype(BF16)

        q, k, v, vt, kx, gates, kmean = _proj(x, row(g_mix_pre[i]), wq, wk, wv, wvt, wf, bf, wg, cos_t, sin_t)
        n_blocks = S // MOBA_BLOCK
        km = jnp.pad(kmean.reshape(B, n_blocks, MOBA_COLS).astype(BF16),
                     ((0, 0), (0, LANES - n_blocks), (0, 0)))
        y_a = _fox(q, k, kx, vt)
        win = [_window(q, k, v, g) for g in range(len(DIL_PATTERNS))]
        y_c = _moba(q, k, vt, km)

        x2 = _merge(y_a.reshape(T, -1), [o for o, _ in win], [l for _, l in win], y_c.reshape(T, -1),
                    gates.reshape(T, -1), x.reshape(T, D),
                    w_br_a[i].astype(BF16), w_br_b[i].astype(BF16), w_br_c[i].astype(BF16),
                    w_out[i].astype(BF16), row(g_mix_post[i]))
        x3 = _ffn(x2, p[i].reshape(T, PLE_DIM), row(g_ffn_pre[i]),
                  w_ffn_gate[i].astype(BF16), w_ffn_up[i].astype(BF16), w_ffn_down[i].astype(BF16),
                  row(g_ffn_post[i]), w_ple[i].astype(BF16), w_ple_gate[i].astype(BF16),
                  row(g_ple_post[i]))
        x = x3.reshape(B, S, D)
    return x
```

```python
import functools
import math

import numpy as np
import jax
import jax.numpy as jnp
from jax import lax
from jax.experimental import pallas as pl
from jax.experimental.pallas import tpu as pltpu

D_MODEL = 1024
HEAD_DIM = 64
N_HEADS_FOX = 4
DIL_PATTERNS = ((128, 1), (512, 4), (2048, 16))
N_HEADS_MOBA = 6
MIX_WIDTH = 1024
MOBA_BLOCK = 256
MOBA_TOPK = 3
PLE_DIM = 256
D_FF = 2816
ROPE_THETA = 10000.0
RMS_EPS = 1e-6
NEG_INF = -1e30
LOG2E = math.log2(math.e)
Q_SCALE = HEAD_DIM ** -0.5 * LOG2E

LANES = 128
SUBLANES = 8
HEADS_PER_LANE_BLOCK = LANES // HEAD_DIM
N_LANE_BLOCKS = MIX_WIDTH // LANES
FOX_COLS = N_HEADS_FOX * HEAD_DIM
DIL_COLS = len(DIL_PATTERNS) * 2 * HEAD_DIM
MOBA_COLS = N_HEADS_MOBA * HEAD_DIM
N_GATE_COLS = 3 * D_MODEL
FORGET_PIECES = 3

TM = 512
TQ = 512
CHUNK = 512
VMEM_LIMIT = 56 * 1024 * 1024

F32 = jnp.float32
BF16 = jnp.bfloat16


def _dot(a, b):
    return jnp.dot(a, b, preferred_element_type=F32)


def _dot_nt(a, b):
    return lax.dot_general(a, b, (((1,), (1,)), ((), ())), preferred_element_type=F32)


def _dot_tn(a, b):
    return lax.dot_general(a, b, (((0,), (0,)), ((), ())), preferred_element_type=F32)


def _rms(x, g):
    var = jnp.mean(x * x, axis=-1, keepdims=True)
    return x * lax.rsqrt(var + RMS_EPS) * g


def _split3(x):
    hi = x.astype(BF16).astype(F32)
    r1 = x - hi
    mid = r1.astype(BF16).astype(F32)
    lo = (r1 - mid).astype(BF16).astype(F32)
    return hi, mid, lo


def _resident(shape):
    nd = len(shape)
    return pl.BlockSpec(shape, lambda *_: (0,) * nd, pipeline_mode=pl.Buffered(1))


def _params(*sem):
    return pltpu.CompilerParams(dimension_semantics=sem, vmem_limit_bytes=VMEM_LIMIT)


STRIDED_GROUPS = tuple((g, dil) for g, (_, dil) in enumerate(DIL_PATTERNS) if dil > 1)


def _proj_kernel(x_ref, g_ref, wq_ref, wk_ref, wv_ref, wvt_ref, wf_ref, bf_ref, wg_ref, cos_ref, sin_ref,
                 q_ref, k_ref, v_ref, vt_ref, kx_ref, gates_ref, kmean_ref, *rest):
    res_refs, (carry_ref, perm_sc) = rest[:-2], rest[-2:]
    i = pl.program_id(1)

    def scatter_residues(t, which, blk):
        for n, (g, dil) in enumerate(STRIDED_GROUPS):
            if blk == FOX_COLS // LANES + g:
                slot = 3 * n + which
                perm_sc[slot] = t
                for r in range(dil):
                    res_refs[slot][0, :, r * LANES:(r + 1) * LANES] = perm_sc[
                        slot, pl.ds(r, TM // dil, stride=dil), :].astype(BF16)

    hb = _rms(x_ref[0], g_ref[...]).astype(BF16)
    cos = cos_ref[...]
    sin = sin_ref[...]
    lane = lax.broadcasted_iota(jnp.int32, (TM, LANES), 1)
    first_half = (lane % HEAD_DIM) < (HEAD_DIM // 2)

    def rope(t):
        rot = jnp.where(first_half, pltpu.roll(t, LANES - HEAD_DIM // 2, 1),
                        pltpu.roll(t, HEAD_DIM // 2, 1))
        return t * cos + rot * sin

    first_rope_blk = FOX_COLS // LANES
    first_moba_blk = (FOX_COLS + DIL_COLS) // LANES

    q = _dot(hb, wq_ref[...])
    for c in range(N_LANE_BLOCKS):
        t = q[:, c * LANES:(c + 1) * LANES]
        if c >= first_rope_blk:
            t = rope(t)
        t = t * Q_SCALE
        q_ref[0, :, c * LANES:(c + 1) * LANES] = t.astype(BF16)
        scatter_residues(t, 0, c)

    k = _dot(hb, wk_ref[...])
    for c in range(N_LANE_BLOCKS):
        t = k[:, c * LANES:(c + 1) * LANES]
        if c >= first_rope_blk:
            t = rope(t)
        k_ref[0, :, c * LANES:(c + 1) * LANES] = t.astype(BF16)
        scatter_residues(t, 1, c)
        if c >= first_moba_blk:
            cm = c - first_moba_blk
            for r in range(TM // MOBA_BLOCK):
                kmean_ref[0, 0, r:r + 1, cm * LANES:(cm + 1) * LANES] = jnp.mean(
                    t[r * MOBA_BLOCK:(r + 1) * MOBA_BLOCK], axis=0, keepdims=True)

    v = _dot(hb, wv_ref[...])
    v_ref[0] = v.astype(BF16)
    for c in range(N_LANE_BLOCKS):
        scatter_residues(v[:, c * LANES:(c + 1) * LANES], 2, c)
    vt_ref[0] = _dot_nt(wvt_ref[...], hb).astype(BF16)

    f = _dot(hb, wf_ref[...]) + bf_ref[...]
    ls = jnp.minimum(f, 0.0) - jnp.log1p(jnp.exp(-jnp.abs(f)))
    ls = jnp.where(lane < N_HEADS_FOX, ls, 0.0)
    row = lax.broadcasted_iota(jnp.int32, (TM, TM), 0)
    col = lax.broadcasted_iota(jnp.int32, (TM, TM), 1)
    tri = jnp.where(col <= row, 1.0, 0.0).astype(BF16)
    hi, mid, lo = _split3(ls)
    cs = _dot(tri, hi.astype(BF16)) + _dot(tri, mid.astype(BF16)) + _dot(tri, lo.astype(BF16))

    @pl.when(i == 0)
    def _():
        carry_ref[...] = jnp.zeros_like(carry_ref)

    cs = cs + carry_ref[...]
    carry_ref[...] = cs[TM - 1:TM, :]
    hi, mid, lo = _split3(-LOG2E * cs)
    pieces = jnp.where(lane < N_HEADS_FOX, hi,
                       jnp.where(lane < 2 * N_HEADS_FOX, pltpu.roll(mid, N_HEADS_FOX, 1),
                                 pltpu.roll(lo, 2 * N_HEADS_FOX, 1)))
    kx_ref[0] = pieces.astype(BF16)

    for c in range(N_GATE_COLS // D_MODEL):
        z = _dot(hb, wg_ref[:, c * D_MODEL:(c + 1) * D_MODEL])
        gates_ref[0, :, c * D_MODEL:(c + 1) * D_MODEL] = jax.nn.sigmoid(z).astype(BF16)


def _proj(x, g, wq, wk, wv, wvt, wf, bf, wg, cos_t, sin_t):
    B, S, D = x.shape
    n_t = S // TM
    tok = lambda w: pl.BlockSpec((1, TM, w), lambda b, i: (b, i, 0))
    res_specs, res_shapes = [], []
    for _, dil in STRIDED_GROUPS:
        assert TM % (dil * 2 * SUBLANES) == 0 and S % dil == 0
        res_specs += [pl.BlockSpec((1, TM // dil, dil * LANES), lambda b, i: (b, i, 0))] * 3
        res_shapes += [jax.ShapeDtypeStruct((B, S // dil, dil * LANES), BF16)] * 3
    outs = pl.pallas_call(
        _proj_kernel,
        grid=(B, n_t),
        in_specs=[tok(D), _resident((1, D)), _resident((D, MIX_WIDTH)), _resident((D, MIX_WIDTH)),
                  _resident((D, MIX_WIDTH)), _resident((MIX_WIDTH, D)),
                  _resident((D, LANES)), _resident((1, LANES)), _resident((D, N_GATE_COLS)),
                  pl.BlockSpec((TM, LANES), lambda b, i: (i, 0)),
                  pl.BlockSpec((TM, LANES), lambda b, i: (i, 0))],
        out_specs=[tok(MIX_WIDTH), tok(MIX_WIDTH), tok(MIX_WIDTH),
                   pl.BlockSpec((1, MIX_WIDTH, TM), lambda b, i: (b, 0, i)),
                   tok(LANES), tok(N_GATE_COLS),
                   pl.BlockSpec((1, 1, TM // MOBA_BLOCK, MOBA_COLS), lambda b, i: (b, i, 0, 0))] + res_specs,
        out_shape=[jax.ShapeDtypeStruct((B, S, MIX_WIDTH), BF16)] * 3
        + [jax.ShapeDtypeStruct((B, MIX_WIDTH, S), BF16),
           jax.ShapeDtypeStruct((B, S, LANES), BF16),
           jax.ShapeDtypeStruct((B, S, N_GATE_COLS), BF16),
           jax.ShapeDtypeStruct((B, n_t, TM // MOBA_BLOCK, MOBA_COLS), F32)] + res_shapes,
        scratch_shapes=[pltpu.VMEM((1, LANES), F32), pltpu.VMEM((len(res_specs), TM, LANES), F32)],
        compiler_params=_params("arbitrary", "arbitrary"),
        name="proj",
    )(x, g, wq, wk, wv, wvt, wf, bf, wg, cos_t, sin_t)
    return outs[:7], outs[7:]


def _reduce_keys(x, op, reduce):
    n = x.shape[0]
    while n > SUBLANES and n % (2 * SUBLANES) == 0:
        n //= 2
        x = op(x[:n], x[n:])
    return reduce(x, axis=0, keepdims=True)


def _key_max(s):
    return _reduce_keys(s, jnp.maximum, jnp.max)


def _probabilities(s, m):
    return jnp.exp2((s - m).astype(BF16))


ACC_ROWS = HEAD_DIM + 2 * SUBLANES


def _values_with_ones(vt):
    return jnp.concatenate([vt, jnp.ones((ACC_ROWS - HEAD_DIM, vt.shape[1]), BF16)], axis=0)


def _chunked_attention(i, scores, values, s_sc, p_sc):
    heads = range(HEADS_PER_LANE_BLOCK)
    n_full = (i * TQ) // CHUNK

    causal = _chunk_causal(n_full * CHUNK, i * TQ)
    s_diag = scores(n_full)
    s_first = scores(0)
    m, alpha = [], []
    for hl in heads:
        sd = jnp.where(causal, s_diag[hl], NEG_INF)
        m_diag = _key_max(sd)
        p_sc[hl] = _probabilities(sd, m_diag)
        sf = jnp.where(n_full > 0, s_first[hl], NEG_INF)
        s_sc[hl] = sf
        m.append(jnp.maximum(m_diag, _key_max(sf)))
        alpha.append(jnp.exp2(m_diag - m[hl]))

    def product(c, p):
        return [_dot(_values_with_ones(values(c, hl)), p[hl]) for hl in heads]

    def before(c):
        return jnp.where(c <= 0, n_full, c - 1)

    def body(c, carry):
        m, alpha, acc = carry
        pv = product(before(c), [p_sc[hl] for hl in heads])
        s_next = scores(c + 1)
        m_next = tuple(jnp.maximum(m[hl], _key_max(s_next[hl])) for hl in heads)
        alpha_next = tuple(jnp.exp2(m[hl] - m_next[hl]) for hl in heads)
        for hl in heads:
            p_sc[hl] = _probabilities(s_sc[hl], m[hl])
            s_sc[hl] = s_next[hl]
        acc = tuple(alpha[hl] * (acc[hl] + pv[hl]) for hl in heads)
        return m_next, alpha_next, acc

    init = (tuple(m), tuple(alpha), tuple(jnp.zeros((ACC_ROWS, TQ), F32) for _ in heads))
    m, alpha, acc = lax.fori_loop(0, n_full - 1, body, init)

    last = n_full - 1
    pv = product(before(last), [p_sc[hl] for hl in heads])
    tail = product(jnp.maximum(last, 0), [_probabilities(s_sc[hl], m[hl]) for hl in heads])
    return [alpha[hl] * (acc[hl] + pv[hl]) + tail[hl] for hl in heads]


def _query_t(q_ref):
    return q_ref[0].astype(F32).T


def _head_rows(q_t, hl):
    sub = lax.broadcasted_iota(jnp.int32, q_t.shape, 0)
    return jnp.where((sub >= hl * HEAD_DIM) & (sub < (hl + 1) * HEAD_DIM), q_t, 0.0)


def _finish(accs):
    out_t = jnp.concatenate([acc[:HEAD_DIM] / acc[HEAD_DIM:HEAD_DIM + 1] for acc in accs], axis=0)
    return out_t.T


def _chunk_scratch():
    return [pltpu.VMEM((HEADS_PER_LANE_BLOCK, CHUNK, TQ), F32),
            pltpu.VMEM((HEADS_PER_LANE_BLOCK, CHUNK, TQ), BF16)]


def _chunk_causal(first_key, first_query):
    kpos = first_key + lax.broadcasted_iota(jnp.int32, (CHUNK, TQ), 0)
    qpos = first_query + lax.broadcasted_iota(jnp.int32, (CHUNK, TQ), 1)
    return kpos <= qpos


def _fox_kernel(q_ref, k_ref, kx_ref, vt_ref, o_ref, s_sc, p_sc):
    p = pl.program_id(1)
    i = pl.program_id(2)
    q_t = _query_t(q_ref)
    sub = lax.broadcasted_iota(jnp.int32, (LANES, TQ), 0)
    q_aug = []
    for hl in range(HEADS_PER_LANE_BLOCK):
        head = HEADS_PER_LANE_BLOCK * p + hl
        ones = jnp.where((sub % N_HEADS_FOX == head) & (sub < FORGET_PIECES * N_HEADS_FOX), 1.0, 0.0)
        q_aug.append(jnp.concatenate([_head_rows(q_t, hl), ones], axis=0).astype(BF16))

    def scores(c):
        off = pl.multiple_of(c * CHUNK, CHUNK)
        k_aug = jnp.concatenate([k_ref[0, pl.ds(off, CHUNK), :], kx_ref[0, pl.ds(off, CHUNK), :]], axis=1)
        return tuple(_dot(k_aug, qa) for qa in q_aug)

    def values(c, hl):
        off = pl.multiple_of(c * CHUNK, CHUNK)
        return vt_ref[0, hl * HEAD_DIM:(hl + 1) * HEAD_DIM, pl.ds(off, CHUNK)]

    o_ref[0] = _finish(_chunked_attention(i, scores, values, s_sc, p_sc)).astype(BF16)


def _fox(q, k, kx, vt):
    B, S, _ = q.shape
    assert S % CHUNK == 0
    n_pairs = FOX_COLS // LANES
    return pl.pallas_call(
        _fox_kernel,
        grid=(B, n_pairs, S // TQ),
        in_specs=[pl.BlockSpec((1, TQ, LANES), lambda b, p, i: (b, i, p)),
                  pl.BlockSpec((1, S, LANES), lambda b, p, i: (b, 0, p)),
                  pl.BlockSpec((1, S, LANES), lambda b, p, i: (b, 0, 0)),
                  pl.BlockSpec((1, LANES, S), lambda b, p, i: (b, p, 0))],
        out_specs=pl.BlockSpec((1, TQ, LANES), lambda b, p, i: (b, i, p)),
        out_shape=jax.ShapeDtypeStruct((B, S, FOX_COLS), BF16),
        scratch_shapes=_chunk_scratch(),
        compiler_params=_params("parallel", "parallel", "arbitrary"),
        name="fox",
    )(q, k, kx, vt)


WINDOW_KEYS = DIL_PATTERNS[0][0] // DIL_PATTERNS[0][1]
WQ = 256
assert all(w // d == WINDOW_KEYS for w, d in DIL_PATTERNS) and WINDOW_KEYS <= WQ


def _window_kernel(q_ref, kd_ref, kp_ref, vd_ref, vp_ref, o_ref, lse_ref, *, dil):
    a = pl.program_id(1)
    r = pl.program_id(2)
    q_t = _query_t(q_ref)
    kd, kp, vd, vp = kd_ref[0], kp_ref[0], vd_ref[0], vp_ref[0]
    kr = lax.broadcasted_iota(jnp.int32, (WQ, WQ), 0)
    qc = lax.broadcasted_iota(jnp.int32, (WQ, WQ), 1)
    valid_d = (kr <= qc) & (qc - kr <= WINDOW_KEYS)
    krp = lax.broadcasted_iota(jnp.int32, (WINDOW_KEYS, WQ), 0)
    qcp = lax.broadcasted_iota(jnp.int32, (WINDOW_KEYS, WQ), 1)
    valid_p = (krp >= qcp) & (a > 0)
    outs, lses = [], []
    q_m = [_head_rows(q_t, hl).astype(BF16) for hl in range(HEADS_PER_LANE_BLOCK)]
    raw = [(_dot(kd, qm), _dot(kp, qm)) for qm in q_m]
    for hl in range(HEADS_PER_LANE_BLOCK):
        s_d = jnp.where(valid_d, raw[hl][0], NEG_INF)
        s_p = jnp.where(valid_p, raw[hl][1], NEG_INF)
        m = jnp.maximum(_reduce_keys(s_d, jnp.maximum, jnp.max), _reduce_keys(s_p, jnp.maximum, jnp.max))
        p_d = jnp.exp2(s_d - m)
        p_p = jnp.exp2(s_p - m)
        l = _reduce_keys(p_d, jnp.add, jnp.sum) + _reduce_keys(p_p, jnp.add, jnp.sum)
        acc = _dot_tn(vd, p_d.astype(BF16)) + _dot_tn(vp, p_p.astype(BF16))
        outs.append(acc[hl * HEAD_DIM:(hl + 1) * HEAD_DIM] / l)
        lses.append(jnp.broadcast_to(m + jnp.log2(l), (HEAD_DIM, WQ)))
    rows = pl.ds(r, WQ, stride=dil) if dil > 1 else pl.ds(0, WQ)
    o_ref[0, rows, :] = jnp.concatenate(outs, axis=0).T
    lse_ref[0, rows, :] = jnp.concatenate(lses, axis=0).T


def _window(q, k, v, group):
    dil = DIL_PATTERNS[group][1]
    B, rows, width = q.shape
    S = rows * dil
    assert rows % WQ == 0 and WQ % WINDOW_KEYS == 0
    col0, col_step = (FOX_COLS // LANES + group, 0) if dil == 1 else (0, 1)
    assert width == (MIX_WIDTH if dil == 1 else dil * LANES)
    cur = pl.BlockSpec((1, WQ, LANES), lambda b, a, r: (b, a, col0 + col_step * r))
    per = WQ // WINDOW_KEYS
    prev = pl.BlockSpec((1, WINDOW_KEYS, LANES),
                        lambda b, a, r: (b, jnp.maximum(a * per - 1, 0), col0 + col_step * r))
    out = pl.BlockSpec((1, dil * WQ, LANES), lambda b, a, r: (b, a, 0))
    o, lse = pl.pallas_call(
        functools.partial(_window_kernel, dil=dil),
        grid=(B, rows // WQ, dil),
        in_specs=[cur, cur, prev, cur, prev],
        out_specs=[out, out],
        out_shape=[jax.ShapeDtypeStruct((B, S, LANES), F32)] * 2,
        compiler_params=_params("parallel", "arbitrary", "arbitrary"),
        name=f"window{group}",
    )(q, k, k, v, v)
    return o.reshape(B * S, LANES), lse.reshape(B * S, LANES)


def _moba_kernel(q_ref, k_ref, vt_ref, km_ref, o_ref, s_sc, p_sc):
    i = pl.program_id(2)
    q_t = _query_t(q_ref)
    km = km_ref[0]
    blk = lax.broadcasted_iota(jnp.int32, (LANES, TQ), 0)
    own = i * (TQ // MOBA_BLOCK) + lax.broadcasted_iota(jnp.int32, (LANES, TQ), 1) // MOBA_BLOCK
    past = blk < own
    q_aug = []
    for hl in range(HEADS_PER_LANE_BLOCK):
        qm = _head_rows(q_t, hl)
        gate = jnp.where(past, _dot(km, qm.astype(BF16)), NEG_INF)
        sel = blk == own
        for _ in range(MOBA_TOPK):
            top = jnp.max(gate, axis=0, keepdims=True)
            idx = jnp.min(jnp.where(gate == top, blk, LANES), axis=0, keepdims=True)
            hit = blk == idx
            sel = sel | (hit & past)
            gate = jnp.where(hit, -jnp.inf, gate)
        sel_bias = jnp.where(sel, 0.0, NEG_INF)
        q_aug.append(jnp.concatenate([qm, sel_bias], axis=0).astype(BF16))

    row = lax.broadcasted_iota(jnp.int32, (CHUNK, LANES), 0)
    lane = lax.broadcasted_iota(jnp.int32, (CHUNK, LANES), 1)
    blocks_per_chunk = CHUNK // MOBA_BLOCK

    def scores(c):
        off = pl.multiple_of(c * CHUNK, CHUNK)
        onehot = jnp.where(lane == c * blocks_per_chunk + row // MOBA_BLOCK, 1.0, 0.0).astype(BF16)
        k_aug = jnp.concatenate([k_ref[0, pl.ds(off, CHUNK), :], onehot], axis=1)
        return tuple(_dot(k_aug, qa) for qa in q_aug)

    def values(c, hl):
        off = pl.multiple_of(c * CHUNK, CHUNK)
        return vt_ref[0, hl * HEAD_DIM:(hl + 1) * HEAD_DIM, pl.ds(off, CHUNK)]

    o_ref[0] = _finish(_chunked_attention(i, scores, values, s_sc, p_sc)).astype(BF16)


def _moba(q, k, vt, km):
    B, S, _ = q.shape
    assert TQ % MOBA_BLOCK == 0 and CHUNK % MOBA_BLOCK == 0 and S % CHUNK == 0 and S // MOBA_BLOCK <= LANES
    n_pairs = MOBA_COLS // LANES
    first = (FOX_COLS + DIL_COLS) // LANES
    return pl.pallas_call(
        _moba_kernel,
        grid=(B, n_pairs, S // TQ),
        in_specs=[pl.BlockSpec((1, TQ, LANES), lambda b, p, i: (b, i, first + p)),
                  pl.BlockSpec((1, S, LANES), lambda b, p, i: (b, 0, first + p)),
                  pl.BlockSpec((1, LANES, S), lambda b, p, i: (b, first + p, 0)),
                  pl.BlockSpec((1, LANES, LANES), lambda b, p, i: (b, 0, p))],
        out_specs=pl.BlockSpec((1, TQ, LANES), lambda b, p, i: (b, i, p)),
        out_shape=jax.ShapeDtypeStruct((B, S, MOBA_COLS), BF16),
        scratch_shapes=_chunk_scratch(),
        compiler_params=_params("parallel", "parallel", "arbitrary"),
        name="moba",
    )(q, k, vt, km)


def _merge_kernel(ya_ref, o0_ref, o1_ref, o2_ref, l0_ref, l1_ref, l2_ref, yc_ref, gates_ref, x_ref,
                  wa_ref, wb_ref, wc_ref, wo_ref, g_ref, o_ref):
    lses = [l0_ref[...], l1_ref[...], l2_ref[...]]
    top = jnp.maximum(jnp.maximum(lses[0], lses[1]), lses[2])
    wts = [jnp.exp2(t - top) for t in lses]
    den = wts[0] + wts[1] + wts[2]
    yb = (wts[0] * o0_ref[...] + wts[1] * o1_ref[...] + wts[2] * o2_ref[...]) / den
    merged = None
    for c, (y, w_ref) in enumerate(((ya_ref[...], wa_ref), (yb.astype(BF16), wb_ref), (yc_ref[...], wc_ref))):
        gate = gates_ref[:, c * D_MODEL:(c + 1) * D_MODEL].astype(F32)
        term = gate * _dot(y, w_ref[...])
        merged = term if merged is None else merged + term
    out = _dot(merged.astype(BF16), wo_ref[...])
    o_ref[...] = x_ref[...] + _rms(out, g_ref[...])


def _merge(ya, ob, lb, yc, gates, x, wa, wb, wc, wo, g):
    T, D = x.shape
    tok = lambda w: pl.BlockSpec((TM, w), lambda i: (i, 0))
    return pl.pallas_call(
        _merge_kernel,
        grid=(T // TM,),
        in_specs=[tok(FOX_COLS)] + [tok(LANES)] * 6 + [tok(MOBA_COLS), tok(N_GATE_COLS), tok(D),
                  _resident(wa.shape), _resident(wb.shape), _resident(wc.shape), _resident(wo.shape),
                  _resident((1, D))],
        out_specs=tok(D),
        out_shape=jax.ShapeDtypeStruct((T, D), F32),
        compiler_params=_params("parallel"),
        name="merge",
    )(ya, *ob, *lb, yc, gates, x, wa, wb, wc, wo, g)


FF_CHUNK = D_FF // 2


def _ffn_kernel(x_ref, p_ref, g_pre_ref, wgate_ref, wup_ref, wdown_ref, g_post_ref,
                wple_ref, wpg_ref, g_ple_ref, o_ref):
    x = x_ref[...]
    hb = _rms(x, g_pre_ref[...]).astype(BF16)
    down = None
    for c in range(D_FF // FF_CHUNK):
        cols = slice(c * FF_CHUNK, (c + 1) * FF_CHUNK)
        gate = _dot(hb, wgate_ref[:, cols])
        up = _dot(hb, wup_ref[:, cols])
        ff = (gate * jax.nn.sigmoid(gate) * up).astype(BF16)
        part = _dot(ff, wdown_ref[cols, :])
        down = part if down is None else down + part
    x = x + _rms(down, g_post_ref[...])
    ple = _dot(p_ref[...].astype(BF16), wple_ref[...]) * jax.nn.sigmoid(_dot(x.astype(BF16), wpg_ref[...]))
    o_ref[...] = x + _rms(ple, g_ple_ref[...])


def _ffn(x, p, g_pre, wgate, wup, wdown, g_post, wple, wpg, g_ple):
    T, D = x.shape
    tok = lambda w: pl.BlockSpec((TM, w), lambda i: (i, 0))
    vec = _resident((1, D))
    return pl.pallas_call(
        _ffn_kernel,
        grid=(T // TM,),
        in_specs=[tok(D), tok(PLE_DIM), vec, _resident(wgate.shape), _resident(wup.shape),
                  _resident(wdown.shape), vec, _resident(wple.shape), _resident(wpg.shape), vec],
        out_specs=tok(D),
        out_shape=jax.ShapeDtypeStruct((T, D), F32),
        compiler_params=_params("parallel"),
        name="ffn",
    )(x, p, g_pre, wgate, wup, wdown, g_post, wple, wpg, g_ple)


def _rope_tables(seq):
    inv = 1.0 / (ROPE_THETA ** (jnp.arange(0, HEAD_DIM, 2, dtype=F32) / HEAD_DIM))
    ang = jnp.arange(seq, dtype=F32)[:, None] * inv[None, :]
    cos, sin = jnp.cos(ang), jnp.sin(ang)
    reps = LANES // HEAD_DIM
    cos_t = jnp.tile(jnp.concatenate([cos, cos], axis=1), (1, reps))
    sin_t = jnp.tile(jnp.concatenate([-sin, sin], axis=1), (1, reps))
    return cos_t, sin_t


def kernel(x, p, g_mix_pre, w_in, b_f, w_br_a, w_br_b, w_br_c, w_out, g_mix_post, g_ffn_pre,
           w_ffn_gate, w_ffn_up, w_ffn_down, g_ffn_post, w_ple, w_ple_gate, g_ple_post):
    B, S, D = x.shape
    T = B * S
    depth = w_in.shape[0]
    cos_t, sin_t = _rope_tables(S)
    row = lambda g: g.reshape(1, -1)
    f0 = 3 * MIX_WIDTH
    for i in range(depth):
        wq = w_in[i, :, :MIX_WIDTH].astype(BF16)
        wk = w_in[i, :, MIX_WIDTH:2 * MIX_WIDTH].astype(BF16)
        wv = w_in[i, :, 2 * MIX_WIDTH:f0].astype(BF16)
        wvt = wv.T
        wf = jnp.pad(w_in[i, :, f0:f0 + N_HEADS_FOX], ((0, 0), (0, LANES - N_HEADS_FOX))).astype(BF16)
        bf = jnp.pad(b_f[i], (0, LANES - N_HEADS_FOX)).reshape(1, LANES)
        wg = w_in[i, :, f0 + N_HEADS_FOX:].astype(BF16)

        (q, k, v, vt, kx, gates, kmean), residue_views = _proj(
            x, row(g_mix_pre[i]), wq, wk, wv, wvt, wf, bf, wg, cos_t, sin_t)
        n_blocks = S // MOBA_BLOCK
        km = jnp.pad(kmean.reshape(B, n_blocks, MOBA_COLS).astype(BF16),
                     ((0, 0), (0, LANES - n_blocks), (0, 0)))
        y_a = _fox(q, k, kx, vt)
        win_in = {g: residue_views[3 * n:3 * n + 3] for n, (g, _) in enumerate(STRIDED_GROUPS)}
        win = [_window(*win_in.get(g, (q, k, v)), g) for g in range(len(DIL_PATTERNS))]
        y_c = _moba(q, k, vt, km)

        x2 = _merge(y_a.reshape(T, -1), [o for o, _ in win], [l for _, l in win], y_c.reshape(T, -1),
                    gates.reshape(T, -1), x.reshape(T, D),
                    w_br_a[i].astype(BF16), w_br_b[i].astype(BF16), w_br_c[i].astype(BF16),
                    w_out[i].astype(BF16), row(g_mix_post[i]))
        x3 = _ffn(x2, p[i].reshape(T, PLE_DIM), row(g_ffn_pre[i]),
                  w_ffn_gate[i].astype(BF16), w_ffn_up[i].astype(BF16), w_ffn_down[i].astype(BF16),
                  row(g_ffn_post[i]), w_ple[i].astype(BF16), w_ple_gate[i].astype(BF16),
                  row(g_ple_post[i]))
        x = x3.reshape(B, S, D)
    return x
```

```python
import functools
import math

import numpy as np
import jax
import jax.numpy as jnp
from jax import lax
from jax.experimental import pallas as pl
from jax.experimental.pallas import tpu as pltpu

D_MODEL = 1024
HEAD_DIM = 64
N_HEADS_FOX = 4
DIL_PATTERNS = ((128, 1), (512, 4), (2048, 16))
N_HEADS_MOBA = 6
MIX_WIDTH = 1024
MOBA_BLOCK = 256
MOBA_TOPK = 3
PLE_DIM = 256
D_FF = 2816
ROPE_THETA = 10000.0
RMS_EPS = 1e-6
NEG_INF = -1e30
LOG2E = math.log2(math.e)
Q_SCALE = HEAD_DIM ** -0.5 * LOG2E

LANES = 128
SUBLANES = 8
HEADS_PER_LANE_BLOCK = LANES // HEAD_DIM
N_LANE_BLOCKS = MIX_WIDTH // LANES
FOX_COLS = N_HEADS_FOX * HEAD_DIM
DIL_COLS = len(DIL_PATTERNS) * 2 * HEAD_DIM
MOBA_COLS = N_HEADS_MOBA * HEAD_DIM
N_GATE_COLS = 3 * D_MODEL
FORGET_PIECES = 3

TM = 512
TQ = 512
CHUNK = 512
VMEM_LIMIT = 56 * 1024 * 1024

F32 = jnp.float32
BF16 = jnp.bfloat16


def _dot(a, b):
    return jnp.dot(a, b, preferred_element_type=F32)


def _dot_nt(a, b):
    return lax.dot_general(a, b, (((1,), (1,)), ((), ())), preferred_element_type=F32)


def _dot_tn(a, b):
    return lax.dot_general(a, b, (((0,), (0,)), ((), ())), preferred_element_type=F32)


def _rms(x, g):
    var = jnp.mean(x * x, axis=-1, keepdims=True)
    return x * lax.rsqrt(var + RMS_EPS) * g


def _split3(x):
    hi = x.astype(BF16).astype(F32)
    r1 = x - hi
    mid = r1.astype(BF16).astype(F32)
    lo = (r1 - mid).astype(BF16).astype(F32)
    return hi, mid, lo


def _resident(shape):
    nd = len(shape)
    return pl.BlockSpec(shape, lambda *_: (0,) * nd, pipeline_mode=pl.Buffered(1))


def _params(*sem):
    return pltpu.CompilerParams(dimension_semantics=sem, vmem_limit_bytes=VMEM_LIMIT)


STRIDED_GROUPS = tuple((g, dil) for g, (_, dil) in enumerate(DIL_PATTERNS) if dil > 1)


def _proj_kernel(x_ref, g_ref, wq_ref, wk_ref, wv_ref, wvt_ref, wf_ref, bf_ref, wg_ref, cos_ref, sin_ref,
                 q_ref, k_ref, v_ref, vt_ref, kx_ref, gates_ref, kmean_ref, *rest):
    res_refs, (carry_ref, perm_sc) = rest[:-2], rest[-2:]
    i = pl.program_id(1)

    def scatter_residues(t, which, blk):
        for n, (g, dil) in enumerate(STRIDED_GROUPS):
            if blk == FOX_COLS // LANES + g:
                slot = 3 * n + which
                perm_sc[slot] = t
                for r in range(dil):
                    res_refs[slot][0, :, r * LANES:(r + 1) * LANES] = perm_sc[
                        slot, pl.ds(r, TM // dil, stride=dil), :].astype(BF16)

    hb = _rms(x_ref[0], g_ref[...]).astype(BF16)
    cos = cos_ref[...]
    sin = sin_ref[...]
    lane = lax.broadcasted_iota(jnp.int32, (TM, LANES), 1)
    first_half = (lane % HEAD_DIM) < (HEAD_DIM // 2)

    def rope(t):
        rot = jnp.where(first_half, pltpu.roll(t, LANES - HEAD_DIM // 2, 1),
                        pltpu.roll(t, HEAD_DIM // 2, 1))
        return t * cos + rot * sin

    first_rope_blk = FOX_COLS // LANES
    first_moba_blk = (FOX_COLS + DIL_COLS) // LANES

    q = _dot(hb, wq_ref[...])
    for c in range(N_LANE_BLOCKS):
        t = q[:, c * LANES:(c + 1) * LANES]
        if c >= first_rope_blk:
            t = rope(t)
        t = t * Q_SCALE
        q_ref[0, :, c * LANES:(c + 1) * LANES] = t.astype(BF16)
        scatter_residues(t, 0, c)

    k = _dot(hb, wk_ref[...])
    for c in range(N_LANE_BLOCKS):
        t = k[:, c * LANES:(c + 1) * LANES]
        if c >= first_rope_blk:
            t = rope(t)
        k_ref[0, :, c * LANES:(c + 1) * LANES] = t.astype(BF16)
        scatter_residues(t, 1, c)
        if c >= first_moba_blk:
            cm = c - first_moba_blk
            for r in range(TM // MOBA_BLOCK):
                kmean_ref[0, 0, r:r + 1, cm * LANES:(cm + 1) * LANES] = jnp.mean(
                    t[r * MOBA_BLOCK:(r + 1) * MOBA_BLOCK], axis=0, keepdims=True)

    v = _dot(hb, wv_ref[...])
    v_ref[0] = v.astype(BF16)
    for c in range(N_LANE_BLOCKS):
        scatter_residues(v[:, c * LANES:(c + 1) * LANES], 2, c)
    vt_ref[0] = _dot_nt(wvt_ref[...], hb).astype(BF16)

    f = _dot(hb, wf_ref[...]) + bf_ref[...]
    ls = jnp.minimum(f, 0.0) - jnp.log1p(jnp.exp(-jnp.abs(f)))
    ls = jnp.where(lane < N_HEADS_FOX, ls, 0.0)
    row = lax.broadcasted_iota(jnp.int32, (TM, TM), 0)
    col = lax.broadcasted_iota(jnp.int32, (TM, TM), 1)
    tri = jnp.where(col <= row, 1.0, 0.0).astype(BF16)
    hi, mid, lo = _split3(ls)
    cs = _dot(tri, hi.astype(BF16)) + _dot(tri, mid.astype(BF16)) + _dot(tri, lo.astype(BF16))

    @pl.when(i == 0)
    def _():
        carry_ref[...] = jnp.zeros_like(carry_ref)

    cs = cs + carry_ref[...]
    carry_ref[...] = cs[TM - 1:TM, :]
    hi, mid, lo = _split3(-LOG2E * cs)
    pieces = jnp.where(lane < N_HEADS_FOX, hi,
                       jnp.where(lane < 2 * N_HEADS_FOX, pltpu.roll(mid, N_HEADS_FOX, 1),
                                 pltpu.roll(lo, 2 * N_HEADS_FOX, 1)))
    kx_ref[0] = pieces.astype(BF16)

    for c in range(N_GATE_COLS // D_MODEL):
        z = _dot(hb, wg_ref[:, c * D_MODEL:(c + 1) * D_MODEL])
        gates_ref[0, :, c * D_MODEL:(c + 1) * D_MODEL] = jax.nn.sigmoid(z).astype(BF16)


def _proj(x, g, wq, wk, wv, wvt, wf, bf, wg, cos_t, sin_t):
    B, S, D = x.shape
    n_t = S // TM
    tok = lambda w: pl.BlockSpec((1, TM, w), lambda b, i: (b, i, 0))
    res_specs, res_shapes = [], []
    for _, dil in STRIDED_GROUPS:
        assert TM % (dil * 2 * SUBLANES) == 0 and S % dil == 0
        res_specs += [pl.BlockSpec((1, TM // dil, dil * LANES), lambda b, i: (b, i, 0))] * 3
        res_shapes += [jax.ShapeDtypeStruct((B, S // dil, dil * LANES), BF16)] * 3
    outs = pl.pallas_call(
        _proj_kernel,
        grid=(B, n_t),
        in_specs=[tok(D), _resident((1, D)), _resident((D, MIX_WIDTH)), _resident((D, MIX_WIDTH)),
                  _resident((D, MIX_WIDTH)), _resident((MIX_WIDTH, D)),
                  _resident((D, LANES)), _resident((1, LANES)), _resident((D, N_GATE_COLS)),
                  pl.BlockSpec((TM, LANES), lambda b, i: (i, 0)),
                  pl.BlockSpec((TM, LANES), lambda b, i: (i, 0))],
        out_specs=[tok(MIX_WIDTH), tok(MIX_WIDTH), tok(MIX_WIDTH),
                   pl.BlockSpec((1, MIX_WIDTH, TM), lambda b, i: (b, 0, i)),
                   tok(LANES), tok(N_GATE_COLS),
                   pl.BlockSpec((1, 1, TM // MOBA_BLOCK, MOBA_COLS), lambda b, i: (b, i, 0, 0))] + res_specs,
        out_shape=[jax.ShapeDtypeStruct((B, S, MIX_WIDTH), BF16)] * 3
        + [jax.ShapeDtypeStruct((B, MIX_WIDTH, S), BF16),
           jax.ShapeDtypeStruct((B, S, LANES), BF16),
           jax.ShapeDtypeStruct((B, S, N_GATE_COLS), BF16),
           jax.ShapeDtypeStruct((B, n_t, TM // MOBA_BLOCK, MOBA_COLS), F32)] + res_shapes,
        scratch_shapes=[pltpu.VMEM((1, LANES), F32), pltpu.VMEM((len(res_specs), TM, LANES), F32)],
        compiler_params=_params("arbitrary", "arbitrary"),
        name="proj",
    )(x, g, wq, wk, wv, wvt, wf, bf, wg, cos_t, sin_t)
    return outs[:7], outs[7:]


def _reduce_keys(x, op, reduce):
    n = x.shape[0]
    while n > SUBLANES and n % (2 * SUBLANES) == 0:
        n //= 2
        x = op(x[:n], x[n:])
    return reduce(x, axis=0, keepdims=True)


def _key_max(s):
    return _reduce_keys(s, jnp.maximum, jnp.max)


def _probabilities(s, m):
    return jnp.exp2((s - m).astype(BF16))


ACC_ROWS = HEAD_DIM + 2 * SUBLANES


def _values_with_ones(vt):
    return jnp.concatenate([vt, jnp.ones((ACC_ROWS - HEAD_DIM, vt.shape[1]), BF16)], axis=0)


def _chunked_attention(i, n_heads, scores, values, s_sc, p_sc):
    heads = range(n_heads)
    n_full = (i * TQ) // CHUNK

    causal = _chunk_causal(n_full * CHUNK, i * TQ)
    s_diag = scores(n_full)
    s_first = scores(0)
    m, alpha = [], []
    for hl in heads:
        sd = jnp.where(causal, s_diag[hl], NEG_INF)
        m_diag = _key_max(sd)
        p_sc[hl] = _probabilities(sd, m_diag)
        sf = jnp.where(n_full > 0, s_first[hl], NEG_INF)
        s_sc[hl] = sf
        m.append(jnp.maximum(m_diag, _key_max(sf)))
        alpha.append(jnp.exp2(m_diag - m[hl]))

    def product(c, p):
        return [_dot(_values_with_ones(values(c, hl)), p[hl]) for hl in heads]

    def before(c):
        return jnp.where(c <= 0, n_full, c - 1)

    def body(c, carry):
        m, alpha, acc = carry
        pv = product(before(c), [p_sc[hl] for hl in heads])
        s_next = scores(c + 1)
        m_next = tuple(jnp.maximum(m[hl], _key_max(s_next[hl])) for hl in heads)
        alpha_next = tuple(jnp.exp2(m[hl] - m_next[hl]) for hl in heads)
        for hl in heads:
            p_sc[hl] = _probabilities(s_sc[hl], m[hl])
            s_sc[hl] = s_next[hl]
        acc = tuple(alpha[hl] * (acc[hl] + pv[hl]) for hl in heads)
        return m_next, alpha_next, acc

    init = (tuple(m), tuple(alpha), tuple(jnp.zeros((ACC_ROWS, TQ), F32) for _ in heads))
    m, alpha, acc = lax.fori_loop(0, n_full - 1, body, init)

    last = n_full - 1
    pv = product(before(last), [p_sc[hl] for hl in heads])
    tail = product(jnp.maximum(last, 0), [_probabilities(s_sc[hl], m[hl]) for hl in heads])
    return [alpha[hl] * (acc[hl] + pv[hl]) + tail[hl] for hl in heads]


def _query_t(q):
    return q.astype(F32).T


def _head_rows(q_t, hl):
    sub = lax.broadcasted_iota(jnp.int32, q_t.shape, 0)
    return jnp.where((sub >= hl * HEAD_DIM) & (sub < (hl + 1) * HEAD_DIM), q_t, 0.0)


def _finish(accs):
    blocks = []
    for p in range(len(accs) // HEADS_PER_LANE_BLOCK):
        pair = accs[p * HEADS_PER_LANE_BLOCK:(p + 1) * HEADS_PER_LANE_BLOCK]
        out_t = jnp.concatenate([acc[:HEAD_DIM] / acc[HEAD_DIM:HEAD_DIM + 1] for acc in pair], axis=0)
        blocks.append(out_t.T)
    return jnp.concatenate(blocks, axis=1)


def _chunk_scratch(n_heads):
    return [pltpu.VMEM((n_heads, CHUNK, TQ), F32), pltpu.VMEM((n_heads, CHUNK, TQ), BF16)]


def _chunk_causal(first_key, first_query):
    kpos = first_key + lax.broadcasted_iota(jnp.int32, (CHUNK, TQ), 0)
    qpos = first_query + lax.broadcasted_iota(jnp.int32, (CHUNK, TQ), 1)
    return kpos <= qpos


def _fox_kernel(q_ref, k_ref, kx_ref, vt_ref, o_ref, s_sc, p_sc):
    i = pl.program_id(1)
    n_pairs = FOX_COLS // LANES
    sub = lax.broadcasted_iota(jnp.int32, (LANES, TQ), 0)
    q_aug = []
    for p in range(n_pairs):
        q_t = _query_t(q_ref[0, :, p * LANES:(p + 1) * LANES])
        for hl in range(HEADS_PER_LANE_BLOCK):
            head = HEADS_PER_LANE_BLOCK * p + hl
            ones = jnp.where((sub % N_HEADS_FOX == head) & (sub < FORGET_PIECES * N_HEADS_FOX), 1.0, 0.0)
            q_aug.append(jnp.concatenate([_head_rows(q_t, hl), ones], axis=0).astype(BF16))

    def scores(c):
        off = pl.multiple_of(c * CHUNK, CHUNK)
        kx = kx_ref[0, pl.ds(off, CHUNK), :]
        out = []
        for p in range(n_pairs):
            k_aug = jnp.concatenate([k_ref[0, pl.ds(off, CHUNK), p * LANES:(p + 1) * LANES], kx], axis=1)
            out += [_dot(k_aug, qa) for qa in q_aug[p * HEADS_PER_LANE_BLOCK:(p + 1) * HEADS_PER_LANE_BLOCK]]
        return out

    def values(c, h):
        off = pl.multiple_of(c * CHUNK, CHUNK)
        return vt_ref[0, h * HEAD_DIM:(h + 1) * HEAD_DIM, pl.ds(off, CHUNK)]

    o_ref[0] = _finish(_chunked_attention(i, N_HEADS_FOX, scores, values, s_sc, p_sc)).astype(BF16)


def _fox(q, k, kx, vt):
    B, S, _ = q.shape
    assert S % CHUNK == 0
    return pl.pallas_call(
        _fox_kernel,
        grid=(B, S // TQ),
        in_specs=[pl.BlockSpec((1, TQ, FOX_COLS), lambda b, i: (b, i, 0)),
                  pl.BlockSpec((1, S, FOX_COLS), lambda b, i: (b, 0, 0)),
                  pl.BlockSpec((1, S, LANES), lambda b, i: (b, 0, 0)),
                  pl.BlockSpec((1, FOX_COLS, S), lambda b, i: (b, 0, 0))],
        out_specs=pl.BlockSpec((1, TQ, FOX_COLS), lambda b, i: (b, i, 0)),
        out_shape=jax.ShapeDtypeStruct((B, S, FOX_COLS), BF16),
        scratch_shapes=_chunk_scratch(N_HEADS_FOX),
        compiler_params=_params("parallel", "arbitrary"),
        name="fox",
    )(q, k, kx, vt)


WINDOW_KEYS = DIL_PATTERNS[0][0] // DIL_PATTERNS[0][1]
WQ = 256
assert all(w // d == WINDOW_KEYS for w, d in DIL_PATTERNS) and WINDOW_KEYS <= WQ


def _window_kernel(q_ref, kd_ref, kp_ref, vd_ref, vp_ref, o_ref, lse_ref, *, dil):
    a = pl.program_id(1)
    r = pl.program_id(2)
    q_t = _query_t(q_ref[0])
    kd, kp, vd, vp = kd_ref[0], kp_ref[0], vd_ref[0], vp_ref[0]
    kr = lax.broadcasted_iota(jnp.int32, (WQ, WQ), 0)
    qc = lax.broadcasted_iota(jnp.int32, (WQ, WQ), 1)
    valid_d = (kr <= qc) & (qc - kr <= WINDOW_KEYS)
    krp = lax.broadcasted_iota(jnp.int32, (WINDOW_KEYS, WQ), 0)
    qcp = lax.broadcasted_iota(jnp.int32, (WINDOW_KEYS, WQ), 1)
    valid_p = (krp >= qcp) & (a > 0)
    outs, lses = [], []
    q_m = [_head_rows(q_t, hl).astype(BF16) for hl in range(HEADS_PER_LANE_BLOCK)]
    raw = [(_dot(kd, qm), _dot(kp, qm)) for qm in q_m]
    for hl in range(HEADS_PER_LANE_BLOCK):
        s_d = jnp.where(valid_d, raw[hl][0], NEG_INF)
        s_p = jnp.where(valid_p, raw[hl][1], NEG_INF)
        m = jnp.maximum(_reduce_keys(s_d, jnp.maximum, jnp.max), _reduce_keys(s_p, jnp.maximum, jnp.max))
        p_d = jnp.exp2(s_d - m)
        p_p = jnp.exp2(s_p - m)
        l = _reduce_keys(p_d, jnp.add, jnp.sum) + _reduce_keys(p_p, jnp.add, jnp.sum)
        acc = _dot_tn(vd, p_d.astype(BF16)) + _dot_tn(vp, p_p.astype(BF16))
        outs.append(acc[hl * HEAD_DIM:(hl + 1) * HEAD_DIM] / l)
        lses.append(jnp.broadcast_to(m + jnp.log2(l), (HEAD_DIM, WQ)))
    rows = pl.ds(r, WQ, stride=dil) if dil > 1 else pl.ds(0, WQ)
    o_ref[0, rows, :] = jnp.concatenate(outs, axis=0).T
    lse_ref[0, rows, :] = jnp.concatenate(lses, axis=0).T


def _window(q, k, v, group):
    dil = DIL_PATTERNS[group][1]
    B, rows, width = q.shape
    S = rows * dil
    assert rows % WQ == 0 and WQ % WINDOW_KEYS == 0
    col0, col_step = (FOX_COLS // LANES + group, 0) if dil == 1 else (0, 1)
    assert width == (MIX_WIDTH if dil == 1 else dil * LANES)
    cur = pl.BlockSpec((1, WQ, LANES), lambda b, a, r: (b, a, col0 + col_step * r))
    per = WQ // WINDOW_KEYS
    prev = pl.BlockSpec((1, WINDOW_KEYS, LANES),
                        lambda b, a, r: (b, jnp.maximum(a * per - 1, 0), col0 + col_step * r))
    out = pl.BlockSpec((1, dil * WQ, LANES), lambda b, a, r: (b, a, 0))
    o, lse = pl.pallas_call(
        functools.partial(_window_kernel, dil=dil),
        grid=(B, rows // WQ, dil),
        in_specs=[cur, cur, prev, cur, prev],
        out_specs=[out, out],
        out_shape=[jax.ShapeDtypeStruct((B, S, LANES), F32)] * 2,
        compiler_params=_params("parallel", "arbitrary", "arbitrary"),
        name=f"window{group}",
    )(q, k, k, v, v)
    return o.reshape(B * S, LANES), lse.reshape(B * S, LANES)


MOBA_PAIRS = MOBA_COLS // LANES


def _moba_kernel(*refs):
    q_refs, k_refs, vt_refs = refs[:MOBA_PAIRS], refs[MOBA_PAIRS:2 * MOBA_PAIRS], refs[2 * MOBA_PAIRS:3 * MOBA_PAIRS]
    km_ref, o_ref, s_sc, p_sc = refs[3 * MOBA_PAIRS:]
    i = pl.program_id(1)
    n_blocks = km_ref.shape[1]
    blk = lax.broadcasted_iota(jnp.int32, (n_blocks, TQ), 0)
    own = i * (TQ // MOBA_BLOCK) + lax.broadcasted_iota(jnp.int32, (n_blocks, TQ), 1) // MOBA_BLOCK
    past = blk < own
    never = jnp.full((LANES - n_blocks, TQ), NEG_INF, F32)
    q_aug = []
    for p in range(MOBA_PAIRS):
        q_t = _query_t(q_refs[p][0])
        km = km_ref[0, :, p * LANES:(p + 1) * LANES]
        for hl in range(HEADS_PER_LANE_BLOCK):
            qm = _head_rows(q_t, hl)
            gate = jnp.where(past, _dot(km, qm.astype(BF16)), NEG_INF)
            sel = blk == own
            for _ in range(MOBA_TOPK):
                top = jnp.max(gate, axis=0, keepdims=True)
                idx = jnp.min(jnp.where(gate == top, blk, n_blocks), axis=0, keepdims=True)
                hit = blk == idx
                sel = sel | (hit & past)
                gate = jnp.where(hit, -jnp.inf, gate)
            sel_bias = jnp.where(sel, 0.0, NEG_INF)
            q_aug.append(jnp.concatenate([qm, sel_bias, never], axis=0).astype(BF16))

    row = lax.broadcasted_iota(jnp.int32, (CHUNK, LANES), 0)
    lane = lax.broadcasted_iota(jnp.int32, (CHUNK, LANES), 1)
    blocks_per_chunk = CHUNK // MOBA_BLOCK

    def scores(c):
        off = pl.multiple_of(c * CHUNK, CHUNK)
        onehot = jnp.where(lane == c * blocks_per_chunk + row // MOBA_BLOCK, 1.0, 0.0).astype(BF16)
        out = []
        for p in range(MOBA_PAIRS):
            k_aug = jnp.concatenate([k_refs[p][0, pl.ds(off, CHUNK), :], onehot], axis=1)
            out += [_dot(k_aug, qa) for qa in q_aug[p * HEADS_PER_LANE_BLOCK:(p + 1) * HEADS_PER_LANE_BLOCK]]
        return out

    def values(c, h):
        off = pl.multiple_of(c * CHUNK, CHUNK)
        p, hl = divmod(h, HEADS_PER_LANE_BLOCK)
        return vt_refs[p][0, hl * HEAD_DIM:(hl + 1) * HEAD_DIM, pl.ds(off, CHUNK)]

    o_ref[0] = _finish(_chunked_attention(i, N_HEADS_MOBA, scores, values, s_sc, p_sc)).astype(BF16)


def _moba(q, k, vt, km):
    B, S, _ = q.shape
    n_blocks = S // MOBA_BLOCK
    assert TQ % MOBA_BLOCK == 0 and CHUNK % MOBA_BLOCK == 0 and S % CHUNK == 0
    assert n_blocks <= LANES and n_blocks % (2 * SUBLANES) == 0
    first = (FOX_COLS + DIL_COLS) // LANES
    at = lambda f: [pl.BlockSpec(*f(first + p)) for p in range(MOBA_PAIRS)]
    return pl.pallas_call(
        _moba_kernel,
        grid=(B, S // TQ),
        in_specs=at(lambda c: ((1, TQ, LANES), lambda b, i: (b, i, c)))
        + at(lambda c: ((1, S, LANES), lambda b, i: (b, 0, c)))
        + at(lambda c: ((1, LANES, S), lambda b, i: (b, c, 0)))
        + [pl.BlockSpec((1, n_blocks, MOBA_COLS), lambda b, i: (b, 0, 0))],
        out_specs=pl.BlockSpec((1, TQ, MOBA_COLS), lambda b, i: (b, i, 0)),
        out_shape=jax.ShapeDtypeStruct((B, S, MOBA_COLS), BF16),
        scratch_shapes=_chunk_scratch(N_HEADS_MOBA),
        compiler_params=_params("parallel", "arbitrary"),
        name="moba",
    )(*([q] * MOBA_PAIRS + [k] * MOBA_PAIRS + [vt] * MOBA_PAIRS + [km]))


def _merge_kernel(ya_ref, o0_ref, o1_ref, o2_ref, l0_ref, l1_ref, l2_ref, yc_ref, gates_ref, x_ref,
                  wa_ref, wb_ref, wc_ref, wo_ref, g_ref, o_ref):
    lses = [l0_ref[...], l1_ref[...], l2_ref[...]]
    top = jnp.maximum(jnp.maximum(lses[0], lses[1]), lses[2])
    wts = [jnp.exp2(t - top) for t in lses]
    den = wts[0] + wts[1] + wts[2]
    yb = (wts[0] * o0_ref[...] + wts[1] * o1_ref[...] + wts[2] * o2_ref[...]) / den
    merged = None
    for c, (y, w_ref) in enumerate(((ya_ref[...], wa_ref), (yb.astype(BF16), wb_ref), (yc_ref[...], wc_ref))):
        gate = gates_ref[:, c * D_MODEL:(c + 1) * D_MODEL].astype(F32)
        term = gate * _dot(y, w_ref[...])
        merged = term if merged is None else merged + term
    out = _dot(merged.astype(BF16), wo_ref[...])
    o_ref[...] = x_ref[...] + _rms(out, g_ref[...])


def _merge(ya, ob, lb, yc, gates, x, wa, wb, wc, wo, g):
    T, D = x.shape
    tok = lambda w: pl.BlockSpec((TM, w), lambda i: (i, 0))
    return pl.pallas_call(
        _merge_kernel,
        grid=(T // TM,),
        in_specs=[tok(FOX_COLS)] + [tok(LANES)] * 6 + [tok(MOBA_COLS), tok(N_GATE_COLS), tok(D),
                  _resident(wa.shape), _resident(wb.shape), _resident(wc.shape), _resident(wo.shape),
                  _resident((1, D))],
        out_specs=tok(D),
        out_shape=jax.ShapeDtypeStruct((T, D), F32),
        compiler_params=_params("parallel"),
        name="merge",
    )(ya, *ob, *lb, yc, gates, x, wa, wb, wc, wo, g)


FF_CHUNK = D_FF // 2


def _ffn_kernel(x_ref, p_ref, g_pre_ref, wgate_ref, wup_ref, wdown_ref, g_post_ref,
                wple_ref, wpg_ref, g_ple_ref, o_ref):
    x = x_ref[...]
    hb = _rms(x, g_pre_ref[...]).astype(BF16)
    down = None
    for c in range(D_FF // FF_CHUNK):
        cols = slice(c * FF_CHUNK, (c + 1) * FF_CHUNK)
        gate = _dot(hb, wgate_ref[:, cols])
        up = _dot(hb, wup_ref[:, cols])
        ff = (gate * jax.nn.sigmoid(gate) * up).astype(BF16)
        part = _dot(ff, wdown_ref[cols, :])
        down = part if down is None else down + part
    x = x + _rms(down, g_post_ref[...])
    ple = _dot(p_ref[...].astype(BF16), wple_ref[...]) * jax.nn.sigmoid(_dot(x.astype(BF16), wpg_ref[...]))
    o_ref[...] = x + _rms(ple, g_ple_ref[...])


def _ffn(x, p, g_pre, wgate, wup, wdown, g_post, wple, wpg, g_ple):
    T, D = x.shape
    tok = lambda w: pl.BlockSpec((TM, w), lambda i: (i, 0))
    vec = _resident((1, D))
    return pl.pallas_call(
        _ffn_kernel,
        grid=(T // TM,),
        in_specs=[tok(D), tok(PLE_DIM), vec, _resident(wgate.shape), _resident(wup.shape),
                  _resident(wdown.shape), vec, _resident(wple.shape), _resident(wpg.shape), vec],
        out_specs=tok(D),
        out_shape=jax.ShapeDtypeStruct((T, D), F32),
        compiler_params=_params("parallel"),
        name="ffn",
    )(x, p, g_pre, wgate, wup, wdown, g_post, wple, wpg, g_ple)


def _rope_tables(seq):
    inv = 1.0 / (ROPE_THETA ** (jnp.arange(0, HEAD_DIM, 2, dtype=F32) / HEAD_DIM))
    ang = jnp.arange(seq, dtype=F32)[:, None] * inv[None, :]
    cos, sin = jnp.cos(ang), jnp.sin(ang)
    reps = LANES // HEAD_DIM
    cos_t = jnp.tile(jnp.concatenate([cos, cos], axis=1), (1, reps))
    sin_t = jnp.tile(jnp.concatenate([-sin, sin], axis=1), (1, reps))
    return cos_t, sin_t


def kernel(x, p, g_mix_pre, w_in, b_f, w_br_a, w_br_b, w_br_c, w_out, g_mix_post, g_ffn_pre,
           w_ffn_gate, w_ffn_up, w_ffn_down, g_ffn_post, w_ple, w_ple_gate, g_ple_post):
    B, S, D = x.shape
    T = B * S
    depth = w_in.shape[0]
    cos_t, sin_t = _rope_tables(S)
    row = lambda g: g.reshape(1, -1)
    f0 = 3 * MIX_WIDTH
    for i in range(depth):
        wq = w_in[i, :, :MIX_WIDTH].astype(BF16)
        wk = w_in[i, :, MIX_WIDTH:2 * MIX_WIDTH].astype(BF16)
        wv = w_in[i, :, 2 * MIX_WIDTH:f0].astype(BF16)
        wvt = wv.T
        wf = jnp.pad(w_in[i, :, f0:f0 + N_HEADS_FOX], ((0, 0), (0, LANES - N_HEADS_FOX))).astype(BF16)
        bf = jnp.pad(b_f[i], (0, LANES - N_HEADS_FOX)).reshape(1, LANES)
        wg = w_in[i, :, f0 + N_HEADS_FOX:].astype(BF16)

        (q, k, v, vt, kx, gates, kmean), residue_views = _proj(
            x, row(g_mix_pre[i]), wq, wk, wv, wvt, wf, bf, wg, cos_t, sin_t)
        km = kmean.reshape(B, S // MOBA_BLOCK, MOBA_COLS).astype(BF16)
        y_a = _fox(q, k, kx, vt)
        win_in = {g: residue_views[3 * n:3 * n + 3] for n, (g, _) in enumerate(STRIDED_GROUPS)}
        win = [_window(*win_in.get(g, (q, k, v)), g) for g in range(len(DIL_PATTERNS))]
        y_c = _moba(q, k, vt, km)

        x2 = _merge(y_a.reshape(T, -1), [o for o, _ in win], [l for _, l in win], y_c.reshape(T, -1),
                    gates.reshape(T, -1), x.reshape(T, D),
                    w_br_a[i].astype(BF16), w_br_b[i].astype(BF16), w_br_c[i].astype(BF16),
                    w_out[i].astype(BF16), row(g_mix_post[i]))
        x3 = _ffn(x2, p[i].reshape(T, PLE_DIM), row(g_ffn_pre[i]),
                  w_ffn_gate[i].astype(BF16), w_ffn_up[i].astype(BF16), w_ffn_down[i].astype(BF16),
                  row(g_ffn_post[i]), w_ple[i].astype(BF16), w_ple_gate[i].astype(BF16),
                  row(g_ple_post[i]))
        x = x3.reshape(B, S, D)
    return x
```

```python
import functools
import math

import numpy as np
import jax
import jax.numpy as jnp
from jax import lax
from jax.experimental import pallas as pl
from jax.experimental.pallas import tpu as pltpu

D_MODEL = 1024
HEAD_DIM = 64
N_HEADS_FOX = 4
DIL_PATTERNS = ((128, 1), (512, 4), (2048, 16))
N_HEADS_MOBA = 6
MIX_WIDTH = 1024
MOBA_BLOCK = 256
MOBA_TOPK = 3
PLE_DIM = 256
D_FF = 2816
ROPE_THETA = 10000.0
RMS_EPS = 1e-6
NEG_INF = -1e30
LOG2E = math.log2(math.e)
Q_SCALE = HEAD_DIM ** -0.5 * LOG2E

LANES = 128
SUBLANES = 8
HEADS_PER_LANE_BLOCK = LANES // HEAD_DIM
N_LANE_BLOCKS = MIX_WIDTH // LANES
FOX_COLS = N_HEADS_FOX * HEAD_DIM
DIL_COLS = len(DIL_PATTERNS) * 2 * HEAD_DIM
MOBA_COLS = N_HEADS_MOBA * HEAD_DIM
N_GATE_COLS = 3 * D_MODEL
FORGET_PIECES = 3

TM = 512
TQ = 512
CHUNK = 512
VMEM_LIMIT = 56 * 1024 * 1024

F32 = jnp.float32
BF16 = jnp.bfloat16


def _dot(a, b):
    return jnp.dot(a, b, preferred_element_type=F32)


def _dot_nt(a, b):
    return lax.dot_general(a, b, (((1,), (1,)), ((), ())), preferred_element_type=F32)


def _dot_tn(a, b):
    return lax.dot_general(a, b, (((0,), (0,)), ((), ())), preferred_element_type=F32)


def _rms(x, g):
    var = jnp.mean(x * x, axis=-1, keepdims=True)
    return x * lax.rsqrt(var + RMS_EPS) * g


def _split3(x):
    hi = x.astype(BF16).astype(F32)
    r1 = x - hi
    mid = r1.astype(BF16).astype(F32)
    lo = (r1 - mid).astype(BF16).astype(F32)
    return hi, mid, lo


def _resident(shape):
    nd = len(shape)
    return pl.BlockSpec(shape, lambda *_: (0,) * nd, pipeline_mode=pl.Buffered(1))


def _params(*sem):
    return pltpu.CompilerParams(dimension_semantics=sem, vmem_limit_bytes=VMEM_LIMIT)


STRIDED_GROUPS = tuple((g, dil) for g, (_, dil) in enumerate(DIL_PATTERNS) if dil > 1)


def _proj_kernel(x_ref, g_ref, wq_ref, wk_ref, wv_ref, wvt_ref, wf_ref, bf_ref, wg_ref, cos_ref, sin_ref,
                 q_ref, k_ref, v_ref, vt_ref, kx_ref, gates_ref, kmean_ref, *rest):
    res_refs, (carry_ref, perm_sc) = rest[:-2], rest[-2:]
    i = pl.program_id(1)

    def scatter_residues(t, which, blk):
        for n, (g, dil) in enumerate(STRIDED_GROUPS):
            if blk == FOX_COLS // LANES + g:
                slot = 3 * n + which
                perm_sc[slot] = t
                for r in range(dil):
                    res_refs[slot][0, :, r * LANES:(r + 1) * LANES] = perm_sc[
                        slot, pl.ds(r, TM // dil, stride=dil), :].astype(BF16)

    @pl.when(i == 0)
    def _():
        carry_ref[...] = jnp.zeros_like(carry_ref)

    hb = _rms(x_ref[0], g_ref[...]).astype(BF16)
    cos = cos_ref[...]
    sin = sin_ref[...]
    lane = lax.broadcasted_iota(jnp.int32, (TM, LANES), 1)
    first_half = (lane % HEAD_DIM) < (HEAD_DIM // 2)

    def rope(t):
        rot = jnp.where(first_half, pltpu.roll(t, LANES - HEAD_DIM // 2, 1),
                        pltpu.roll(t, HEAD_DIM // 2, 1))
        return t * cos + rot * sin

    first_rope_blk = FOX_COLS // LANES
    first_moba_blk = (FOX_COLS + DIL_COLS) // LANES

    f = _dot(hb, wf_ref[...]) + bf_ref[...]
    ls = jnp.minimum(f, 0.0) - jnp.log1p(jnp.exp(-jnp.abs(f)))
    ls = jnp.where(lane < N_HEADS_FOX, ls, 0.0)
    row = lax.broadcasted_iota(jnp.int32, (TM, TM), 0)
    col = lax.broadcasted_iota(jnp.int32, (TM, TM), 1)
    tri = jnp.where(col <= row, 1.0, 0.0).astype(BF16)
    ls_hi, ls_mid, ls_lo = _split3(ls)

    q = _dot(hb, wq_ref[...])
    cs = _dot(tri, ls_hi.astype(BF16)) + _dot(tri, ls_mid.astype(BF16)) + _dot(tri, ls_lo.astype(BF16))
    for c in range(N_LANE_BLOCKS):
        t = q[:, c * LANES:(c + 1) * LANES]
        if c >= first_rope_blk:
            t = rope(t)
        t = t * Q_SCALE
        q_ref[0, :, c * LANES:(c + 1) * LANES] = t.astype(BF16)
        scatter_residues(t, 0, c)

    k = _dot(hb, wk_ref[...])

    cs = cs + carry_ref[...]
    carry_ref[...] = cs[TM - 1:TM, :]
    hi, mid, lo = _split3(-LOG2E * cs)
    pieces = jnp.where(lane < N_HEADS_FOX, hi,
                       jnp.where(lane < 2 * N_HEADS_FOX, pltpu.roll(mid, N_HEADS_FOX, 1),
                                 pltpu.roll(lo, 2 * N_HEADS_FOX, 1)))
    kx_ref[0] = pieces.astype(BF16)

    for c in range(N_LANE_BLOCKS):
        t = k[:, c * LANES:(c + 1) * LANES]
        if c >= first_rope_blk:
            t = rope(t)
        k_ref[0, :, c * LANES:(c + 1) * LANES] = t.astype(BF16)
        scatter_residues(t, 1, c)
        if c >= first_moba_blk:
            cm = c - first_moba_blk
            for r in range(TM // MOBA_BLOCK):
                kmean_ref[0, 0, r:r + 1, cm * LANES:(cm + 1) * LANES] = jnp.mean(
                    t[r * MOBA_BLOCK:(r + 1) * MOBA_BLOCK], axis=0, keepdims=True)

    v = _dot(hb, wv_ref[...])
    v_ref[0] = v.astype(BF16)
    for c in range(N_LANE_BLOCKS):
        scatter_residues(v[:, c * LANES:(c + 1) * LANES], 2, c)
    vt_ref[0] = _dot_nt(wvt_ref[...], hb).astype(BF16)

    for c in range(N_GATE_COLS // D_MODEL):
        z = _dot(hb, wg_ref[:, c * D_MODEL:(c + 1) * D_MODEL])
        gates_ref[0, :, c * D_MODEL:(c + 1) * D_MODEL] = jax.nn.sigmoid(z).astype(BF16)


def _proj(x, g, wq, wk, wv, wvt, wf, bf, wg, cos_t, sin_t):
    B, S, D = x.shape
    n_t = S // TM
    tok = lambda w: pl.BlockSpec((1, TM, w), lambda b, i: (b, i, 0))
    res_specs, res_shapes = [], []
    for _, dil in STRIDED_GROUPS:
        assert TM % (dil * 2 * SUBLANES) == 0 and S % dil == 0
        res_specs += [pl.BlockSpec((1, TM // dil, dil * LANES), lambda b, i: (b, i, 0))] * 3
        res_shapes += [jax.ShapeDtypeStruct((B, S // dil, dil * LANES), BF16)] * 3
    outs = pl.pallas_call(
        _proj_kernel,
        grid=(B, n_t),
        in_specs=[tok(D), _resident((1, D)), _resident((D, MIX_WIDTH)), _resident((D, MIX_WIDTH)),
                  _resident((D, MIX_WIDTH)), _resident((MIX_WIDTH, D)),
                  _resident((D, LANES)), _resident((1, LANES)), _resident((D, N_GATE_COLS)),
                  pl.BlockSpec((TM, LANES), lambda b, i: (i, 0)),
                  pl.BlockSpec((TM, LANES), lambda b, i: (i, 0))],
        out_specs=[tok(MIX_WIDTH), tok(MIX_WIDTH), tok(MIX_WIDTH),
                   pl.BlockSpec((1, MIX_WIDTH, TM), lambda b, i: (b, 0, i)),
                   tok(LANES), tok(N_GATE_COLS),
                   pl.BlockSpec((1, 1, TM // MOBA_BLOCK, MOBA_COLS), lambda b, i: (b, i, 0, 0))] + res_specs,
        out_shape=[jax.ShapeDtypeStruct((B, S, MIX_WIDTH), BF16)] * 3
        + [jax.ShapeDtypeStruct((B, MIX_WIDTH, S), BF16),
           jax.ShapeDtypeStruct((B, S, LANES), BF16),
           jax.ShapeDtypeStruct((B, S, N_GATE_COLS), BF16),
           jax.ShapeDtypeStruct((B, n_t, TM // MOBA_BLOCK, MOBA_COLS), F32)] + res_shapes,
        scratch_shapes=[pltpu.VMEM((1, LANES), F32), pltpu.VMEM((len(res_specs), TM, LANES), F32)],
        compiler_params=_params("arbitrary", "arbitrary"),
        name="proj",
    )(x, g, wq, wk, wv, wvt, wf, bf, wg, cos_t, sin_t)
    return outs[:7], outs[7:]


def _reduce_keys(x, op, reduce):
    n = x.shape[0]
    while n > SUBLANES and n % (2 * SUBLANES) == 0:
        n //= 2
        x = op(x[:n], x[n:])
    return reduce(x, axis=0, keepdims=True)


def _key_max(s):
    return _reduce_keys(s, jnp.maximum, jnp.max)


def _probabilities(s, m):
    return jnp.exp2((s - m).astype(BF16))


ACC_ROWS = HEAD_DIM + 2 * SUBLANES


def _values_with_ones(vt):
    return jnp.concatenate([vt, jnp.ones((ACC_ROWS - HEAD_DIM, vt.shape[1]), BF16)], axis=0)


def _chunked_attention(i, n_heads, scores, values, s_sc, p_sc):
    heads = range(n_heads)
    n_full = (i * TQ) // CHUNK

    causal = _chunk_causal(n_full * CHUNK, i * TQ)
    s_diag = scores(n_full)
    s_first = scores(0)
    m, alpha = [], []
    for hl in heads:
        sd = jnp.where(causal, s_diag[hl], NEG_INF)
        m_diag = _key_max(sd)
        p_sc[hl] = _probabilities(sd, m_diag)
        sf = jnp.where(n_full > 0, s_first[hl], NEG_INF)
        s_sc[hl] = sf
        m.append(jnp.maximum(m_diag, _key_max(sf)))
        alpha.append(jnp.exp2(m_diag - m[hl]))

    def product(c, p):
        return [_dot(_values_with_ones(values(c, hl)), p[hl]) for hl in heads]

    def before(c):
        return jnp.where(c <= 0, n_full, c - 1)

    def body(c, carry):
        m, alpha, acc = carry
        pv = product(before(c), [p_sc[hl] for hl in heads])
        s_next = scores(c + 1)
        m_next = tuple(jnp.maximum(m[hl], _key_max(s_next[hl])) for hl in heads)
        alpha_next = tuple(jnp.exp2(m[hl] - m_next[hl]) for hl in heads)
        for hl in heads:
            p_sc[hl] = _probabilities(s_sc[hl], m[hl])
            s_sc[hl] = s_next[hl]
        acc = tuple(alpha[hl] * (acc[hl] + pv[hl]) for hl in heads)
        return m_next, alpha_next, acc

    init = (tuple(m), tuple(alpha), tuple(jnp.zeros((ACC_ROWS, TQ), F32) for _ in heads))
    m, alpha, acc = lax.fori_loop(0, n_full - 1, body, init)

    last = n_full - 1
    pv = product(before(last), [p_sc[hl] for hl in heads])
    tail = product(jnp.maximum(last, 0), [_probabilities(s_sc[hl], m[hl]) for hl in heads])
    return [alpha[hl] * (acc[hl] + pv[hl]) + tail[hl] for hl in heads]


def _query_t(q):
    return q.astype(F32).T


def _head_rows(q_t, hl):
    sub = lax.broadcasted_iota(jnp.int32, q_t.shape, 0)
    return jnp.where((sub >= hl * HEAD_DIM) & (sub < (hl + 1) * HEAD_DIM), q_t, 0.0)


def _finish(accs):
    blocks = []
    for p in range(len(accs) // HEADS_PER_LANE_BLOCK):
        pair = accs[p * HEADS_PER_LANE_BLOCK:(p + 1) * HEADS_PER_LANE_BLOCK]
        out_t = jnp.concatenate([acc[:HEAD_DIM] / acc[HEAD_DIM:HEAD_DIM + 1] for acc in pair], axis=0)
        blocks.append(out_t.T)
    return jnp.concatenate(blocks, axis=1)


def _chunk_scratch(n_heads):
    return [pltpu.VMEM((n_heads, CHUNK, TQ), F32), pltpu.VMEM((n_heads, CHUNK, TQ), BF16)]


def _chunk_causal(first_key, first_query):
    kpos = first_key + lax.broadcasted_iota(jnp.int32, (CHUNK, TQ), 0)
    qpos = first_query + lax.broadcasted_iota(jnp.int32, (CHUNK, TQ), 1)
    return kpos <= qpos


def _fox_kernel(q_ref, k_ref, kx_ref, vt_ref, o_ref, s_sc, p_sc):
    i = pl.program_id(1)
    n_pairs = FOX_COLS // LANES
    sub = lax.broadcasted_iota(jnp.int32, (LANES, TQ), 0)
    q_aug = []
    for p in range(n_pairs):
        q_t = _query_t(q_ref[0, :, p * LANES:(p + 1) * LANES])
        for hl in range(HEADS_PER_LANE_BLOCK):
            head = HEADS_PER_LANE_BLOCK * p + hl
            ones = jnp.where((sub % N_HEADS_FOX == head) & (sub < FORGET_PIECES * N_HEADS_FOX), 1.0, 0.0)
            q_aug.append(jnp.concatenate([_head_rows(q_t, hl), ones], axis=0).astype(BF16))

    def scores(c):
        off = pl.multiple_of(c * CHUNK, CHUNK)
        kx = kx_ref[0, pl.ds(off, CHUNK), :]
        out = []
        for p in range(n_pairs):
            k_aug = jnp.concatenate([k_ref[0, pl.ds(off, CHUNK), p * LANES:(p + 1) * LANES], kx], axis=1)
            out += [_dot(k_aug, qa) for qa in q_aug[p * HEADS_PER_LANE_BLOCK:(p + 1) * HEADS_PER_LANE_BLOCK]]
        return out

    def values(c, h):
        off = pl.multiple_of(c * CHUNK, CHUNK)
        return vt_ref[0, h * HEAD_DIM:(h + 1) * HEAD_DIM, pl.ds(off, CHUNK)]

    o_ref[0] = _finish(_chunked_attention(i, N_HEADS_FOX, scores, values, s_sc, p_sc)).astype(BF16)


def _fox(q, k, kx, vt):
    B, S, _ = q.shape
    assert S % CHUNK == 0
    return pl.pallas_call(
        _fox_kernel,
        grid=(B, S // TQ),
        in_specs=[pl.BlockSpec((1, TQ, FOX_COLS), lambda b, i: (b, i, 0)),
                  pl.BlockSpec((1, S, FOX_COLS), lambda b, i: (b, 0, 0)),
                  pl.BlockSpec((1, S, LANES), lambda b, i: (b, 0, 0)),
                  pl.BlockSpec((1, FOX_COLS, S), lambda b, i: (b, 0, 0))],
        out_specs=pl.BlockSpec((1, TQ, FOX_COLS), lambda b, i: (b, i, 0)),
        out_shape=jax.ShapeDtypeStruct((B, S, FOX_COLS), BF16),
        scratch_shapes=_chunk_scratch(N_HEADS_FOX),
        compiler_params=_params("parallel", "arbitrary"),
        name="fox",
    )(q, k, kx, vt)


WINDOW_KEYS = DIL_PATTERNS[0][0] // DIL_PATTERNS[0][1]
WQ = 256
assert all(w // d == WINDOW_KEYS for w, d in DIL_PATTERNS) and WINDOW_KEYS <= WQ


WINDOW_CHAINS = 4


def _window_kernel(q_ref, kc_ref, kp_ref, vc_ref, vp_ref, o_ref, lse_ref, *, dil, n_res, n_sub):
    a = pl.program_id(1)
    rg = pl.program_id(2)
    kr = lax.broadcasted_iota(jnp.int32, (WQ, WQ), 0)
    qc = lax.broadcasted_iota(jnp.int32, (WQ, WQ), 1)
    valid_d = (kr <= qc) & (qc - kr <= WINDOW_KEYS)
    krp = lax.broadcasted_iota(jnp.int32, (WINDOW_KEYS, WQ), 0)
    qcp = lax.broadcasted_iota(jnp.int32, (WINDOW_KEYS, WQ), 1)
    valid_p = krp >= qcp
    valid_first = valid_p & (a > 0)

    chains = [(rr, t) for rr in range(n_res) for t in range(n_sub)]
    heads = range(HEADS_PER_LANE_BLOCK)
    raw, vals = [], []
    for rr, t in chains:
        cols = slice(rr * LANES, (rr + 1) * LANES)
        sub = slice(t * WQ, (t + 1) * WQ)
        q_t = _query_t(q_ref[0, sub, cols])
        if t == 0:
            kp, vp = kp_ref[0, :, cols], vp_ref[0, :, cols]
        else:
            before = slice(t * WQ - WINDOW_KEYS, t * WQ)
            kp, vp = kc_ref[0, before, cols], vc_ref[0, before, cols]
        kd = kc_ref[0, sub, cols]
        vals.append((vc_ref[0, sub, cols], vp))
        q_m = [_head_rows(q_t, hl).astype(BF16) for hl in heads]
        raw.append([(_dot(kd, qm), _dot(kp, qm)) for qm in q_m])

    probs, stats = [], []
    for n, (rr, t) in enumerate(chains):
        for hl in heads:
            s_d = jnp.where(valid_d, raw[n][hl][0], NEG_INF)
            s_p = jnp.where(valid_first if t == 0 else valid_p, raw[n][hl][1], NEG_INF)
            m = jnp.maximum(_key_max(s_d), _key_max(s_p))
            p_d = jnp.exp2(s_d - m)
            p_p = jnp.exp2(s_p - m)
            l = _reduce_keys(p_d, jnp.add, jnp.sum) + _reduce_keys(p_p, jnp.add, jnp.sum)
            probs.append((p_d.astype(BF16), p_p.astype(BF16)))
            stats.append((m, l))

    accs = [_dot_tn(vals[n][0], probs[n * len(heads) + hl][0]) + _dot_tn(vals[n][1], probs[n * len(heads) + hl][1])
            for n in range(len(chains)) for hl in heads]

    for n, (rr, t) in enumerate(chains):
        outs, lses = [], []
        for hl in heads:
            m, l = stats[n * len(heads) + hl]
            outs.append(accs[n * len(heads) + hl][hl * HEAD_DIM:(hl + 1) * HEAD_DIM] / l)
            lses.append(jnp.broadcast_to(m + jnp.log2(l), (HEAD_DIM, WQ)))
        r = rg * n_res + rr
        rows = pl.ds(t * WQ * dil + r, WQ, stride=dil) if dil > 1 else pl.ds(t * WQ, WQ)
        o_ref[0, rows, :] = jnp.concatenate(outs, axis=0).T
        lse_ref[0, rows, :] = jnp.concatenate(lses, axis=0).T


def _window(q, k, v, group):
    dil = DIL_PATTERNS[group][1]
    B, rows, width = q.shape
    S = rows * dil
    n_res = min(dil, 2)
    n_sub = WINDOW_CHAINS // n_res
    tile = n_sub * WQ
    assert rows % tile == 0 and tile % WINDOW_KEYS == 0 and dil % n_res == 0
    col0, col_step = (FOX_COLS // LANES + group, 0) if dil == 1 else (0, 1)
    assert width == (MIX_WIDTH if dil == 1 else dil * LANES)
    cur = pl.BlockSpec((1, tile, n_res * LANES), lambda b, a, rg: (b, a, col0 + col_step * rg))
    per = tile // WINDOW_KEYS
    prev = pl.BlockSpec((1, WINDOW_KEYS, n_res * LANES),
                        lambda b, a, rg: (b, jnp.maximum(a * per - 1, 0), col0 + col_step * rg))
    out = pl.BlockSpec((1, dil * tile, LANES), lambda b, a, rg: (b, a, 0))
    o, lse = pl.pallas_call(
        functools.partial(_window_kernel, dil=dil, n_res=n_res, n_sub=n_sub),
        grid=(B, rows // tile, dil // n_res),
        in_specs=[cur, cur, prev, cur, prev],
        out_specs=[out, out],
        out_shape=[jax.ShapeDtypeStruct((B, S, LANES), F32)] * 2,
        compiler_params=_params("parallel", "arbitrary", "arbitrary"),
        name=f"window{group}",
    )(q, k, k, v, v)
    return o.reshape(B * S, LANES), lse.reshape(B * S, LANES)


MOBA_PAIRS = MOBA_COLS // LANES


def _moba_kernel(*refs):
    q_refs, k_refs, vt_refs = refs[:MOBA_PAIRS], refs[MOBA_PAIRS:2 * MOBA_PAIRS], refs[2 * MOBA_PAIRS:3 * MOBA_PAIRS]
    km_ref, o_ref, s_sc, p_sc = refs[3 * MOBA_PAIRS:]
    i = pl.program_id(1)
    n_blocks = km_ref.shape[1]
    blk = lax.broadcasted_iota(jnp.int32, (n_blocks, TQ), 0)
    own = i * (TQ // MOBA_BLOCK) + lax.broadcasted_iota(jnp.int32, (n_blocks, TQ), 1) // MOBA_BLOCK
    past = blk < own
    never = jnp.full((LANES - n_blocks, TQ), NEG_INF, F32)
    q_aug = []
    for p in range(MOBA_PAIRS):
        q_t = _query_t(q_refs[p][0])
        km = km_ref[0, :, p * LANES:(p + 1) * LANES]
        for hl in range(HEADS_PER_LANE_BLOCK):
            qm = _head_rows(q_t, hl)
            gate = jnp.where(past, _dot(km, qm.astype(BF16)), NEG_INF)
            sel = blk == own
            for _ in range(MOBA_TOPK):
                top = jnp.max(gate, axis=0, keepdims=True)
                idx = jnp.min(jnp.where(gate == top, blk, n_blocks), axis=0, keepdims=True)
                hit = blk == idx
                sel = sel | (hit & past)
                gate = jnp.where(hit, -jnp.inf, gate)
            sel_bias = jnp.where(sel, 0.0, NEG_INF)
            q_aug.append(jnp.concatenate([qm, sel_bias, never], axis=0).astype(BF16))

    row = lax.broadcasted_iota(jnp.int32, (CHUNK, LANES), 0)
    lane = lax.broadcasted_iota(jnp.int32, (CHUNK, LANES), 1)
    blocks_per_chunk = CHUNK // MOBA_BLOCK

    def scores(c):
        off = pl.multiple_of(c * CHUNK, CHUNK)
        onehot = jnp.where(lane == c * blocks_per_chunk + row // MOBA_BLOCK, 1.0, 0.0).astype(BF16)
        out = []
        for p in range(MOBA_PAIRS):
            k_aug = jnp.concatenate([k_refs[p][0, pl.ds(off, CHUNK), :], onehot], axis=1)
            out += [_dot(k_aug, qa) for qa in q_aug[p * HEADS_PER_LANE_BLOCK:(p + 1) * HEADS_PER_LANE_BLOCK]]
        return out

    def values(c, h):
        off = pl.multiple_of(c * CHUNK, CHUNK)
        p, hl = divmod(h, HEADS_PER_LANE_BLOCK)
        return vt_refs[p][0, hl * HEAD_DIM:(hl + 1) * HEAD_DIM, pl.ds(off, CHUNK)]

    o_ref[0] = _finish(_chunked_attention(i, N_HEADS_MOBA, scores, values, s_sc, p_sc)).astype(BF16)


def _moba(q, k, vt, km):
    B, S, _ = q.shape
    n_blocks = S // MOBA_BLOCK
    assert TQ % MOBA_BLOCK == 0 and CHUNK % MOBA_BLOCK == 0 and S % CHUNK == 0
    assert n_blocks <= LANES and n_blocks % (2 * SUBLANES) == 0
    first = (FOX_COLS + DIL_COLS) // LANES
    at = lambda f: [pl.BlockSpec(*f(first + p)) for p in range(MOBA_PAIRS)]
    return pl.pallas_call(
        _moba_kernel,
        grid=(B, S // TQ),
        in_specs=at(lambda c: ((1, TQ, LANES), lambda b, i: (b, i, c)))
        + at(lambda c: ((1, S, LANES), lambda b, i: (b, 0, c)))
        + at(lambda c: ((1, LANES, S), lambda b, i: (b, c, 0)))
        + [pl.BlockSpec((1, n_blocks, MOBA_COLS), lambda b, i: (b, 0, 0))],
        out_specs=pl.BlockSpec((1, TQ, MOBA_COLS), lambda b, i: (b, i, 0)),
        out_shape=jax.ShapeDtypeStruct((B, S, MOBA_COLS), BF16),
        scratch_shapes=_chunk_scratch(N_HEADS_MOBA),
        compiler_params=_params("parallel", "arbitrary"),
        name="moba",
    )(*([q] * MOBA_PAIRS + [k] * MOBA_PAIRS + [vt] * MOBA_PAIRS + [km]))


def _merge_kernel(ya_ref, o0_ref, o1_ref, o2_ref, l0_ref, l1_ref, l2_ref, yc_ref, gates_ref, x_ref,
                  wa_ref, wb_ref, wc_ref, wo_ref, g_ref, o_ref):
    lses = [l0_ref[...], l1_ref[...], l2_ref[...]]
    top = jnp.maximum(jnp.maximum(lses[0], lses[1]), lses[2])
    wts = [jnp.exp2(t - top) for t in lses]
    den = wts[0] + wts[1] + wts[2]
    yb = (wts[0] * o0_ref[...] + wts[1] * o1_ref[...] + wts[2] * o2_ref[...]) / den
    merged = None
    for c, (y, w_ref) in enumerate(((ya_ref[...], wa_ref), (yb.astype(BF16), wb_ref), (yc_ref[...], wc_ref))):
        gate = gates_ref[:, c * D_MODEL:(c + 1) * D_MODEL].astype(F32)
        term = gate * _dot(y, w_ref[...])
        merged = term if merged is None else merged + term
    out = _dot(merged.astype(BF16), wo_ref[...])
    o_ref[...] = x_ref[...] + _rms(out, g_ref[...])


def _merge(ya, ob, lb, yc, gates, x, wa, wb, wc, wo, g):
    T, D = x.shape
    tok = lambda w: pl.BlockSpec((TM, w), lambda i: (i, 0))
    return pl.pallas_call(
        _merge_kernel,
        grid=(T // TM,),
        in_specs=[tok(FOX_COLS)] + [tok(LANES)] * 6 + [tok(MOBA_COLS), tok(N_GATE_COLS), tok(D),
                  _resident(wa.shape), _resident(wb.shape), _resident(wc.shape), _resident(wo.shape),
                  _resident((1, D))],
        out_specs=tok(D),
        out_shape=jax.ShapeDtypeStruct((T, D), F32),
        compiler_params=_params("parallel"),
        name="merge",
    )(ya, *ob, *lb, yc, gates, x, wa, wb, wc, wo, g)


MXU_TILE = 256
assert D_FF % MXU_TILE == 0
FF_SPLITS = (0, (D_FF // MXU_TILE + 1) // 2 * MXU_TILE, D_FF)


def _ffn_kernel(x_ref, p_ref, g_pre_ref, wgate_ref, wup_ref, wdown_ref, g_post_ref,
                wple_ref, wpg_ref, g_ple_ref, o_ref):
    x = x_ref[...]
    hb = _rms(x, g_pre_ref[...]).astype(BF16)
    down = None
    for lo, hi in zip(FF_SPLITS[:-1], FF_SPLITS[1:]):
        cols = slice(lo, hi)
        gate = _dot(hb, wgate_ref[:, cols])
        up = _dot(hb, wup_ref[:, cols])
        ff = (gate * jax.nn.sigmoid(gate) * up).astype(BF16)
        part = _dot(ff, wdown_ref[cols, :])
        down = part if down is None else down + part
    x = x + _rms(down, g_post_ref[...])
    ple = _dot(p_ref[...].astype(BF16), wple_ref[...]) * jax.nn.sigmoid(_dot(x.astype(BF16), wpg_ref[...]))
    o_ref[...] = x + _rms(ple, g_ple_ref[...])


def _ffn(x, p, g_pre, wgate, wup, wdown, g_post, wple, wpg, g_ple):
    T, D = x.shape
    tok = lambda w: pl.BlockSpec((TM, w), lambda i: (i, 0))
    vec = _resident((1, D))
    return pl.pallas_call(
        _ffn_kernel,
        grid=(T // TM,),
        in_specs=[tok(D), tok(PLE_DIM), vec, _resident(wgate.shape), _resident(wup.shape),
                  _resident(wdown.shape), vec, _resident(wple.shape), _resident(wpg.shape), vec],
        out_specs=tok(D),
        out_shape=jax.ShapeDtypeStruct((T, D), F32),
        compiler_params=_params("parallel"),
        name="ffn",
    )(x, p, g_pre, wgate, wup, wdown, g_post, wple, wpg, g_ple)


def _rope_tables(seq):
    inv = 1.0 / (ROPE_THETA ** (jnp.arange(0, HEAD_DIM, 2, dtype=F32) / HEAD_DIM))
    ang = jnp.arange(seq, dtype=F32)[:, None] * inv[None, :]
    cos, sin = jnp.cos(ang), jnp.sin(ang)
    reps = LANES // HEAD_DIM
    cos_t = jnp.tile(jnp.concatenate([cos, cos], axis=1), (1, reps))
    sin_t = jnp.tile(jnp.concatenate([-sin, sin], axis=1), (1, reps))
    return cos_t, sin_t


def kernel(x, p, g_mix_pre, w_in, b_f, w_br_a, w_br_b, w_br_c, w_out, g_mix_post, g_ffn_pre,
           w_ffn_gate, w_ffn_up, w_ffn_down, g_ffn_post, w_ple, w_ple_gate, g_ple_post):
    B, S, D = x.shape
    T = B * S
    depth = w_in.shape[0]
    cos_t, sin_t = _rope_tables(S)
    row = lambda g: g.reshape(1, -1)
    f0 = 3 * MIX_WIDTH
    for i in range(depth):
        wq = w_in[i, :, :MIX_WIDTH].astype(BF16)
        wk = w_in[i, :, MIX_WIDTH:2 * MIX_WIDTH].astype(BF16)
        wv = w_in[i, :, 2 * MIX_WIDTH:f0].astype(BF16)
        wvt = wv.T
        wf = jnp.pad(w_in[i, :, f0:f0 + N_HEADS_FOX], ((0, 0), (0, LANES - N_HEADS_FOX))).astype(BF16)
        bf = jnp.pad(b_f[i], (0, LANES - N_HEADS_FOX)).reshape(1, LANES)
        wg = w_in[i, :, f0 + N_HEADS_FOX:].astype(BF16)

        (q, k, v, vt, kx, gates, kmean), residue_views = _proj(
            x, row(g_mix_pre[i]), wq, wk, wv, wvt, wf, bf, wg, cos_t, sin_t)
        km = kmean.reshape(B, S // MOBA_BLOCK, MOBA_COLS).astype(BF16)
        y_a = _fox(q, k, kx, vt)
        win_in = {g: residue_views[3 * n:3 * n + 3] for n, (g, _) in enumerate(STRIDED_GROUPS)}
        win = [_window(*win_in.get(g, (q, k, v)), g) for g in range(len(DIL_PATTERNS))]
        y_c = _moba(q, k, vt, km)

        x2 = _merge(y_a.reshape(T, -1), [o for o, _ in win], [l for _, l in win], y_c.reshape(T, -1),
                    gates.reshape(T, -1), x.reshape(T, D),
                    w_br_a[i].astype(BF16), w_br_b[i].astype(BF16), w_br_c[i].astype(BF16),
                    w_out[i].astype(BF16), row(g_mix_post[i]))
        x3 = _ffn(x2, p[i].reshape(T, PLE_DIM), row(g_ffn_pre[i]),
                  w_ffn_gate[i].astype(BF16), w_ffn_up[i].astype(BF16), w_ffn_down[i].astype(BF16),
                  row(g_ffn_post[i]), w_ple[i].astype(BF16), w_ple_gate[i].astype(BF16),
                  row(g_ple_post[i]))
        x = x3.reshape(B, S, D)
    return x
```

```python
import functools
import math

import numpy as np
import jax
import jax.numpy as jnp
from jax import lax
from jax.experimental import pallas as pl
from jax.experimental.pallas import tpu as pltpu

D_MODEL = 1024
HEAD_DIM = 64
N_HEADS_FOX = 4
DIL_PATTERNS = ((128, 1), (512, 4), (2048, 16))
N_HEADS_MOBA = 6
MIX_WIDTH = 1024
MOBA_BLOCK = 256
MOBA_TOPK = 3
PLE_DIM = 256
D_FF = 2816
ROPE_THETA = 10000.0
RMS_EPS = 1e-6
NEG_INF = -1e30
LOG2E = math.log2(math.e)
Q_SCALE = HEAD_DIM ** -0.5 * LOG2E

LANES = 128
SUBLANES = 8
HEADS_PER_LANE_BLOCK = LANES // HEAD_DIM
N_LANE_BLOCKS = MIX_WIDTH // LANES
FOX_COLS = N_HEADS_FOX * HEAD_DIM
DIL_COLS = len(DIL_PATTERNS) * 2 * HEAD_DIM
MOBA_COLS = N_HEADS_MOBA * HEAD_DIM
N_GATE_COLS = 3 * D_MODEL
FORGET_PIECES = 3

TM = 512
TQ = 512
CHUNK = 512
VMEM_LIMIT = 56 * 1024 * 1024

F32 = jnp.float32
BF16 = jnp.bfloat16


def _dot(a, b):
    return jnp.dot(a, b, preferred_element_type=F32)


def _dot_nt(a, b):
    return lax.dot_general(a, b, (((1,), (1,)), ((), ())), preferred_element_type=F32)


def _dot_tn(a, b):
    return lax.dot_general(a, b, (((0,), (0,)), ((), ())), preferred_element_type=F32)


def _rms(x, g):
    var = jnp.mean(x * x, axis=-1, keepdims=True)
    return x * lax.rsqrt(var + RMS_EPS) * g


def _split3(x):
    hi = x.astype(BF16).astype(F32)
    r1 = x - hi
    mid = r1.astype(BF16).astype(F32)
    lo = (r1 - mid).astype(BF16).astype(F32)
    return hi, mid, lo


def _resident(shape):
    nd = len(shape)
    return pl.BlockSpec(shape, lambda *_: (0,) * nd, pipeline_mode=pl.Buffered(1))


def _params(*sem):
    return pltpu.CompilerParams(dimension_semantics=sem, vmem_limit_bytes=VMEM_LIMIT)


STRIDED_GROUPS = tuple((g, dil) for g, (_, dil) in enumerate(DIL_PATTERNS) if dil > 1)


def _proj_kernel(x_ref, g_ref, wq_ref, wk_ref, wv_ref, wf_ref, bf_ref, wg_ref, cos_ref, sin_ref,
                 q_ref, k_ref, v_ref, vt_ref, kx_ref, gates_ref, kmean_ref, *rest):
    res_refs, (carry_ref, perm_sc) = rest[:-2], rest[-2:]
    i = pl.program_id(1)

    def scatter_residues(t, which, blk):
        for n, (g, dil) in enumerate(STRIDED_GROUPS):
            if blk == FOX_COLS // LANES + g:
                slot = 3 * n + which
                perm_sc[slot] = t
                for r in range(dil):
                    res_refs[slot][0, :, r * LANES:(r + 1) * LANES] = perm_sc[
                        slot, pl.ds(r, TM // dil, stride=dil), :].astype(BF16)

    @pl.when(i == 0)
    def _():
        carry_ref[...] = jnp.zeros_like(carry_ref)

    hb = _rms(x_ref[0], g_ref[...]).astype(BF16)
    cos = cos_ref[...]
    sin = sin_ref[...]
    lane = lax.broadcasted_iota(jnp.int32, (TM, LANES), 1)
    first_half = (lane % HEAD_DIM) < (HEAD_DIM // 2)

    def rope(t):
        rot = jnp.where(first_half, pltpu.roll(t, LANES - HEAD_DIM // 2, 1),
                        pltpu.roll(t, HEAD_DIM // 2, 1))
        return t * cos + rot * sin

    first_rope_blk = FOX_COLS // LANES
    first_moba_blk = (FOX_COLS + DIL_COLS) // LANES

    f = _dot(hb, wf_ref[...]) + bf_ref[...]
    ls = jnp.minimum(f, 0.0) - jnp.log1p(jnp.exp(-jnp.abs(f)))
    ls = jnp.where(lane < N_HEADS_FOX, ls, 0.0)
    row = lax.broadcasted_iota(jnp.int32, (TM, TM), 0)
    col = lax.broadcasted_iota(jnp.int32, (TM, TM), 1)
    tri = jnp.where(col <= row, 1.0, 0.0).astype(BF16)
    ls_hi, ls_mid, ls_lo = _split3(ls)

    q = _dot(hb, wq_ref[...])
    cs = _dot(tri, ls_hi.astype(BF16)) + _dot(tri, ls_mid.astype(BF16)) + _dot(tri, ls_lo.astype(BF16))
    for c in range(N_LANE_BLOCKS):
        t = q[:, c * LANES:(c + 1) * LANES]
        if c >= first_rope_blk:
            t = rope(t)
        t = t * Q_SCALE
        q_ref[0, :, c * LANES:(c + 1) * LANES] = t.astype(BF16)
        scatter_residues(t, 0, c)

    k = _dot(hb, wk_ref[...])

    cs = cs + carry_ref[...]
    carry_ref[...] = cs[TM - 1:TM, :]
    hi, mid, lo = _split3(-LOG2E * cs)
    pieces = jnp.where(lane < N_HEADS_FOX, hi,
                       jnp.where(lane < 2 * N_HEADS_FOX, pltpu.roll(mid, N_HEADS_FOX, 1),
                                 pltpu.roll(lo, 2 * N_HEADS_FOX, 1)))
    kx_ref[0] = pieces.astype(BF16)

    for c in range(N_LANE_BLOCKS):
        t = k[:, c * LANES:(c + 1) * LANES]
        if c >= first_rope_blk:
            t = rope(t)
        k_ref[0, :, c * LANES:(c + 1) * LANES] = t.astype(BF16)
        scatter_residues(t, 1, c)
        if c >= first_moba_blk:
            cm = c - first_moba_blk
            for r in range(TM // MOBA_BLOCK):
                kmean_ref[0, 0, r:r + 1, cm * LANES:(cm + 1) * LANES] = jnp.mean(
                    t[r * MOBA_BLOCK:(r + 1) * MOBA_BLOCK], axis=0, keepdims=True)

    v = _dot(hb, wv_ref[...])
    v_ref[0] = v.astype(BF16)
    for c in range(N_LANE_BLOCKS):
        scatter_residues(v[:, c * LANES:(c + 1) * LANES], 2, c)
    vt_ref[0] = v.T.astype(BF16)

    for c in range(N_GATE_COLS // D_MODEL):
        z = _dot(hb, wg_ref[:, c * D_MODEL:(c + 1) * D_MODEL])
        gates_ref[0, :, c * D_MODEL:(c + 1) * D_MODEL] = jax.nn.sigmoid(z).astype(BF16)


def _proj(x, g, wq, wk, wv, wf, bf, wg, cos_t, sin_t):
    B, S, D = x.shape
    n_t = S // TM
    tok = lambda w: pl.BlockSpec((1, TM, w), lambda b, i: (b, i, 0))
    res_specs, res_shapes = [], []
    for _, dil in STRIDED_GROUPS:
        assert TM % (dil * 2 * SUBLANES) == 0 and S % dil == 0
        res_specs += [pl.BlockSpec((1, TM // dil, dil * LANES), lambda b, i: (b, i, 0))] * 3
        res_shapes += [jax.ShapeDtypeStruct((B, S // dil, dil * LANES), BF16)] * 3
    outs = pl.pallas_call(
        _proj_kernel,
        grid=(B, n_t),
        in_specs=[tok(D), _resident((1, D)), _resident((D, MIX_WIDTH)), _resident((D, MIX_WIDTH)),
                  _resident((D, MIX_WIDTH)),
                  _resident((D, LANES)), _resident((1, LANES)), _resident((D, N_GATE_COLS)),
                  pl.BlockSpec((TM, LANES), lambda b, i: (i, 0)),
                  pl.BlockSpec((TM, LANES), lambda b, i: (i, 0))],
        out_specs=[tok(MIX_WIDTH), tok(MIX_WIDTH), tok(MIX_WIDTH),
                   pl.BlockSpec((1, MIX_WIDTH, TM), lambda b, i: (b, 0, i)),
                   tok(LANES), tok(N_GATE_COLS),
                   pl.BlockSpec((1, 1, TM // MOBA_BLOCK, MOBA_COLS), lambda b, i: (b, i, 0, 0))] + res_specs,
        out_shape=[jax.ShapeDtypeStruct((B, S, MIX_WIDTH), BF16)] * 3
        + [jax.ShapeDtypeStruct((B, MIX_WIDTH, S), BF16),
           jax.ShapeDtypeStruct((B, S, LANES), BF16),
           jax.ShapeDtypeStruct((B, S, N_GATE_COLS), BF16),
           jax.ShapeDtypeStruct((B, n_t, TM // MOBA_BLOCK, MOBA_COLS), F32)] + res_shapes,
        scratch_shapes=[pltpu.VMEM((1, LANES), F32), pltpu.VMEM((len(res_specs), TM, LANES), F32)],
        compiler_params=_params("arbitrary", "arbitrary"),
        name="proj",
    )(x, g, wq, wk, wv, wf, bf, wg, cos_t, sin_t)
    return outs[:7], outs[7:]


def _reduce_keys(x, op, reduce):
    n = x.shape[0]
    while n > SUBLANES and n % (2 * SUBLANES) == 0:
        n //= 2
        x = op(x[:n], x[n:])
    return reduce(x, axis=0, keepdims=True)


def _key_max(s):
    return _reduce_keys(s, jnp.maximum, jnp.max)


def _probabilities(s, m):
    return jnp.exp2((s - m).astype(BF16))


ACC_ROWS = HEAD_DIM + 2 * SUBLANES


def _values_with_ones(vt):
    return jnp.concatenate([vt, jnp.ones((ACC_ROWS - HEAD_DIM, vt.shape[1]), BF16)], axis=0)


def _chunked_attention(i, n_heads, scores, values, s_sc, p_sc):
    heads = range(n_heads)
    n_full = (i * TQ) // CHUNK

    causal = _chunk_causal(n_full * CHUNK, i * TQ)
    s_diag = [score() for score in scores(n_full)]
    s_first = [score() for score in scores(0)]
    m, alpha = [], []
    for hl in heads:
        sd = jnp.where(causal, s_diag[hl], NEG_INF)
        m_diag = _key_max(sd)
        p_sc[hl] = _probabilities(sd, m_diag)
        sf = jnp.where(n_full > 0, s_first[hl], NEG_INF)
        s_sc[hl] = sf
        m.append(jnp.maximum(m_diag, _key_max(sf)))
        alpha.append(jnp.exp2(m_diag - m[hl]))

    def product(c, p):
        return [_dot(_values_with_ones(values(c, hl)), p[hl]) for hl in heads]

    def before(c):
        return jnp.where(c <= 0, n_full, c - 1)

    def body(c, carry):
        m, alpha, acc = carry
        pv = product(before(c), [p_sc[hl] for hl in heads])
        s_next = [score() for score in scores(c + 1)]
        m_next = tuple(jnp.maximum(m[hl], _key_max(s_next[hl])) for hl in heads)
        alpha_next = tuple(jnp.exp2(m[hl] - m_next[hl]) for hl in heads)
        for hl in heads:
            p_sc[hl] = _probabilities(s_sc[hl], m[hl])
            s_sc[hl] = s_next[hl]
        acc = tuple(alpha[hl] * (acc[hl] + pv[hl]) for hl in heads)
        return m_next, alpha_next, acc

    init = (tuple(m), tuple(alpha), tuple(jnp.zeros((ACC_ROWS, TQ), F32) for _ in heads))
    m, alpha, acc = lax.fori_loop(0, n_full - 1, body, init)

    last = n_full - 1
    pv = product(before(last), [p_sc[hl] for hl in heads])
    tail = product(jnp.maximum(last, 0), [_probabilities(s_sc[hl], m[hl]) for hl in heads])
    return [alpha[hl] * (acc[hl] + pv[hl]) + tail[hl] for hl in heads]


def _query_t(q):
    return q.astype(F32).T


def _head_rows(q_t, hl):
    sub = lax.broadcasted_iota(jnp.int32, q_t.shape, 0)
    return jnp.where((sub >= hl * HEAD_DIM) & (sub < (hl + 1) * HEAD_DIM), q_t, 0.0)


def _finish(accs):
    blocks = []
    for p in range(len(accs) // HEADS_PER_LANE_BLOCK):
        pair = accs[p * HEADS_PER_LANE_BLOCK:(p + 1) * HEADS_PER_LANE_BLOCK]
        out_t = jnp.concatenate([acc[:HEAD_DIM] / acc[HEAD_DIM:HEAD_DIM + 1] for acc in pair], axis=0)
        blocks.append(out_t.T)
    return jnp.concatenate(blocks, axis=1)


def _chunk_scratch(n_heads):
    return [pltpu.VMEM((n_heads, CHUNK, TQ), F32), pltpu.VMEM((n_heads, CHUNK, TQ), BF16)]


def _chunk_causal(first_key, first_query):
    kpos = first_key + lax.broadcasted_iota(jnp.int32, (CHUNK, TQ), 0)
    qpos = first_query + lax.broadcasted_iota(jnp.int32, (CHUNK, TQ), 1)
    return kpos <= qpos


def _fox_kernel(q_ref, k_ref, kx_ref, vt_ref, o_ref, s_sc, p_sc):
    i = pl.program_id(1)
    n_pairs = FOX_COLS // LANES
    sub = lax.broadcasted_iota(jnp.int32, (LANES, TQ), 0)
    q_aug = []
    for p in range(n_pairs):
        q_t = _query_t(q_ref[0, :, p * LANES:(p + 1) * LANES])
        for hl in range(HEADS_PER_LANE_BLOCK):
            head = HEADS_PER_LANE_BLOCK * p + hl
            ones = jnp.where((sub % N_HEADS_FOX == head) & (sub < FORGET_PIECES * N_HEADS_FOX), 1.0, 0.0)
            q_aug.append(jnp.concatenate([_head_rows(q_t, hl), ones], axis=0).astype(BF16))

    def scores(c):
        off = pl.multiple_of(c * CHUNK, CHUNK)
        kx = kx_ref[0, pl.ds(off, CHUNK), :]
        out = []
        for p in range(n_pairs):
            k_aug = jnp.concatenate([k_ref[0, pl.ds(off, CHUNK), p * LANES:(p + 1) * LANES], kx], axis=1)
            out += [functools.partial(_dot, k_aug, qa)
                    for qa in q_aug[p * HEADS_PER_LANE_BLOCK:(p + 1) * HEADS_PER_LANE_BLOCK]]
        return out

    def values(c, h):
        off = pl.multiple_of(c * CHUNK, CHUNK)
        return vt_ref[0, h * HEAD_DIM:(h + 1) * HEAD_DIM, pl.ds(off, CHUNK)]

    o_ref[0] = _finish(_chunked_attention(i, N_HEADS_FOX, scores, values, s_sc, p_sc)).astype(BF16)


def _fox(q, k, kx, vt):
    B, S, _ = q.shape
    assert S % CHUNK == 0
    return pl.pallas_call(
        _fox_kernel,
        grid=(B, S // TQ),
        in_specs=[pl.BlockSpec((1, TQ, FOX_COLS), lambda b, i: (b, i, 0)),
                  pl.BlockSpec((1, S, FOX_COLS), lambda b, i: (b, 0, 0)),
                  pl.BlockSpec((1, S, LANES), lambda b, i: (b, 0, 0)),
                  pl.BlockSpec((1, FOX_COLS, S), lambda b, i: (b, 0, 0))],
        out_specs=pl.BlockSpec((1, TQ, FOX_COLS), lambda b, i: (b, i, 0)),
        out_shape=jax.ShapeDtypeStruct((B, S, FOX_COLS), BF16),
        scratch_shapes=_chunk_scratch(N_HEADS_FOX),
        compiler_params=_params("parallel", "arbitrary"),
        name="fox",
    )(q, k, kx, vt)


WINDOW_KEYS = DIL_PATTERNS[0][0] // DIL_PATTERNS[0][1]
WQ = 256
assert all(w // d == WINDOW_KEYS for w, d in DIL_PATTERNS) and WINDOW_KEYS <= WQ


WINDOW_CHAINS = 4


def _window_kernel(q_ref, kc_ref, kp_ref, vc_ref, vp_ref, o_ref, lse_ref, *, dil, n_res, n_sub):
    a = pl.program_id(1)
    rg = pl.program_id(2)
    kr = lax.broadcasted_iota(jnp.int32, (WQ, WQ), 0)
    qc = lax.broadcasted_iota(jnp.int32, (WQ, WQ), 1)
    valid_d = (kr <= qc) & (qc - kr <= WINDOW_KEYS)
    krp = lax.broadcasted_iota(jnp.int32, (WINDOW_KEYS, WQ), 0)
    qcp = lax.broadcasted_iota(jnp.int32, (WINDOW_KEYS, WQ), 1)
    valid_p = krp >= qcp
    valid_first = valid_p & (a > 0)

    chains = [(rr, t) for rr in range(n_res) for t in range(n_sub)]
    heads = range(HEADS_PER_LANE_BLOCK)
    raw, vals = [], []
    for rr, t in chains:
        cols = slice(rr * LANES, (rr + 1) * LANES)
        sub = slice(t * WQ, (t + 1) * WQ)
        q_t = _query_t(q_ref[0, sub, cols])
        if t == 0:
            kp, vp = kp_ref[0, :, cols], vp_ref[0, :, cols]
        else:
            before = slice(t * WQ - WINDOW_KEYS, t * WQ)
            kp, vp = kc_ref[0, before, cols], vc_ref[0, before, cols]
        kd = kc_ref[0, sub, cols]
        vals.append((vc_ref[0, sub, cols], vp))
        q_m = [_head_rows(q_t, hl).astype(BF16) for hl in heads]
        raw.append([(_dot(kd, qm), _dot(kp, qm)) for qm in q_m])

    probs, stats = [], []
    for n, (rr, t) in enumerate(chains):
        for hl in heads:
            s_d = jnp.where(valid_d, raw[n][hl][0], NEG_INF)
            s_p = jnp.where(valid_first if t == 0 else valid_p, raw[n][hl][1], NEG_INF)
            m = jnp.maximum(_key_max(s_d), _key_max(s_p))
            p_d = jnp.exp2(s_d - m)
            p_p = jnp.exp2(s_p - m)
            l = _reduce_keys(p_d, jnp.add, jnp.sum) + _reduce_keys(p_p, jnp.add, jnp.sum)
            probs.append((p_d.astype(BF16), p_p.astype(BF16)))
            stats.append((m, l))

    accs = [_dot_tn(vals[n][0], probs[n * len(heads) + hl][0]) + _dot_tn(vals[n][1], probs[n * len(heads) + hl][1])
            for n in range(len(chains)) for hl in heads]

    for n, (rr, t) in enumerate(chains):
        outs, lses = [], []
        for hl in heads:
            m, l = stats[n * len(heads) + hl]
            outs.append(accs[n * len(heads) + hl][hl * HEAD_DIM:(hl + 1) * HEAD_DIM] / l)
            lses.append(jnp.broadcast_to(m + jnp.log2(l), (HEAD_DIM, WQ)))
        r = rg * n_res + rr
        rows = pl.ds(t * WQ * dil + r, WQ, stride=dil) if dil > 1 else pl.ds(t * WQ, WQ)
        o_ref[0, rows, :] = jnp.concatenate(outs, axis=0).T
        lse_ref[0, rows, :] = jnp.concatenate(lses, axis=0).T


def _window(q, k, v, group):
    dil = DIL_PATTERNS[group][1]
    B, rows, width = q.shape
    S = rows * dil
    n_res = min(dil, 2)
    n_sub = WINDOW_CHAINS // n_res
    tile = n_sub * WQ
    assert rows % tile == 0 and tile % WINDOW_KEYS == 0 and dil % n_res == 0
    col0, col_step = (FOX_COLS // LANES + group, 0) if dil == 1 else (0, 1)
    assert width == (MIX_WIDTH if dil == 1 else dil * LANES)
    cur = pl.BlockSpec((1, tile, n_res * LANES), lambda b, a, rg: (b, a, col0 + col_step * rg))
    per = tile // WINDOW_KEYS
    prev = pl.BlockSpec((1, WINDOW_KEYS, n_res * LANES),
                        lambda b, a, rg: (b, jnp.maximum(a * per - 1, 0), col0 + col_step * rg))
    out = pl.BlockSpec((1, dil * tile, LANES), lambda b, a, rg: (b, a, 0))
    o, lse = pl.pallas_call(
        functools.partial(_window_kernel, dil=dil, n_res=n_res, n_sub=n_sub),
        grid=(B, rows // tile, dil // n_res),
        in_specs=[cur, cur, prev, cur, prev],
        out_specs=[out, out],
        out_shape=[jax.ShapeDtypeStruct((B, S, LANES), F32)] * 2,
        compiler_params=_params("parallel", "arbitrary", "arbitrary"),
        name=f"window{group}",
    )(q, k, k, v, v)
    return o.reshape(B * S, LANES), lse.reshape(B * S, LANES)


MOBA_PAIRS = MOBA_COLS // LANES


def _moba_kernel(*refs):
    q_refs, k_refs, vt_refs = refs[:MOBA_PAIRS], refs[MOBA_PAIRS:2 * MOBA_PAIRS], refs[2 * MOBA_PAIRS:3 * MOBA_PAIRS]
    km_ref, o_ref, s_sc, p_sc = refs[3 * MOBA_PAIRS:]
    i = pl.program_id(1)
    n_blocks = km_ref.shape[1]
    blk = lax.broadcasted_iota(jnp.int32, (n_blocks, TQ), 0)
    own = i * (TQ // MOBA_BLOCK) + lax.broadcasted_iota(jnp.int32, (n_blocks, TQ), 1) // MOBA_BLOCK
    past = blk < own
    never = jnp.full((LANES - n_blocks, TQ), NEG_INF, F32)
    q_aug = []
    for p in range(MOBA_PAIRS):
        q_t = _query_t(q_refs[p][0])
        km = km_ref[0, :, p * LANES:(p + 1) * LANES]
        for hl in range(HEADS_PER_LANE_BLOCK):
            qm = _head_rows(q_t, hl)
            gate = jnp.where(past, _dot(km, qm.astype(BF16)), NEG_INF)
            sel = blk == own
            for _ in range(MOBA_TOPK):
                top = jnp.max(gate, axis=0, keepdims=True)
                idx = jnp.min(jnp.where(gate == top, blk, n_blocks), axis=0, keepdims=True)
                hit = blk == idx
                sel = sel | (hit & past)
                gate = jnp.where(hit, -jnp.inf, gate)
            sel_bias = jnp.where(sel, 0.0, NEG_INF)
            q_aug.append(jnp.concatenate([qm, sel_bias, never], axis=0).astype(BF16))

    row = lax.broadcasted_iota(jnp.int32, (CHUNK, LANES), 0)
    lane = lax.broadcasted_iota(jnp.int32, (CHUNK, LANES), 1)
    blocks_per_chunk = CHUNK // MOBA_BLOCK

    def scores(c):
        off = pl.multiple_of(c * CHUNK, CHUNK)
        onehot = jnp.where(lane == c * blocks_per_chunk + row // MOBA_BLOCK, 1.0, 0.0).astype(BF16)
        out = []
        for p in range(MOBA_PAIRS):
            k_aug = jnp.concatenate([k_refs[p][0, pl.ds(off, CHUNK), :], onehot], axis=1)
            out += [functools.partial(_dot, k_aug, qa)
                    for qa in q_aug[p * HEADS_PER_LANE_BLOCK:(p + 1) * HEADS_PER_LANE_BLOCK]]
        return out

    def values(c, h):
        off = pl.multiple_of(c * CHUNK, CHUNK)
        p, hl = divmod(h, HEADS_PER_LANE_BLOCK)
        return vt_refs[p][0, hl * HEAD_DIM:(hl + 1) * HEAD_DIM, pl.ds(off, CHUNK)]

    o_ref[0] = _finish(_chunked_attention(i, N_HEADS_MOBA, scores, values, s_sc, p_sc)).astype(BF16)


def _moba(q, k, vt, km):
    B, S, _ = q.shape
    n_blocks = S // MOBA_BLOCK
    assert TQ % MOBA_BLOCK == 0 and CHUNK % MOBA_BLOCK == 0 and S % CHUNK == 0
    assert n_blocks <= LANES and n_blocks % (2 * SUBLANES) == 0
    first = (FOX_COLS + DIL_COLS) // LANES
    at = lambda f: [pl.BlockSpec(*f(first + p)) for p in range(MOBA_PAIRS)]
    return pl.pallas_call(
        _moba_kernel,
        grid=(B, S // TQ),
        in_specs=at(lambda c: ((1, TQ, LANES), lambda b, i: (b, i, c)))
        + at(lambda c: ((1, S, LANES), lambda b, i: (b, 0, c)))
        + at(lambda c: ((1, LANES, S), lambda b, i: (b, c, 0)))
        + [pl.BlockSpec((1, n_blocks, MOBA_COLS), lambda b, i: (b, 0, 0))],
        out_specs=pl.BlockSpec((1, TQ, MOBA_COLS), lambda b, i: (b, i, 0)),
        out_shape=jax.ShapeDtypeStruct((B, S, MOBA_COLS), BF16),
        scratch_shapes=_chunk_scratch(N_HEADS_MOBA),
        compiler_params=_params("parallel", "arbitrary"),
        name="moba",
    )(*([q] * MOBA_PAIRS + [k] * MOBA_PAIRS + [vt] * MOBA_PAIRS + [km]))


def _merge_kernel(ya_ref, o0_ref, o1_ref, o2_ref, l0_ref, l1_ref, l2_ref, yc_ref, gates_ref, x_ref,
                  wa_ref, wb_ref, wc_ref, wo_ref, g_ref, o_ref):
    lses = [l0_ref[...], l1_ref[...], l2_ref[...]]
    top = jnp.maximum(jnp.maximum(lses[0], lses[1]), lses[2])
    wts = [jnp.exp2(t - top) for t in lses]
    den = wts[0] + wts[1] + wts[2]
    yb = (wts[0] * o0_ref[...] + wts[1] * o1_ref[...] + wts[2] * o2_ref[...]) / den
    merged = None
    for c, (y, w_ref) in enumerate(((ya_ref[...], wa_ref), (yb.astype(BF16), wb_ref), (yc_ref[...], wc_ref))):
        gate = gates_ref[:, c * D_MODEL:(c + 1) * D_MODEL].astype(F32)
        term = gate * _dot(y, w_ref[...])
        merged = term if merged is None else merged + term
    out = _dot(merged.astype(BF16), wo_ref[...])
    o_ref[...] = x_ref[...] + _rms(out, g_ref[...])


def _merge(ya, ob, lb, yc, gates, x, wa, wb, wc, wo, g):
    T, D = x.shape
    tok = lambda w: pl.BlockSpec((TM, w), lambda i: (i, 0))
    return pl.pallas_call(
        _merge_kernel,
        grid=(T // TM,),
        in_specs=[tok(FOX_COLS)] + [tok(LANES)] * 6 + [tok(MOBA_COLS), tok(N_GATE_COLS), tok(D),
                  _resident(wa.shape), _resident(wb.shape), _resident(wc.shape), _resident(wo.shape),
                  _resident((1, D))],
        out_specs=tok(D),
        out_shape=jax.ShapeDtypeStruct((T, D), F32),
        compiler_params=_params("parallel"),
        name="merge",
    )(ya, *ob, *lb, yc, gates, x, wa, wb, wc, wo, g)


MXU_TILE = 256
assert D_FF % MXU_TILE == 0
FF_SPLITS = (0, (D_FF // MXU_TILE + 1) // 2 * MXU_TILE, D_FF)


def _ffn_kernel(x_ref, p_ref, g_pre_ref, wgate_ref, wup_ref, wdown_ref, g_post_ref,
                wple_ref, wpg_ref, g_ple_ref, o_ref):
    x = x_ref[...]
    hb = _rms(x, g_pre_ref[...]).astype(BF16)
    down = None
    for lo, hi in zip(FF_SPLITS[:-1], FF_SPLITS[1:]):
        cols = slice(lo, hi)
        gate = _dot(hb, wgate_ref[:, cols])
        up = _dot(hb, wup_ref[:, cols])
        ff = (gate * jax.nn.sigmoid(gate) * up).astype(BF16)
        part = _dot(ff, wdown_ref[cols, :])
        down = part if down is None else down + part
    x = x + _rms(down, g_post_ref[...])
    ple = _dot(p_ref[...].astype(BF16), wple_ref[...]) * jax.nn.sigmoid(_dot(x.astype(BF16), wpg_ref[...]))
    o_ref[...] = x + _rms(ple, g_ple_ref[...])


def _ffn(x, p, g_pre, wgate, wup, wdown, g_post, wple, wpg, g_ple):
    T, D = x.shape
    tok = lambda w: pl.BlockSpec((TM, w), lambda i: (i, 0))
    vec = _resident((1, D))
    return pl.pallas_call(
        _ffn_kernel,
        grid=(T // TM,),
        in_specs=[tok(D), tok(PLE_DIM), vec, _resident(wgate.shape), _resident(wup.shape),
                  _resident(wdown.shape), vec, _resident(wple.shape), _resident(wpg.shape), vec],
        out_specs=tok(D),
        out_shape=jax.ShapeDtypeStruct((T, D), F32),
        compiler_params=_params("parallel"),
        name="ffn",
    )(x, p, g_pre, wgate, wup, wdown, g_post, wple, wpg, g_ple)


def _rope_tables(seq):
    inv = 1.0 / (ROPE_THETA ** (jnp.arange(0, HEAD_DIM, 2, dtype=F32) / HEAD_DIM))
    ang = jnp.arange(seq, dtype=F32)[:, None] * inv[None, :]
    cos, sin = jnp.cos(ang), jnp.sin(ang)
    reps = LANES // HEAD_DIM
    cos_t = jnp.tile(jnp.concatenate([cos, cos], axis=1), (1, reps))
    sin_t = jnp.tile(jnp.concatenate([-sin, sin], axis=1), (1, reps))
    return cos_t, sin_t


def kernel(x, p, g_mix_pre, w_in, b_f, w_br_a, w_br_b, w_br_c, w_out, g_mix_post, g_ffn_pre,
           w_ffn_gate, w_ffn_up, w_ffn_down, g_ffn_post, w_ple, w_ple_gate, g_ple_post):
    B, S, D = x.shape
    T = B * S
    depth = w_in.shape[0]
    cos_t, sin_t = _rope_tables(S)
    row = lambda g: g.reshape(1, -1)
    f0 = 3 * MIX_WIDTH
    for i in range(depth):
        wq = w_in[i, :, :MIX_WIDTH].astype(BF16)
        wk = w_in[i, :, MIX_WIDTH:2 * MIX_WIDTH].astype(BF16)
        wv = w_in[i, :, 2 * MIX_WIDTH:f0].astype(BF16)
        wf = jnp.pad(w_in[i, :, f0:f0 + N_HEADS_FOX], ((0, 0), (0, LANES - N_HEADS_FOX))).astype(BF16)
        bf = jnp.pad(b_f[i], (0, LANES - N_HEADS_FOX)).reshape(1, LANES)
        wg = w_in[i, :, f0 + N_HEADS_FOX:].astype(BF16)

        (q, k, v, vt, kx, gates, kmean), residue_views = _proj(
            x, row(g_mix_pre[i]), wq, wk, wv, wf, bf, wg, cos_t, sin_t)
        km = kmean.reshape(B, S // MOBA_BLOCK, MOBA_COLS).astype(BF16)
        y_a = _fox(q, k, kx, vt)
        win_in = {g: residue_views[3 * n:3 * n + 3] for n, (g, _) in enumerate(STRIDED_GROUPS)}
        win = [_window(*win_in.get(g, (q, k, v)), g) for g in range(len(DIL_PATTERNS))]
        y_c = _moba(q, k, vt, km)

        x2 = _merge(y_a.reshape(T, -1), [o for o, _ in win], [l for _, l in win], y_c.reshape(T, -1),
                    gates.reshape(T, -1), x.reshape(T, D),
                    w_br_a[i].astype(BF16), w_br_b[i].astype(BF16), w_br_c[i].astype(BF16),
                    w_out[i].astype(BF16), row(g_mix_post[i]))
        x3 = _ffn(x2, p[i].reshape(T, PLE_DIM), row(g_ffn_pre[i]),
                  w_ffn_gate[i].astype(BF16), w_ffn_up[i].astype(BF16), w_ffn_down[i].astype(BF16),
                  row(g_ffn_post[i]), w_ple[i].astype(BF16), w_ple_gate[i].astype(BF16),
                  row(g_ple_post[i]))
        x = x3.reshape(B, S, D)
    return x
```

```python
import functools
import math

import numpy as np
import jax
import jax.numpy as jnp
from jax import lax
from jax.experimental import pallas as pl
from jax.experimental.pallas import tpu as pltpu

D_MODEL = 1024
HEAD_DIM = 64
N_HEADS_FOX = 4
DIL_PATTERNS = ((128, 1), (512, 4), (2048, 16))
N_HEADS_MOBA = 6
MIX_WIDTH = 1024
MOBA_BLOCK = 256
MOBA_TOPK = 3
PLE_DIM = 256
D_FF = 2816
ROPE_THETA = 10000.0
RMS_EPS = 1e-6
NEG_INF = -1e30
LOG2E = math.log2(math.e)
Q_SCALE = HEAD_DIM ** -0.5 * LOG2E

LANES = 128
SUBLANES = 8
HEADS_PER_LANE_BLOCK = LANES // HEAD_DIM
N_LANE_BLOCKS = MIX_WIDTH // LANES
FOX_COLS = N_HEADS_FOX * HEAD_DIM
DIL_COLS = len(DIL_PATTERNS) * 2 * HEAD_DIM
MOBA_COLS = N_HEADS_MOBA * HEAD_DIM
N_GATE_COLS = 3 * D_MODEL
FORGET_PIECES = 3

TM = 512
TQ = 512
CHUNK = 512
VMEM_LIMIT = 56 * 1024 * 1024

F32 = jnp.float32
BF16 = jnp.bfloat16


def _dot(a, b):
    return jnp.dot(a, b, preferred_element_type=F32)


def _dot_nt(a, b):
    return lax.dot_general(a, b, (((1,), (1,)), ((), ())), preferred_element_type=F32)


def _dot_tn(a, b):
    return lax.dot_general(a, b, (((0,), (0,)), ((), ())), preferred_element_type=F32)


def _rms(x, g):
    var = jnp.mean(x * x, axis=-1, keepdims=True)
    return x * lax.rsqrt(var + RMS_EPS) * g


def _split3(x):
    hi = x.astype(BF16).astype(F32)
    r1 = x - hi
    mid = r1.astype(BF16).astype(F32)
    lo = (r1 - mid).astype(BF16).astype(F32)
    return hi, mid, lo


def _resident(shape):
    nd = len(shape)
    return pl.BlockSpec(shape, lambda *_: (0,) * nd, pipeline_mode=pl.Buffered(1))


def _params(*sem):
    return pltpu.CompilerParams(dimension_semantics=sem, vmem_limit_bytes=VMEM_LIMIT)


STRIDED_GROUPS = tuple((g, dil) for g, (_, dil) in enumerate(DIL_PATTERNS) if dil > 1)


def _proj_kernel(x_ref, g_ref, wq_ref, wk_ref, wv_ref, wf_ref, bf_ref, wg_ref, cos_ref, sin_ref,
                 q_ref, k_ref, v_ref, vt_ref, kx_ref, gates_ref, kmean_ref, *rest):
    res_refs, (carry_ref, perm_sc) = rest[:-2], rest[-2:]
    i = pl.program_id(1)

    def scatter_residues(t, which, blk):
        for n, (g, dil) in enumerate(STRIDED_GROUPS):
            if blk == FOX_COLS // LANES + g:
                slot = 3 * n + which
                perm_sc[slot] = t
                for r in range(dil):
                    res_refs[slot][0, :, r * LANES:(r + 1) * LANES] = perm_sc[
                        slot, pl.ds(r, TM // dil, stride=dil), :].astype(BF16)

    @pl.when(i == 0)
    def _():
        carry_ref[...] = jnp.zeros_like(carry_ref)

    hb = _rms(x_ref[0], g_ref[...]).astype(BF16)
    cos = cos_ref[...]
    sin = sin_ref[...]
    lane = lax.broadcasted_iota(jnp.int32, (TM, LANES), 1)
    first_half = (lane % HEAD_DIM) < (HEAD_DIM // 2)

    def rope(t):
        rot = jnp.where(first_half, pltpu.roll(t, LANES - HEAD_DIM // 2, 1),
                        pltpu.roll(t, HEAD_DIM // 2, 1))
        return t * cos + rot * sin

    first_rope_blk = FOX_COLS // LANES
    first_moba_blk = (FOX_COLS + DIL_COLS) // LANES

    f = _dot(hb, wf_ref[...]) + bf_ref[...]
    ls = jnp.minimum(f, 0.0) - jnp.log1p(jnp.exp(-jnp.abs(f)))
    ls = jnp.where(lane < N_HEADS_FOX, ls, 0.0)
    row = lax.broadcasted_iota(jnp.int32, (TM, TM), 0)
    col = lax.broadcasted_iota(jnp.int32, (TM, TM), 1)
    tri = jnp.where(col <= row, 1.0, 0.0).astype(BF16)
    ls_hi, ls_mid, ls_lo = _split3(ls)

    q = _dot(hb, wq_ref[...])
    cs = _dot(tri, ls_hi.astype(BF16)) + _dot(tri, ls_mid.astype(BF16)) + _dot(tri, ls_lo.astype(BF16))
    for c in range(N_LANE_BLOCKS):
        t = q[:, c * LANES:(c + 1) * LANES]
        if c >= first_rope_blk:
            t = rope(t)
        t = t * Q_SCALE
        q_ref[0, :, c * LANES:(c + 1) * LANES] = t.astype(BF16)
        scatter_residues(t, 0, c)

    k = _dot(hb, wk_ref[...])

    cs = cs + carry_ref[...]
    carry_ref[...] = cs[TM - 1:TM, :]
    hi, mid, lo = _split3(-LOG2E * cs)
    pieces = jnp.where(lane < N_HEADS_FOX, hi,
                       jnp.where(lane < 2 * N_HEADS_FOX, pltpu.roll(mid, N_HEADS_FOX, 1),
                                 pltpu.roll(lo, 2 * N_HEADS_FOX, 1)))
    kx_ref[0] = pieces.astype(BF16)

    for c in range(N_LANE_BLOCKS):
        t = k[:, c * LANES:(c + 1) * LANES]
        if c >= first_rope_blk:
            t = rope(t)
        k_ref[0, :, c * LANES:(c + 1) * LANES] = t.astype(BF16)
        scatter_residues(t, 1, c)
        if c >= first_moba_blk:
            cm = c - first_moba_blk
            for r in range(TM // MOBA_BLOCK):
                kmean_ref[0, 0, r:r + 1, cm * LANES:(cm + 1) * LANES] = jnp.mean(
                    t[r * MOBA_BLOCK:(r + 1) * MOBA_BLOCK], axis=0, keepdims=True)

    v = _dot(hb, wv_ref[...])
    v_ref[0] = v.astype(BF16)
    for c in range(N_LANE_BLOCKS):
        scatter_residues(v[:, c * LANES:(c + 1) * LANES], 2, c)
    vt_ref[0] = v.T.astype(BF16)

    for c in range(N_GATE_COLS // D_MODEL):
        z = _dot(hb, wg_ref[:, c * D_MODEL:(c + 1) * D_MODEL])
        gates_ref[0, :, c * D_MODEL:(c + 1) * D_MODEL] = jax.nn.sigmoid(z).astype(BF16)


def _proj(x, g, wq, wk, wv, wf, bf, wg, cos_t, sin_t):
    B, S, D = x.shape
    n_t = S // TM
    tok = lambda w: pl.BlockSpec((1, TM, w), lambda b, i: (b, i, 0))
    res_specs, res_shapes = [], []
    for _, dil in STRIDED_GROUPS:
        assert TM % (dil * 2 * SUBLANES) == 0 and S % dil == 0
        res_specs += [pl.BlockSpec((1, TM // dil, dil * LANES), lambda b, i: (b, i, 0))] * 3
        res_shapes += [jax.ShapeDtypeStruct((B, S // dil, dil * LANES), BF16)] * 3
    outs = pl.pallas_call(
        _proj_kernel,
        grid=(B, n_t),
        in_specs=[tok(D), _resident((1, D)), _resident((D, MIX_WIDTH)), _resident((D, MIX_WIDTH)),
                  _resident((D, MIX_WIDTH)),
                  _resident((D, LANES)), _resident((1, LANES)), _resident((D, N_GATE_COLS)),
                  pl.BlockSpec((TM, LANES), lambda b, i: (i, 0)),
                  pl.BlockSpec((TM, LANES), lambda b, i: (i, 0))],
        out_specs=[tok(MIX_WIDTH), tok(MIX_WIDTH), tok(MIX_WIDTH),
                   pl.BlockSpec((1, MIX_WIDTH, TM), lambda b, i: (b, 0, i)),
                   tok(LANES), tok(N_GATE_COLS),
                   pl.BlockSpec((1, 1, TM // MOBA_BLOCK, MOBA_COLS), lambda b, i: (b, i, 0, 0))] + res_specs,
        out_shape=[jax.ShapeDtypeStruct((B, S, MIX_WIDTH), BF16)] * 3
        + [jax.ShapeDtypeStruct((B, MIX_WIDTH, S), BF16),
           jax.ShapeDtypeStruct((B, S, LANES), BF16),
           jax.ShapeDtypeStruct((B, S, N_GATE_COLS), BF16),
           jax.ShapeDtypeStruct((B, n_t, TM // MOBA_BLOCK, MOBA_COLS), F32)] + res_shapes,
        scratch_shapes=[pltpu.VMEM((1, LANES), F32), pltpu.VMEM((len(res_specs), TM, LANES), F32)],
        compiler_params=_params("arbitrary", "arbitrary"),
        name="proj",
    )(x, g, wq, wk, wv, wf, bf, wg, cos_t, sin_t)
    return outs[:7], outs[7:]


def _reduce_keys(x, op, reduce):
    n = x.shape[0]
    while n > SUBLANES and n % (2 * SUBLANES) == 0:
        n //= 2
        x = op(x[:n], x[n:])
    return reduce(x, axis=0, keepdims=True)


def _key_max(s):
    return _reduce_keys(s, jnp.maximum, jnp.max)


def _probabilities(s, m):
    return jnp.exp2((s - m).astype(BF16))


ACC_ROWS = HEAD_DIM + 2 * SUBLANES


def _values_with_ones(vt):
    return jnp.concatenate([vt, jnp.ones((ACC_ROWS - HEAD_DIM, vt.shape[1]), BF16)], axis=0)


def _chunked_attention(i, n_heads, scores, values, s_sc, p_sc):
    heads = range(n_heads)
    n_full = (i * TQ) // CHUNK

    causal = _chunk_causal(n_full * CHUNK, i * TQ)
    s_diag = [score() for score in scores(n_full)]
    s_first = [score() for score in scores(0)]
    m, alpha = [], []
    for hl in heads:
        sd = jnp.where(causal, s_diag[hl], NEG_INF)
        m_diag = _key_max(sd)
        p_sc[hl] = _probabilities(sd, m_diag)
        s_sc[hl] = s_first[hl]
        m.append(jnp.where(n_full > 0, jnp.maximum(m_diag, _key_max(s_first[hl])), m_diag))
        alpha.append(jnp.exp2(m_diag - m[hl]))

    def product(c, p):
        return [_dot(_values_with_ones(values(c, hl)), p[hl]) for hl in heads]

    def before(c):
        return jnp.where(c <= 0, n_full, c - 1)

    def body(c, carry):
        m, alpha, acc = carry
        pv = product(before(c), [p_sc[hl] for hl in heads])
        s_next = [score() for score in scores(c + 1)]
        m_next = tuple(jnp.maximum(m[hl], _key_max(s_next[hl])) for hl in heads)
        alpha_next = tuple(jnp.exp2(m[hl] - m_next[hl]) for hl in heads)
        for hl in heads:
            p_sc[hl] = _probabilities(s_sc[hl], m[hl])
            s_sc[hl] = s_next[hl]
        acc = tuple(alpha[hl] * (acc[hl] + pv[hl]) for hl in heads)
        return m_next, alpha_next, acc

    init = (tuple(m), tuple(alpha), tuple(jnp.zeros((ACC_ROWS, TQ), F32) for _ in heads))
    m, alpha, acc = lax.fori_loop(0, n_full - 1, body, init)

    last = n_full - 1
    pv = product(before(last), [p_sc[hl] for hl in heads])
    m_tail = [jnp.where(n_full > 0, m[hl], -NEG_INF) for hl in heads]
    tail = product(jnp.maximum(last, 0), [_probabilities(s_sc[hl], m_tail[hl]) for hl in heads])
    return [alpha[hl] * (acc[hl] + pv[hl]) + tail[hl] for hl in heads]


def _query_t(q):
    return q.astype(F32).T


def _head_rows(q_t, hl):
    sub = lax.broadcasted_iota(jnp.int32, q_t.shape, 0)
    return jnp.where((sub >= hl * HEAD_DIM) & (sub < (hl + 1) * HEAD_DIM), q_t, 0.0)


def _finish(accs):
    blocks = []
    for p in range(len(accs) // HEADS_PER_LANE_BLOCK):
        pair = accs[p * HEADS_PER_LANE_BLOCK:(p + 1) * HEADS_PER_LANE_BLOCK]
        out_t = jnp.concatenate([acc[:HEAD_DIM] / acc[HEAD_DIM:HEAD_DIM + 1] for acc in pair], axis=0)
        blocks.append(out_t.T)
    return jnp.concatenate(blocks, axis=1)


def _chunk_scratch(n_heads):
    return [pltpu.VMEM((n_heads, CHUNK, TQ), F32), pltpu.VMEM((n_heads, CHUNK, TQ), BF16)]


def _chunk_causal(first_key, first_query):
    kpos = first_key + lax.broadcasted_iota(jnp.int32, (CHUNK, TQ), 0)
    qpos = first_query + lax.broadcasted_iota(jnp.int32, (CHUNK, TQ), 1)
    return kpos <= qpos


def _fox_kernel(q_ref, k_ref, kx_ref, vt_ref, o_ref, s_sc, p_sc):
    i = pl.program_id(1)
    n_pairs = FOX_COLS // LANES
    sub = lax.broadcasted_iota(jnp.int32, (LANES, TQ), 0)
    q_aug = []
    for p in range(n_pairs):
        q_t = _query_t(q_ref[0, :, p * LANES:(p + 1) * LANES])
        for hl in range(HEADS_PER_LANE_BLOCK):
            head = HEADS_PER_LANE_BLOCK * p + hl
            ones = jnp.where((sub % N_HEADS_FOX == head) & (sub < FORGET_PIECES * N_HEADS_FOX), 1.0, 0.0)
            q_aug.append(jnp.concatenate([_head_rows(q_t, hl), ones], axis=0).astype(BF16))

    def scores(c):
        off = pl.multiple_of(c * CHUNK, CHUNK)
        kx = kx_ref[0, pl.ds(off, CHUNK), :]
        out = []
        for p in range(n_pairs):
            k_aug = jnp.concatenate([k_ref[0, pl.ds(off, CHUNK), p * LANES:(p + 1) * LANES], kx], axis=1)
            out += [functools.partial(_dot, k_aug, qa)
                    for qa in q_aug[p * HEADS_PER_LANE_BLOCK:(p + 1) * HEADS_PER_LANE_BLOCK]]
        return out

    def values(c, h):
        off = pl.multiple_of(c * CHUNK, CHUNK)
        return vt_ref[0, h * HEAD_DIM:(h + 1) * HEAD_DIM, pl.ds(off, CHUNK)]

    o_ref[0] = _finish(_chunked_attention(i, N_HEADS_FOX, scores, values, s_sc, p_sc)).astype(BF16)


def _fox(q, k, kx, vt):
    B, S, _ = q.shape
    assert S % CHUNK == 0
    return pl.pallas_call(
        _fox_kernel,
        grid=(B, S // TQ),
        in_specs=[pl.BlockSpec((1, TQ, FOX_COLS), lambda b, i: (b, i, 0)),
                  pl.BlockSpec((1, S, FOX_COLS), lambda b, i: (b, 0, 0)),
                  pl.BlockSpec((1, S, LANES), lambda b, i: (b, 0, 0)),
                  pl.BlockSpec((1, FOX_COLS, S), lambda b, i: (b, 0, 0))],
        out_specs=pl.BlockSpec((1, TQ, FOX_COLS), lambda b, i: (b, i, 0)),
        out_shape=jax.ShapeDtypeStruct((B, S, FOX_COLS), BF16),
        scratch_shapes=_chunk_scratch(N_HEADS_FOX),
        compiler_params=_params("parallel", "arbitrary"),
        name="fox",
    )(q, k, kx, vt)


WINDOW_KEYS = DIL_PATTERNS[0][0] // DIL_PATTERNS[0][1]
WQ = 256
assert all(w // d == WINDOW_KEYS for w, d in DIL_PATTERNS) and WINDOW_KEYS <= WQ


WINDOW_CHAINS = 4


def _window_kernel(q_ref, kc_ref, kp_ref, vc_ref, vp_ref, o_ref, lse_ref, *, dil, n_res, n_sub):
    a = pl.program_id(1)
    rg = pl.program_id(2)
    kr = lax.broadcasted_iota(jnp.int32, (WQ, WQ), 0)
    qc = lax.broadcasted_iota(jnp.int32, (WQ, WQ), 1)
    valid_d = (kr <= qc) & (qc - kr <= WINDOW_KEYS)
    krp = lax.broadcasted_iota(jnp.int32, (WINDOW_KEYS, WQ), 0)
    qcp = lax.broadcasted_iota(jnp.int32, (WINDOW_KEYS, WQ), 1)
    valid_p = krp >= qcp
    valid_first = valid_p & (a > 0)

    chains = [(rr, t) for rr in range(n_res) for t in range(n_sub)]
    heads = range(HEADS_PER_LANE_BLOCK)
    raw, vals = [], []
    for rr, t in chains:
        cols = slice(rr * LANES, (rr + 1) * LANES)
        sub = slice(t * WQ, (t + 1) * WQ)
        q_t = _query_t(q_ref[0, sub, cols])
        if t == 0:
            kp, vp = kp_ref[0, :, cols], vp_ref[0, :, cols]
        else:
            before = slice(t * WQ - WINDOW_KEYS, t * WQ)
            kp, vp = kc_ref[0, before, cols], vc_ref[0, before, cols]
        kd = kc_ref[0, sub, cols]
        vals.append((vc_ref[0, sub, cols], vp))
        q_m = [_head_rows(q_t, hl).astype(BF16) for hl in heads]
        raw.append([(_dot(kd, qm), _dot(kp, qm)) for qm in q_m])

    probs, stats = [], []
    for n, (rr, t) in enumerate(chains):
        for hl in heads:
            s_d = jnp.where(valid_d, raw[n][hl][0], NEG_INF)
            s_p = jnp.where(valid_first if t == 0 else valid_p, raw[n][hl][1], NEG_INF)
            m = jnp.maximum(_key_max(s_d), _key_max(s_p))
            p_d = jnp.exp2(s_d - m)
            p_p = jnp.exp2(s_p - m)
            l = _reduce_keys(p_d, jnp.add, jnp.sum) + _reduce_keys(p_p, jnp.add, jnp.sum)
            probs.append((p_d.astype(BF16), p_p.astype(BF16)))
            stats.append((m, l))

    accs = [_dot_tn(vals[n][0], probs[n * len(heads) + hl][0]) + _dot_tn(vals[n][1], probs[n * len(heads) + hl][1])
            for n in range(len(chains)) for hl in heads]

    for n, (rr, t) in enumerate(chains):
        outs, lses = [], []
        for hl in heads:
            m, l = stats[n * len(heads) + hl]
            outs.append(accs[n * len(heads) + hl][hl * HEAD_DIM:(hl + 1) * HEAD_DIM] / l)
            lses.append(jnp.broadcast_to(m + jnp.log2(l), (HEAD_DIM, WQ)))
        r = rg * n_res + rr
        rows = pl.ds(t * WQ * dil + r, WQ, stride=dil) if dil > 1 else pl.ds(t * WQ, WQ)
        o_ref[0, rows, :] = jnp.concatenate(outs, axis=0).T
        lse_ref[0, rows, :] = jnp.concatenate(lses, axis=0).T


def _window(q, k, v, group):
    dil = DIL_PATTERNS[group][1]
    B, rows, width = q.shape
    S = rows * dil
    n_res = min(dil, 2)
    n_sub = WINDOW_CHAINS // n_res
    tile = n_sub * WQ
    assert rows % tile == 0 and tile % WINDOW_KEYS == 0 and dil % n_res == 0
    col0, col_step = (FOX_COLS // LANES + group, 0) if dil == 1 else (0, 1)
    assert width == (MIX_WIDTH if dil == 1 else dil * LANES)
    cur = pl.BlockSpec((1, tile, n_res * LANES), lambda b, a, rg: (b, a, col0 + col_step * rg))
    per = tile // WINDOW_KEYS
    prev = pl.BlockSpec((1, WINDOW_KEYS, n_res * LANES),
                        lambda b, a, rg: (b, jnp.maximum(a * per - 1, 0), col0 + col_step * rg))
    out = pl.BlockSpec((1, dil * tile, LANES), lambda b, a, rg: (b, a, 0))
    o, lse = pl.pallas_call(
        functools.partial(_window_kernel, dil=dil, n_res=n_res, n_sub=n_sub),
        grid=(B, rows // tile, dil // n_res),
        in_specs=[cur, cur, prev, cur, prev],
        out_specs=[out, out],
        out_shape=[jax.ShapeDtypeStruct((B, S, LANES), F32)] * 2,
        compiler_params=_params("parallel", "arbitrary", "arbitrary"),
        name=f"window{group}",
    )(q, k, k, v, v)
    return o.reshape(B * S, LANES), lse.reshape(B * S, LANES)


MOBA_PAIRS = MOBA_COLS // LANES


def _moba_kernel(*refs):
    q_refs, k_refs, vt_refs = refs[:MOBA_PAIRS], refs[MOBA_PAIRS:2 * MOBA_PAIRS], refs[2 * MOBA_PAIRS:3 * MOBA_PAIRS]
    km_ref, o_ref, s_sc, p_sc = refs[3 * MOBA_PAIRS:]
    i = pl.program_id(1)
    n_blocks = km_ref.shape[1]
    blk = lax.broadcasted_iota(jnp.int32, (n_blocks, TQ), 0)
    own = i * (TQ // MOBA_BLOCK) + lax.broadcasted_iota(jnp.int32, (n_blocks, TQ), 1) // MOBA_BLOCK
    past = blk < own
    never = jnp.full((LANES - n_blocks, TQ), NEG_INF, F32)
    q_aug = []
    for p in range(MOBA_PAIRS):
        q_t = _query_t(q_refs[p][0])
        km = km_ref[0, :, p * LANES:(p + 1) * LANES]
        for hl in range(HEADS_PER_LANE_BLOCK):
            qm = _head_rows(q_t, hl)
            gate = jnp.where(past, _dot(km, qm.astype(BF16)), NEG_INF)
            sel = blk == own
            for _ in range(MOBA_TOPK):
                top = jnp.max(gate, axis=0, keepdims=True)
                idx = jnp.min(jnp.where(gate == top, blk, n_blocks), axis=0, keepdims=True)
                hit = blk == idx
                sel = sel | (hit & past)
                gate = jnp.where(hit, -jnp.inf, gate)
            sel_bias = jnp.where(sel, 0.0, NEG_INF)
            q_aug.append(jnp.concatenate([qm, sel_bias, never], axis=0).astype(BF16))

    row = lax.broadcasted_iota(jnp.int32, (CHUNK, LANES), 0)
    lane = lax.broadcasted_iota(jnp.int32, (CHUNK, LANES), 1)
    blocks_per_chunk = CHUNK // MOBA_BLOCK

    def scores(c):
        off = pl.multiple_of(c * CHUNK, CHUNK)
        onehot = jnp.where(lane == c * blocks_per_chunk + row // MOBA_BLOCK, 1.0, 0.0).astype(BF16)
        out = []
        for p in range(MOBA_PAIRS):
            k_aug = jnp.concatenate([k_refs[p][0, pl.ds(off, CHUNK), :], onehot], axis=1)
            out += [functools.partial(_dot, k_aug, qa)
                    for qa in q_aug[p * HEADS_PER_LANE_BLOCK:(p + 1) * HEADS_PER_LANE_BLOCK]]
        return out

    def values(c, h):
        off = pl.multiple_of(c * CHUNK, CHUNK)
        p, hl = divmod(h, HEADS_PER_LANE_BLOCK)
        return vt_refs[p][0, hl * HEAD_DIM:(hl + 1) * HEAD_DIM, pl.ds(off, CHUNK)]

    o_ref[0] = _finish(_chunked_attention(i, N_HEADS_MOBA, scores, values, s_sc, p_sc)).astype(BF16)


def _moba(q, k, vt, km):
    B, S, _ = q.shape
    n_blocks = S // MOBA_BLOCK
    assert TQ % MOBA_BLOCK == 0 and CHUNK % MOBA_BLOCK == 0 and S % CHUNK == 0
    assert n_blocks <= LANES and n_blocks % (2 * SUBLANES) == 0
    first = (FOX_COLS + DIL_COLS) // LANES
    at = lambda f: [pl.BlockSpec(*f(first + p)) for p in range(MOBA_PAIRS)]
    return pl.pallas_call(
        _moba_kernel,
        grid=(B, S // TQ),
        in_specs=at(lambda c: ((1, TQ, LANES), lambda b, i: (b, i, c)))
        + at(lambda c: ((1, S, LANES), lambda b, i: (b, 0, c)))
        + at(lambda c: ((1, LANES, S), lambda b, i: (b, c, 0)))
        + [pl.BlockSpec((1, n_blocks, MOBA_COLS), lambda b, i: (b, 0, 0))],
        out_specs=pl.BlockSpec((1, TQ, MOBA_COLS), lambda b, i: (b, i, 0)),
        out_shape=jax.ShapeDtypeStruct((B, S, MOBA_COLS), BF16),
        scratch_shapes=_chunk_scratch(N_HEADS_MOBA),
        compiler_params=_params("parallel", "arbitrary"),
        name="moba",
    )(*([q] * MOBA_PAIRS + [k] * MOBA_PAIRS + [vt] * MOBA_PAIRS + [km]))


MXU_TILE = 256
assert D_FF % MXU_TILE == 0
FF_SPLITS = (0, (D_FF // MXU_TILE + 1) // 2 * MXU_TILE, D_FF)


def _mix_kernel(ya_ref, o0_ref, o1_ref, o2_ref, l0_ref, l1_ref, l2_ref, yc_ref, gates_ref, x_ref, p_ref,
                wa_ref, wb_ref, wc_ref, wo_ref, g_mix_ref,
                g_pre_ref, wgate_ref, wup_ref, wdown_ref, g_post_ref, wple_ref, wpg_ref, g_ple_ref, o_ref):
    lses = [l0_ref[...], l1_ref[...], l2_ref[...]]
    top = jnp.maximum(jnp.maximum(lses[0], lses[1]), lses[2])
    wts = [jnp.exp2(t - top) for t in lses]
    den = wts[0] + wts[1] + wts[2]
    yb = (wts[0] * o0_ref[...] + wts[1] * o1_ref[...] + wts[2] * o2_ref[...]) / den
    merged = None
    for c, (y, w_ref) in enumerate(((ya_ref[...], wa_ref), (yb.astype(BF16), wb_ref), (yc_ref[...], wc_ref))):
        gate = gates_ref[:, c * D_MODEL:(c + 1) * D_MODEL].astype(F32)
        term = gate * _dot(y, w_ref[...])
        merged = term if merged is None else merged + term
    x = x_ref[...] + _rms(_dot(merged.astype(BF16), wo_ref[...]), g_mix_ref[...])

    hb = _rms(x, g_pre_ref[...]).astype(BF16)
    down = None
    for lo, hi in zip(FF_SPLITS[:-1], FF_SPLITS[1:]):
        cols = slice(lo, hi)
        gate = _dot(hb, wgate_ref[:, cols])
        up = _dot(hb, wup_ref[:, cols])
        ff = (gate * jax.nn.sigmoid(gate) * up).astype(BF16)
        part = _dot(ff, wdown_ref[cols, :])
        down = part if down is None else down + part
    x = x + _rms(down, g_post_ref[...])

    ple = _dot(p_ref[...].astype(BF16), wple_ref[...]) * jax.nn.sigmoid(_dot(x.astype(BF16), wpg_ref[...]))
    o_ref[...] = x + _rms(ple, g_ple_ref[...])


def _mix(ya, ob, lb, yc, gates, x, p, mix_weights, g_mix, g_pre, ffn_weights, g_post, ple_weights, g_ple):
    T, D = x.shape
    tok = lambda w: pl.BlockSpec((TM, w), lambda i: (i, 0))
    vec = _resident((1, D))
    whole = lambda ws: [_resident(w.shape) for w in ws]
    return pl.pallas_call(
        _mix_kernel,
        grid=(T // TM,),
        in_specs=[tok(FOX_COLS)] + [tok(LANES)] * 6 + [tok(MOBA_COLS), tok(N_GATE_COLS), tok(D), tok(PLE_DIM)]
        + whole(mix_weights) + [vec, vec] + whole(ffn_weights) + [vec] + whole(ple_weights) + [vec],
        out_specs=tok(D),
        out_shape=jax.ShapeDtypeStruct((T, D), F32),
        compiler_params=_params("parallel"),
        name="mix",
    )(ya, *ob, *lb, yc, gates, x, p, *mix_weights, g_mix, g_pre, *ffn_weights, g_post, *ple_weights, g_ple)


def _rope_tables(seq):
    inv = 1.0 / (ROPE_THETA ** (jnp.arange(0, HEAD_DIM, 2, dtype=F32) / HEAD_DIM))
    ang = jnp.arange(seq, dtype=F32)[:, None] * inv[None, :]
    cos, sin = jnp.cos(ang), jnp.sin(ang)
    reps = LANES // HEAD_DIM
    cos_t = jnp.tile(jnp.concatenate([cos, cos], axis=1), (1, reps))
    sin_t = jnp.tile(jnp.concatenate([-sin, sin], axis=1), (1, reps))
    return cos_t, sin_t


def kernel(x, p, g_mix_pre, w_in, b_f, w_br_a, w_br_b, w_br_c, w_out, g_mix_post, g_ffn_pre,
           w_ffn_gate, w_ffn_up, w_ffn_down, g_ffn_post, w_ple, w_ple_gate, g_ple_post):
    B, S, D = x.shape
    T = B * S
    depth = w_in.shape[0]
    cos_t, sin_t = _rope_tables(S)
    row = lambda g: g.reshape(1, -1)
    f0 = 3 * MIX_WIDTH
    for i in range(depth):
        wq = w_in[i, :, :MIX_WIDTH].astype(BF16)
        wk = w_in[i, :, MIX_WIDTH:2 * MIX_WIDTH].astype(BF16)
        wv = w_in[i, :, 2 * MIX_WIDTH:f0].astype(BF16)
        wf = jnp.pad(w_in[i, :, f0:f0 + N_HEADS_FOX], ((0, 0), (0, LANES - N_HEADS_FOX))).astype(BF16)
        bf = jnp.pad(b_f[i], (0, LANES - N_HEADS_FOX)).reshape(1, LANES)
        wg = w_in[i, :, f0 + N_HEADS_FOX:].astype(BF16)

        (q, k, v, vt, kx, gates, kmean), residue_views = _proj(
            x, row(g_mix_pre[i]), wq, wk, wv, wf, bf, wg, cos_t, sin_t)
        km = kmean.reshape(B, S // MOBA_BLOCK, MOBA_COLS).astype(BF16)
        y_a = _fox(q, k, kx, vt)
        win_in = {g: residue_views[3 * n:3 * n + 3] for n, (g, _) in enumerate(STRIDED_GROUPS)}
        win = [_window(*win_in.get(g, (q, k, v)), g) for g in range(len(DIL_PATTERNS))]
        y_c = _moba(q, k, vt, km)

        bf16 = lambda *ws: [w[i].astype(BF16) for w in ws]
        x = _mix(y_a.reshape(T, -1), [o for o, _ in win], [l for _, l in win], y_c.reshape(T, -1),
                 gates.reshape(T, -1), x.reshape(T, D), p[i].reshape(T, PLE_DIM),
                 bf16(w_br_a, w_br_b, w_br_c, w_out), row(g_mix_post[i]),
                 row(g_ffn_pre[i]), bf16(w_ffn_gate, w_ffn_up, w_ffn_down), row(g_ffn_post[i]),
                 bf16(w_ple, w_ple_gate), row(g_ple_post[i])).reshape(B, S, D)
    return x
```

```python
import functools
import math

import numpy as np
import jax
import jax.numpy as jnp
from jax import lax
from jax.experimental import pallas as pl
from jax.experimental.pallas import tpu as pltpu

D_MODEL = 1024
HEAD_DIM = 64
N_HEADS_FOX = 4
DIL_PATTERNS = ((128, 1), (512, 4), (2048, 16))
N_HEADS_MOBA = 6
MIX_WIDTH = 1024
MOBA_BLOCK = 256
MOBA_TOPK = 3
PLE_DIM = 256
D_FF = 2816
ROPE_THETA = 10000.0
RMS_EPS = 1e-6
NEG_INF = -1e30
LOG2E = math.log2(math.e)
Q_SCALE = HEAD_DIM ** -0.5 * LOG2E

LANES = 128
SUBLANES = 8
HEADS_PER_LANE_BLOCK = LANES // HEAD_DIM
N_LANE_BLOCKS = MIX_WIDTH // LANES
FOX_COLS = N_HEADS_FOX * HEAD_DIM
DIL_COLS = len(DIL_PATTERNS) * 2 * HEAD_DIM
MOBA_COLS = N_HEADS_MOBA * HEAD_DIM
N_GATE_COLS = 3 * D_MODEL
FORGET_PIECES = 3

TM = 512
TQ = 512
CHUNK = 512
VMEM_LIMIT = 56 * 1024 * 1024

F32 = jnp.float32
BF16 = jnp.bfloat16


def _dot(a, b):
    return jnp.dot(a, b, preferred_element_type=F32)


def _dot_nt(a, b):
    return lax.dot_general(a, b, (((1,), (1,)), ((), ())), preferred_element_type=F32)


def _dot_tn(a, b):
    return lax.dot_general(a, b, (((0,), (0,)), ((), ())), preferred_element_type=F32)


def _rms(x, g):
    var = jnp.mean(x * x, axis=-1, keepdims=True)
    return x * lax.rsqrt(var + RMS_EPS) * g


def _split3(x):
    hi = x.astype(BF16).astype(F32)
    r1 = x - hi
    mid = r1.astype(BF16).astype(F32)
    lo = (r1 - mid).astype(BF16).astype(F32)
    return hi, mid, lo


def _resident(shape):
    nd = len(shape)
    return pl.BlockSpec(shape, lambda *_: (0,) * nd, pipeline_mode=pl.Buffered(1))


def _params(*sem):
    return pltpu.CompilerParams(dimension_semantics=sem, vmem_limit_bytes=VMEM_LIMIT)


STRIDED_GROUPS = tuple((g, dil) for g, (_, dil) in enumerate(DIL_PATTERNS) if dil > 1)


PROJ_SLAB = MOBA_BLOCK


def _proj_slab(n, carry, x_ref, g_ref, wq_ref, wk_ref, wv_ref, wf_ref, bf_ref, wg_ref, cos_ref, sin_ref,
               q_ref, k_ref, v_ref, vt_ref, kx_ref, gates_ref, kmean_ref, res_refs, perm_sc):
    rows = pl.ds(n * PROJ_SLAB, PROJ_SLAB)

    def scatter_residues(t, which, blk):
        for m, (g, dil) in enumerate(STRIDED_GROUPS):
            if blk == FOX_COLS // LANES + g:
                slot = 3 * m + which
                per = PROJ_SLAB // dil
                perm_sc[slot, rows, :] = t
                for r in range(dil):
                    res_refs[slot][0, n * per:(n + 1) * per, r * LANES:(r + 1) * LANES] = perm_sc[
                        slot, pl.ds(n * PROJ_SLAB + r, per, stride=dil), :].astype(BF16)

    hb = _rms(x_ref[0, rows, :], g_ref[...]).astype(BF16)
    cos = cos_ref[rows, :]
    sin = sin_ref[rows, :]
    lane = lax.broadcasted_iota(jnp.int32, (PROJ_SLAB, LANES), 1)
    first_half = (lane % HEAD_DIM) < (HEAD_DIM // 2)

    def rope(t):
        rot = jnp.where(first_half, pltpu.roll(t, LANES - HEAD_DIM // 2, 1),
                        pltpu.roll(t, HEAD_DIM // 2, 1))
        return t * cos + rot * sin

    first_rope_blk = FOX_COLS // LANES
    first_moba_blk = (FOX_COLS + DIL_COLS) // LANES

    f = _dot(hb, wf_ref[...]) + bf_ref[...]
    ls = jnp.minimum(f, 0.0) - jnp.log1p(jnp.exp(-jnp.abs(f)))
    ls = jnp.where(lane < N_HEADS_FOX, ls, 0.0)
    row = lax.broadcasted_iota(jnp.int32, (PROJ_SLAB, PROJ_SLAB), 0)
    col = lax.broadcasted_iota(jnp.int32, (PROJ_SLAB, PROJ_SLAB), 1)
    tri = jnp.where(col <= row, 1.0, 0.0).astype(BF16)
    ls_hi, ls_mid, ls_lo = _split3(ls)

    q = _dot(hb, wq_ref[...])
    yield
    cs = _dot(tri, ls_hi.astype(BF16)) + _dot(tri, ls_mid.astype(BF16)) + _dot(tri, ls_lo.astype(BF16))
    for c in range(N_LANE_BLOCKS):
        t = q[:, c * LANES:(c + 1) * LANES]
        if c >= first_rope_blk:
            t = rope(t)
        t = t * Q_SCALE
        q_ref[0, rows, c * LANES:(c + 1) * LANES] = t.astype(BF16)
        scatter_residues(t, 0, c)

    k = _dot(hb, wk_ref[...])
    yield
    cs = cs + carry[n]
    carry.append(cs[PROJ_SLAB - 1:PROJ_SLAB, :])
    hi, mid, lo = _split3(-LOG2E * cs)
    pieces = jnp.where(lane < N_HEADS_FOX, hi,
                       jnp.where(lane < 2 * N_HEADS_FOX, pltpu.roll(mid, N_HEADS_FOX, 1),
                                 pltpu.roll(lo, 2 * N_HEADS_FOX, 1)))
    kx_ref[0, rows, :] = pieces.astype(BF16)

    for c in range(N_LANE_BLOCKS):
        t = k[:, c * LANES:(c + 1) * LANES]
        if c >= first_rope_blk:
            t = rope(t)
        k_ref[0, rows, c * LANES:(c + 1) * LANES] = t.astype(BF16)
        scatter_residues(t, 1, c)
        if c >= first_moba_blk:
            cm = c - first_moba_blk
            kmean_ref[0, 0, n:n + 1, cm * LANES:(cm + 1) * LANES] = jnp.mean(t, axis=0, keepdims=True)

    v = _dot(hb, wv_ref[...])
    yield
    v_ref[0, rows, :] = v.astype(BF16)
    for c in range(N_LANE_BLOCKS):
        scatter_residues(v[:, c * LANES:(c + 1) * LANES], 2, c)
    vt_ref[0, :, rows] = v.T.astype(BF16)

    for c in range(N_GATE_COLS // D_MODEL):
        z = _dot(hb, wg_ref[:, c * D_MODEL:(c + 1) * D_MODEL])
        yield
        gates_ref[0, rows, c * D_MODEL:(c + 1) * D_MODEL] = jax.nn.sigmoid(z).astype(BF16)


def _proj_kernel(*refs):
    *io_refs, carry_ref, perm_sc = refs
    n_res = 3 * len(STRIDED_GROUPS)
    main_refs, res_refs = io_refs[:len(io_refs) - n_res], io_refs[len(io_refs) - n_res:]

    @pl.when(pl.program_id(1) == 0)
    def _():
        carry_ref[...] = jnp.zeros_like(carry_ref)

    carry = [carry_ref[...]]
    chains = [_proj_slab(n, carry, *main_refs, res_refs, perm_sc) for n in range(TM // PROJ_SLAB)]
    while chains:
        chains = [c for c in chains if next(c, StopIteration) is not StopIteration]
    carry_ref[...] = carry[-1]


def _proj(x, g, wq, wk, wv, wf, bf, wg, cos_t, sin_t):
    B, S, D = x.shape
    n_t = S // TM
    tok = lambda w: pl.BlockSpec((1, TM, w), lambda b, i: (b, i, 0))
    res_specs, res_shapes = [], []
    for _, dil in STRIDED_GROUPS:
        assert TM % (dil * 2 * SUBLANES) == 0 and S % dil == 0
        res_specs += [pl.BlockSpec((1, TM // dil, dil * LANES), lambda b, i: (b, i, 0))] * 3
        res_shapes += [jax.ShapeDtypeStruct((B, S // dil, dil * LANES), BF16)] * 3
    outs = pl.pallas_call(
        _proj_kernel,
        grid=(B, n_t),
        in_specs=[tok(D), _resident((1, D)), _resident((D, MIX_WIDTH)), _resident((D, MIX_WIDTH)),
                  _resident((D, MIX_WIDTH)),
                  _resident((D, LANES)), _resident((1, LANES)), _resident((D, N_GATE_COLS)),
                  pl.BlockSpec((TM, LANES), lambda b, i: (i, 0)),
                  pl.BlockSpec((TM, LANES), lambda b, i: (i, 0))],
        out_specs=[tok(MIX_WIDTH), tok(MIX_WIDTH), tok(MIX_WIDTH),
                   pl.BlockSpec((1, MIX_WIDTH, TM), lambda b, i: (b, 0, i)),
                   tok(LANES), tok(N_GATE_COLS),
                   pl.BlockSpec((1, 1, TM // MOBA_BLOCK, MOBA_COLS), lambda b, i: (b, i, 0, 0))] + res_specs,
        out_shape=[jax.ShapeDtypeStruct((B, S, MIX_WIDTH), BF16)] * 3
        + [jax.ShapeDtypeStruct((B, MIX_WIDTH, S), BF16),
           jax.ShapeDtypeStruct((B, S, LANES), BF16),
           jax.ShapeDtypeStruct((B, S, N_GATE_COLS), BF16),
           jax.ShapeDtypeStruct((B, n_t, TM // MOBA_BLOCK, MOBA_COLS), F32)] + res_shapes,
        scratch_shapes=[pltpu.VMEM((1, LANES), F32), pltpu.VMEM((len(res_specs), TM, LANES), F32)],
        compiler_params=_params("arbitrary", "arbitrary"),
        name="proj",
    )(x, g, wq, wk, wv, wf, bf, wg, cos_t, sin_t)
    return outs[:7], outs[7:]


def _reduce_keys(x, op, reduce):
    n = x.shape[0]
    while n > SUBLANES and n % (2 * SUBLANES) == 0:
        n //= 2
        x = op(x[:n], x[n:])
    return reduce(x, axis=0, keepdims=True)


def _key_max(s):
    return _reduce_keys(s, jnp.maximum, jnp.max)


def _probabilities(s, m):
    return jnp.exp2((s - m).astype(BF16))


ACC_ROWS = HEAD_DIM + 2 * SUBLANES


def _values_with_ones(vt):
    return jnp.concatenate([vt, jnp.ones((ACC_ROWS - HEAD_DIM, vt.shape[1]), BF16)], axis=0)


def _chunked_attention(i, n_heads, scores, values, s_sc, p_sc):
    heads = range(n_heads)
    n_full = (i * TQ) // CHUNK

    causal = _chunk_causal(n_full * CHUNK, i * TQ)
    s_diag = [score() for score in scores(n_full)]
    s_first = [score() for score in scores(0)]
    m, alpha = [], []
    for hl in heads:
        sd = jnp.where(causal, s_diag[hl], NEG_INF)
        m_diag = _key_max(sd)
        p_sc[hl] = _probabilities(sd, m_diag)
        s_sc[hl] = s_first[hl]
        m.append(jnp.where(n_full > 0, jnp.maximum(m_diag, _key_max(s_first[hl])), m_diag))
        alpha.append(jnp.exp2(m_diag - m[hl]))

    def product(c, p):
        return [_dot(_values_with_ones(values(c, hl)), p[hl]) for hl in heads]

    def before(c):
        return jnp.where(c <= 0, n_full, c - 1)

    def body(c, carry):
        m, alpha, acc = carry
        pv = product(before(c), [p_sc[hl] for hl in heads])
        s_next = [score() for score in scores(c + 1)]
        m_next = tuple(jnp.maximum(m[hl], _key_max(s_next[hl])) for hl in heads)
        alpha_next = tuple(jnp.exp2(m[hl] - m_next[hl]) for hl in heads)
        for hl in heads:
            p_sc[hl] = _probabilities(s_sc[hl], m[hl])
            s_sc[hl] = s_next[hl]
        acc = tuple(alpha[hl] * (acc[hl] + pv[hl]) for hl in heads)
        return m_next, alpha_next, acc

    init = (tuple(m), tuple(alpha), tuple(jnp.zeros((ACC_ROWS, TQ), F32) for _ in heads))
    m, alpha, acc = lax.fori_loop(0, n_full - 1, body, init)

    last = n_full - 1
    pv = product(before(last), [p_sc[hl] for hl in heads])
    m_tail = [jnp.where(n_full > 0, m[hl], -NEG_INF) for hl in heads]
    tail = product(jnp.maximum(last, 0), [_probabilities(s_sc[hl], m_tail[hl]) for hl in heads])
    return [alpha[hl] * (acc[hl] + pv[hl]) + tail[hl] for hl in heads]


def _query_t(q):
    return q.astype(F32).T


def _head_rows(q_t, hl):
    sub = lax.broadcasted_iota(jnp.int32, q_t.shape, 0)
    return jnp.where((sub >= hl * HEAD_DIM) & (sub < (hl + 1) * HEAD_DIM), q_t, 0.0)


def _finish(accs):
    blocks = []
    for p in range(len(accs) // HEADS_PER_LANE_BLOCK):
        pair = accs[p * HEADS_PER_LANE_BLOCK:(p + 1) * HEADS_PER_LANE_BLOCK]
        out_t = jnp.concatenate([acc[:HEAD_DIM] / acc[HEAD_DIM:HEAD_DIM + 1] for acc in pair], axis=0)
        blocks.append(out_t.T)
    return jnp.concatenate(blocks, axis=1)


def _chunk_scratch(n_heads):
    return [pltpu.VMEM((n_heads, CHUNK, TQ), F32), pltpu.VMEM((n_heads, CHUNK, TQ), BF16)]


def _chunk_causal(first_key, first_query):
    kpos = first_key + lax.broadcasted_iota(jnp.int32, (CHUNK, TQ), 0)
    qpos = first_query + lax.broadcasted_iota(jnp.int32, (CHUNK, TQ), 1)
    return kpos <= qpos


def _fox_kernel(q_ref, k_ref, kx_ref, vt_ref, o_ref, s_sc, p_sc):
    i = pl.program_id(1)
    n_pairs = FOX_COLS // LANES
    sub = lax.broadcasted_iota(jnp.int32, (LANES, TQ), 0)
    q_aug = []
    for p in range(n_pairs):
        q_t = _query_t(q_ref[0, :, p * LANES:(p + 1) * LANES])
        for hl in range(HEADS_PER_LANE_BLOCK):
            head = HEADS_PER_LANE_BLOCK * p + hl
            ones = jnp.where((sub % N_HEADS_FOX == head) & (sub < FORGET_PIECES * N_HEADS_FOX), 1.0, 0.0)
            q_aug.append(jnp.concatenate([_head_rows(q_t, hl), ones], axis=0).astype(BF16))

    def scores(c):
        off = pl.multiple_of(c * CHUNK, CHUNK)
        kx = kx_ref[0, pl.ds(off, CHUNK), :]
        out = []
        for p in range(n_pairs):
            k_aug = jnp.concatenate([k_ref[0, pl.ds(off, CHUNK), p * LANES:(p + 1) * LANES], kx], axis=1)
            out += [functools.partial(_dot, k_aug, qa)
                    for qa in q_aug[p * HEADS_PER_LANE_BLOCK:(p + 1) * HEADS_PER_LANE_BLOCK]]
        return out

    def values(c, h):
        off = pl.multiple_of(c * CHUNK, CHUNK)
        return vt_ref[0, h * HEAD_DIM:(h + 1) * HEAD_DIM, pl.ds(off, CHUNK)]

    o_ref[0] = _finish(_chunked_attention(i, N_HEADS_FOX, scores, values, s_sc, p_sc)).astype(BF16)


def _fox(q, k, kx, vt):
    B, S, _ = q.shape
    assert S % CHUNK == 0
    return pl.pallas_call(
        _fox_kernel,
        grid=(B, S // TQ),
        in_specs=[pl.BlockSpec((1, TQ, FOX_COLS), lambda b, i: (b, i, 0)),
                  pl.BlockSpec((1, S, FOX_COLS), lambda b, i: (b, 0, 0)),
                  pl.BlockSpec((1, S, LANES), lambda b, i: (b, 0, 0)),
                  pl.BlockSpec((1, FOX_COLS, S), lambda b, i: (b, 0, 0))],
        out_specs=pl.BlockSpec((1, TQ, FOX_COLS), lambda b, i: (b, i, 0)),
        out_shape=jax.ShapeDtypeStruct((B, S, FOX_COLS), BF16),
        scratch_shapes=_chunk_scratch(N_HEADS_FOX),
        compiler_params=_params("parallel", "arbitrary"),
        name="fox",
    )(q, k, kx, vt)


WINDOW_KEYS = DIL_PATTERNS[0][0] // DIL_PATTERNS[0][1]
WQ = 256
assert all(w // d == WINDOW_KEYS for w, d in DIL_PATTERNS) and WINDOW_KEYS <= WQ


WINDOW_CHAINS = 4


def _window_kernel(q_ref, kc_ref, kp_ref, vc_ref, vp_ref, o_ref, lse_ref, *, dil, n_res, n_sub):
    a = pl.program_id(1)
    rg = pl.program_id(2)
    kr = lax.broadcasted_iota(jnp.int32, (WQ, WQ), 0)
    qc = lax.broadcasted_iota(jnp.int32, (WQ, WQ), 1)
    valid_d = (kr <= qc) & (qc - kr <= WINDOW_KEYS)
    krp = lax.broadcasted_iota(jnp.int32, (WINDOW_KEYS, WQ), 0)
    qcp = lax.broadcasted_iota(jnp.int32, (WINDOW_KEYS, WQ), 1)
    valid_p = krp >= qcp
    valid_first = valid_p & (a > 0)

    chains = [(rr, t) for rr in range(n_res) for t in range(n_sub)]
    heads = range(HEADS_PER_LANE_BLOCK)
    raw, vals = [], []
    for rr, t in chains:
        cols = slice(rr * LANES, (rr + 1) * LANES)
        sub = slice(t * WQ, (t + 1) * WQ)
        q_t = _query_t(q_ref[0, sub, cols])
        if t == 0:
            kp, vp = kp_ref[0, :, cols], vp_ref[0, :, cols]
        else:
            before = slice(t * WQ - WINDOW_KEYS, t * WQ)
            kp, vp = kc_ref[0, before, cols], vc_ref[0, before, cols]
        kd = kc_ref[0, sub, cols]
        vals.append((vc_ref[0, sub, cols], vp))
        q_m = [_head_rows(q_t, hl).astype(BF16) for hl in heads]
        raw.append([(_dot(kd, qm), _dot(kp, qm)) for qm in q_m])

    probs, stats = [], []
    for n, (rr, t) in enumerate(chains):
        for hl in heads:
            s_d = jnp.where(valid_d, raw[n][hl][0], NEG_INF)
            s_p = jnp.where(valid_first if t == 0 else valid_p, raw[n][hl][1], NEG_INF)
            m = jnp.maximum(_key_max(s_d), _key_max(s_p))
            p_d = jnp.exp2(s_d - m)
            p_p = jnp.exp2(s_p - m)
            l = _reduce_keys(p_d, jnp.add, jnp.sum) + _reduce_keys(p_p, jnp.add, jnp.sum)
            probs.append((p_d.astype(BF16), p_p.astype(BF16)))
            stats.append((m, l))

    accs = [_dot_tn(vals[n][0], probs[n * len(heads) + hl][0]) + _dot_tn(vals[n][1], probs[n * len(heads) + hl][1])
            for n in range(len(chains)) for hl in heads]

    for n, (rr, t) in enumerate(chains):
        outs, lses = [], []
        for hl in heads:
            m, l = stats[n * len(heads) + hl]
            outs.append(accs[n * len(heads) + hl][hl * HEAD_DIM:(hl + 1) * HEAD_DIM] / l)
            lses.append(jnp.broadcast_to(m + jnp.log2(l), (HEAD_DIM, WQ)))
        r = rg * n_res + rr
        rows = pl.ds(t * WQ * dil + r, WQ, stride=dil) if dil > 1 else pl.ds(t * WQ, WQ)
        o_ref[0, rows, :] = jnp.concatenate(outs, axis=0).T
        lse_ref[0, rows, :] = jnp.concatenate(lses, axis=0).T


def _window(q, k, v, group):
    dil = DIL_PATTERNS[group][1]
    B, rows, width = q.shape
    S = rows * dil
    n_res = min(dil, 2)
    n_sub = WINDOW_CHAINS // n_res
    tile = n_sub * WQ
    assert rows % tile == 0 and tile % WINDOW_KEYS == 0 and dil % n_res == 0
    col0, col_step = (FOX_COLS // LANES + group, 0) if dil == 1 else (0, 1)
    assert width == (MIX_WIDTH if dil == 1 else dil * LANES)
    cur = pl.BlockSpec((1, tile, n_res * LANES), lambda b, a, rg: (b, a, col0 + col_step * rg))
    per = tile // WINDOW_KEYS
    prev = pl.BlockSpec((1, WINDOW_KEYS, n_res * LANES),
                        lambda b, a, rg: (b, jnp.maximum(a * per - 1, 0), col0 + col_step * rg))
    out = pl.BlockSpec((1, dil * tile, LANES), lambda b, a, rg: (b, a, 0))
    o, lse = pl.pallas_call(
        functools.partial(_window_kernel, dil=dil, n_res=n_res, n_sub=n_sub),
        grid=(B, rows // tile, dil // n_res),
        in_specs=[cur, cur, prev, cur, prev],
        out_specs=[out, out],
        out_shape=[jax.ShapeDtypeStruct((B, S, LANES), F32)] * 2,
        compiler_params=_params("parallel", "arbitrary", "arbitrary"),
        name=f"window{group}",
    )(q, k, k, v, v)
    return o.reshape(B * S, LANES), lse.reshape(B * S, LANES)


MOBA_PAIRS = MOBA_COLS // LANES


def _moba_kernel(*refs):
    q_refs, k_refs, vt_refs = refs[:MOBA_PAIRS], refs[MOBA_PAIRS:2 * MOBA_PAIRS], refs[2 * MOBA_PAIRS:3 * MOBA_PAIRS]
    km_ref, o_ref, s_sc, p_sc = refs[3 * MOBA_PAIRS:]
    i = pl.program_id(1)
    n_blocks = km_ref.shape[1]
    blk = lax.broadcasted_iota(jnp.int32, (n_blocks, TQ), 0)
    own = i * (TQ // MOBA_BLOCK) + lax.broadcasted_iota(jnp.int32, (n_blocks, TQ), 1) // MOBA_BLOCK
    past = blk < own
    never = jnp.full((LANES - n_blocks, TQ), NEG_INF, F32)
    q_aug = []
    for p in range(MOBA_PAIRS):
        q_t = _query_t(q_refs[p][0])
        km = km_ref[0, :, p * LANES:(p + 1) * LANES]
        for hl in range(HEADS_PER_LANE_BLOCK):
            qm = _head_rows(q_t, hl)
            gate = jnp.where(past, _dot(km, qm.astype(BF16)), NEG_INF)
            sel = blk == own
            for _ in range(MOBA_TOPK):
                top = jnp.max(gate, axis=0, keepdims=True)
                idx = jnp.min(jnp.where(gate == top, blk, n_blocks), axis=0, keepdims=True)
                hit = blk == idx
                sel = sel | (hit & past)
                gate = jnp.where(hit, -jnp.inf, gate)
            sel_bias = jnp.where(sel, 0.0, NEG_INF)
            q_aug.append(jnp.concatenate([qm, sel_bias, never], axis=0).astype(BF16))

    row = lax.broadcasted_iota(jnp.int32, (CHUNK, LANES), 0)
    lane = lax.broadcasted_iota(jnp.int32, (CHUNK, LANES), 1)
    blocks_per_chunk = CHUNK // MOBA_BLOCK

    def scores(c):
        off = pl.multiple_of(c * CHUNK, CHUNK)
        onehot = jnp.where(lane == c * blocks_per_chunk + row // MOBA_BLOCK, 1.0, 0.0).astype(BF16)
        out = []
        for p in range(MOBA_PAIRS):
            k_aug = jnp.concatenate([k_refs[p][0, pl.ds(off, CHUNK), :], onehot], axis=1)
            out += [functools.partial(_dot, k_aug, qa)
                    for qa in q_aug[p * HEADS_PER_LANE_BLOCK:(p + 1) * HEADS_PER_LANE_BLOCK]]
        return out

    def values(c, h):
        off = pl.multiple_of(c * CHUNK, CHUNK)
        p, hl = divmod(h, HEADS_PER_LANE_BLOCK)
        return vt_refs[p][0, hl * HEAD_DIM:(hl + 1) * HEAD_DIM, pl.ds(off, CHUNK)]

    o_ref[0] = _finish(_chunked_attention(i, N_HEADS_MOBA, scores, values, s_sc, p_sc)).astype(BF16)


def _moba(q, k, vt, km):
    B, S, _ = q.shape
    n_blocks = S // MOBA_BLOCK
    assert TQ % MOBA_BLOCK == 0 and CHUNK % MOBA_BLOCK == 0 and S % CHUNK == 0
    assert n_blocks <= LANES and n_blocks % (2 * SUBLANES) == 0
    first = (FOX_COLS + DIL_COLS) // LANES
    at = lambda f: [pl.BlockSpec(*f(first + p)) for p in range(MOBA_PAIRS)]
    return pl.pallas_call(
        _moba_kernel,
        grid=(B, S // TQ),
        in_specs=at(lambda c: ((1, TQ, LANES), lambda b, i: (b, i, c)))
        + at(lambda c: ((1, S, LANES), lambda b, i: (b, 0, c)))
        + at(lambda c: ((1, LANES, S), lambda b, i: (b, c, 0)))
        + [pl.BlockSpec((1, n_blocks, MOBA_COLS), lambda b, i: (b, 0, 0))],
        out_specs=pl.BlockSpec((1, TQ, MOBA_COLS), lambda b, i: (b, i, 0)),
        out_shape=jax.ShapeDtypeStruct((B, S, MOBA_COLS), BF16),
        scratch_shapes=_chunk_scratch(N_HEADS_MOBA),
        compiler_params=_params("parallel", "arbitrary"),
        name="moba",
    )(*([q] * MOBA_PAIRS + [k] * MOBA_PAIRS + [vt] * MOBA_PAIRS + [km]))


MXU_TILE = 256
assert D_FF % MXU_TILE == 0
FF_SPLITS = (0, (D_FF // MXU_TILE + 1) // 2 * MXU_TILE, D_FF)


MIX_SPLIT = 2


def _mix_slab(rows, ya_ref, o_refs, l_refs, yc_ref, gates_ref, x_ref, p_ref,
              wa_ref, wb_ref, wc_ref, wo_ref, g_mix_ref,
              g_pre_ref, wgate_ref, wup_ref, wdown_ref, g_post_ref, wple_ref, wpg_ref, g_ple_ref, o_ref):
    lses = [l_ref[rows, :] for l_ref in l_refs]
    top = jnp.maximum(jnp.maximum(lses[0], lses[1]), lses[2])
    wts = [jnp.exp2(t - top) for t in lses]
    den = wts[0] + wts[1] + wts[2]
    yb = (wts[0] * o_refs[0][rows, :] + wts[1] * o_refs[1][rows, :] + wts[2] * o_refs[2][rows, :]) / den
    merged = None
    for c, (y, w_ref) in enumerate(((ya_ref[rows, :], wa_ref), (yb.astype(BF16), wb_ref), (yc_ref[rows, :], wc_ref))):
        branch = _dot(y, w_ref[...])
        yield
        term = gates_ref[rows, c * D_MODEL:(c + 1) * D_MODEL].astype(F32) * branch
        merged = term if merged is None else merged + term
    out = _dot(merged.astype(BF16), wo_ref[...])
    yield
    x = x_ref[rows, :] + _rms(out, g_mix_ref[...])

    hb = _rms(x, g_pre_ref[...]).astype(BF16)
    down = None
    for lo, hi in zip(FF_SPLITS[:-1], FF_SPLITS[1:]):
        cols = slice(lo, hi)
        gate = _dot(hb, wgate_ref[:, cols])
        yield
        up = _dot(hb, wup_ref[:, cols])
        yield
        ff = (gate * jax.nn.sigmoid(gate) * up).astype(BF16)
        part = _dot(ff, wdown_ref[cols, :])
        yield
        down = part if down is None else down + part
    x = x + _rms(down, g_post_ref[...])

    emb = _dot(p_ref[rows, :].astype(BF16), wple_ref[...])
    yield
    ple = emb * jax.nn.sigmoid(_dot(x.astype(BF16), wpg_ref[...]))
    yield
    o_ref[rows, :] = x + _rms(ple, g_ple_ref[...])


def _mix_kernel(ya_ref, o0_ref, o1_ref, o2_ref, l0_ref, l1_ref, l2_ref, *rest):
    slab = TM // MIX_SPLIT
    chains = [_mix_slab(pl.ds(n * slab, slab), ya_ref, (o0_ref, o1_ref, o2_ref), (l0_ref, l1_ref, l2_ref), *rest)
              for n in range(MIX_SPLIT)]
    while chains:
        chains = [c for c in chains if next(c, StopIteration) is not StopIteration]


def _mix(ya, ob, lb, yc, gates, x, p, mix_weights, g_mix, g_pre, ffn_weights, g_post, ple_weights, g_ple):
    T, D = x.shape
    tok = lambda w: pl.BlockSpec((TM, w), lambda i: (i, 0))
    vec = _resident((1, D))
    whole = lambda ws: [_resident(w.shape) for w in ws]
    return pl.pallas_call(
        _mix_kernel,
        grid=(T // TM,),
        in_specs=[tok(FOX_COLS)] + [tok(LANES)] * 6 + [tok(MOBA_COLS), tok(N_GATE_COLS), tok(D), tok(PLE_DIM)]
        + whole(mix_weights) + [vec, vec] + whole(ffn_weights) + [vec] + whole(ple_weights) + [vec],
        out_specs=tok(D),
        out_shape=jax.ShapeDtypeStruct((T, D), F32),
        compiler_params=_params("parallel"),
        name="mix",
    )(ya, *ob, *lb, yc, gates, x, p, *mix_weights, g_mix, g_pre, *ffn_weights, g_post, *ple_weights, g_ple)


def _rope_tables(seq):
    inv = 1.0 / (ROPE_THETA ** (jnp.arange(0, HEAD_DIM, 2, dtype=F32) / HEAD_DIM))
    ang = jnp.arange(seq, dtype=F32)[:, None] * inv[None, :]
    cos, sin = jnp.cos(ang), jnp.sin(ang)
    reps = LANES // HEAD_DIM
    cos_t = jnp.tile(jnp.concatenate([cos, cos], axis=1), (1, reps))
    sin_t = jnp.tile(jnp.concatenate([-sin, sin], axis=1), (1, reps))
    return cos_t, sin_t


def kernel(x, p, g_mix_pre, w_in, b_f, w_br_a, w_br_b, w_br_c, w_out, g_mix_post, g_ffn_pre,
           w_ffn_gate, w_ffn_up, w_ffn_down, g_ffn_post, w_ple, w_ple_gate, g_ple_post):
    B, S, D = x.shape
    T = B * S
    depth = w_in.shape[0]
    cos_t, sin_t = _rope_tables(S)
    row = lambda g: g.reshape(1, -1)
    f0 = 3 * MIX_WIDTH
    for i in range(depth):
        wq = w_in[i, :, :MIX_WIDTH].astype(BF16)
        wk = w_in[i, :, MIX_WIDTH:2 * MIX_WIDTH].astype(BF16)
        wv = w_in[i, :, 2 * MIX_WIDTH:f0].astype(BF16)
        wf = jnp.pad(w_in[i, :, f0:f0 + N_HEADS_FOX], ((0, 0), (0, LANES - N_HEADS_FOX))).astype(BF16)
        bf = jnp.pad(b_f[i], (0, LANES - N_HEADS_FOX)).reshape(1, LANES)
        wg = w_in[i, :, f0 + N_HEADS_FOX:].astype(BF16)

        (q, k, v, vt, kx, gates, kmean), residue_views = _proj(
            x, row(g_mix_pre[i]), wq, wk, wv, wf, bf, wg, cos_t, sin_t)
        km = kmean.reshape(B, S // MOBA_BLOCK, MOBA_COLS).astype(BF16)
        y_a = _fox(q, k, kx, vt)
        win_in = {g: residue_views[3 * n:3 * n + 3] for n, (g, _) in enumerate(STRIDED_GROUPS)}
        win = [_window(*win_in.get(g, (q, k, v)), g) for g in range(len(DIL_PATTERNS))]
        y_c = _moba(q, k, vt, km)

        bf16 = lambda *ws: [w[i].astype(BF16) for w in ws]
        x = _mix(y_a.reshape(T, -1), [o for o, _ in win], [l for _, l in win], y_c.reshape(T, -1),
                 gates.reshape(T, -1), x.reshape(T, D), p[i].reshape(T, PLE_DIM),
                 bf16(w_br_a, w_br_b, w_br_c, w_out), row(g_mix_post[i]),
                 row(g_ffn_pre[i]), bf16(w_ffn_gate, w_ffn_up, w_ffn_down), row(g_ffn_post[i]),
                 bf16(w_ple, w_ple_gate), row(g_ple_post[i])).reshape(B, S, D)
    return x
```

```python
import functools
import math

import numpy as np
import jax
import jax.numpy as jnp
from jax import lax
from jax.experimental import pallas as pl
from jax.experimental.pallas import tpu as pltpu

D_MODEL = 1024
HEAD_DIM = 64
N_HEADS_FOX = 4
DIL_PATTERNS = ((128, 1), (512, 4), (2048, 16))
N_HEADS_MOBA = 6
MIX_WIDTH = 1024
MOBA_BLOCK = 256
MOBA_TOPK = 3
PLE_DIM = 256
D_FF = 2816
ROPE_THETA = 10000.0
RMS_EPS = 1e-6
NEG_INF = -1e30
LOG2E = math.log2(math.e)
Q_SCALE = HEAD_DIM ** -0.5 * LOG2E

LANES = 128
SUBLANES = 8
HEADS_PER_LANE_BLOCK = LANES // HEAD_DIM
N_LANE_BLOCKS = MIX_WIDTH // LANES
FOX_COLS = N_HEADS_FOX * HEAD_DIM
DIL_COLS = len(DIL_PATTERNS) * 2 * HEAD_DIM
MOBA_COLS = N_HEADS_MOBA * HEAD_DIM
N_GATE_COLS = 3 * D_MODEL
FORGET_PIECES = 3

TM = 512
TQ = 512
CHUNK = 512
VMEM_LIMIT = 56 * 1024 * 1024

F32 = jnp.float32
BF16 = jnp.bfloat16


def _dot(a, b):
    return jnp.dot(a, b, preferred_element_type=F32)


def _dot_nt(a, b):
    return lax.dot_general(a, b, (((1,), (1,)), ((), ())), preferred_element_type=F32)


def _dot_tn(a, b):
    return lax.dot_general(a, b, (((0,), (0,)), ((), ())), preferred_element_type=F32)


def _rms(x, g):
    var = jnp.mean(x * x, axis=-1, keepdims=True)
    return x * lax.rsqrt(var + RMS_EPS) * g


def _split3(x):
    hi = x.astype(BF16).astype(F32)
    r1 = x - hi
    mid = r1.astype(BF16).astype(F32)
    lo = (r1 - mid).astype(BF16).astype(F32)
    return hi, mid, lo


def _resident(shape):
    nd = len(shape)
    return pl.BlockSpec(shape, lambda *_: (0,) * nd, pipeline_mode=pl.Buffered(1))


def _params(*sem):
    return pltpu.CompilerParams(dimension_semantics=sem, vmem_limit_bytes=VMEM_LIMIT)


SPLIT_ROWS = 128


def _split_kernel(w_ref, wq_ref, wk_ref, wv_ref, wf_ref, wg_ref):
    f0 = 3 * MIX_WIDTH
    for n, ref in enumerate((wq_ref, wk_ref, wv_ref)):
        ref[...] = w_ref[0, :, n * MIX_WIDTH:(n + 1) * MIX_WIDTH].astype(BF16)
    lane = lax.broadcasted_iota(jnp.int32, (SPLIT_ROWS, LANES), 1)
    wf_ref[...] = jnp.where(lane < N_HEADS_FOX, w_ref[0, :, f0:f0 + LANES], 0.0).astype(BF16)
    wg_ref[...] = w_ref[0, :, f0 + N_HEADS_FOX:].astype(BF16)


def _split_w_in(w_in, layer):
    _, rows, cols = w_in.shape
    assert cols == 3 * MIX_WIDTH + N_HEADS_FOX + N_GATE_COLS and rows % SPLIT_ROWS == 0
    blk = lambda n: pl.BlockSpec((SPLIT_ROWS, n), lambda i: (i, 0))
    widths = (MIX_WIDTH, MIX_WIDTH, MIX_WIDTH, LANES, N_GATE_COLS)
    return pl.pallas_call(
        _split_kernel,
        grid=(rows // SPLIT_ROWS,),
        in_specs=[pl.BlockSpec((1, SPLIT_ROWS, cols), lambda i: (layer, i, 0))],
        out_specs=[blk(n) for n in widths],
        out_shape=[jax.ShapeDtypeStruct((rows, n), BF16) for n in widths],
        compiler_params=_params("parallel"),
        name="split",
    )(w_in)


STRIDED_GROUPS = tuple((g, dil) for g, (_, dil) in enumerate(DIL_PATTERNS) if dil > 1)


PROJ_SLAB = MOBA_BLOCK


def _proj_slab(n, carry, x_ref, g_ref, wq_ref, wk_ref, wv_ref, wf_ref, bf_ref, wg_ref, cos_ref, sin_ref,
               q_ref, k_ref, v_ref, vt_ref, kx_ref, gates_ref, kmean_ref, res_refs, perm_sc):
    rows = pl.ds(n * PROJ_SLAB, PROJ_SLAB)

    def scatter_residues(t, which, blk):
        for m, (g, dil) in enumerate(STRIDED_GROUPS):
            if blk == FOX_COLS // LANES + g:
                slot = 3 * m + which
                per = PROJ_SLAB // dil
                perm_sc[slot, rows, :] = t
                for r in range(dil):
                    res_refs[slot][0, n * per:(n + 1) * per, r * LANES:(r + 1) * LANES] = perm_sc[
                        slot, pl.ds(n * PROJ_SLAB + r, per, stride=dil), :].astype(BF16)

    hb = _rms(x_ref[0, rows, :], g_ref[...]).astype(BF16)
    cos = cos_ref[rows, :]
    sin = sin_ref[rows, :]
    lane = lax.broadcasted_iota(jnp.int32, (PROJ_SLAB, LANES), 1)
    first_half = (lane % HEAD_DIM) < (HEAD_DIM // 2)

    def rope(t):
        rot = jnp.where(first_half, pltpu.roll(t, LANES - HEAD_DIM // 2, 1),
                        pltpu.roll(t, HEAD_DIM // 2, 1))
        return t * cos + rot * sin

    first_rope_blk = FOX_COLS // LANES
    first_moba_blk = (FOX_COLS + DIL_COLS) // LANES

    f = _dot(hb, wf_ref[...]) + bf_ref[...]
    ls = jnp.minimum(f, 0.0) - jnp.log1p(jnp.exp(-jnp.abs(f)))
    ls = jnp.where(lane < N_HEADS_FOX, ls, 0.0)
    row = lax.broadcasted_iota(jnp.int32, (PROJ_SLAB, PROJ_SLAB), 0)
    col = lax.broadcasted_iota(jnp.int32, (PROJ_SLAB, PROJ_SLAB), 1)
    tri = jnp.where(col <= row, 1.0, 0.0).astype(BF16)
    ls_hi, ls_mid, ls_lo = _split3(ls)

    q = _dot(hb, wq_ref[...])
    yield
    cs = _dot(tri, ls_hi.astype(BF16)) + _dot(tri, ls_mid.astype(BF16)) + _dot(tri, ls_lo.astype(BF16))
    for c in range(N_LANE_BLOCKS):
        t = q[:, c * LANES:(c + 1) * LANES]
        if c >= first_rope_blk:
            t = rope(t)
        t = t * Q_SCALE
        q_ref[0, rows, c * LANES:(c + 1) * LANES] = t.astype(BF16)
        scatter_residues(t, 0, c)

    k = _dot(hb, wk_ref[...])
    yield
    cs = cs + carry[n]
    carry.append(cs[PROJ_SLAB - 1:PROJ_SLAB, :])
    hi, mid, lo = _split3(-LOG2E * cs)
    pieces = jnp.where(lane < N_HEADS_FOX, hi,
                       jnp.where(lane < 2 * N_HEADS_FOX, pltpu.roll(mid, N_HEADS_FOX, 1),
                                 pltpu.roll(lo, 2 * N_HEADS_FOX, 1)))
    kx_ref[0, rows, :] = pieces.astype(BF16)

    for c in range(N_LANE_BLOCKS):
        t = k[:, c * LANES:(c + 1) * LANES]
        if c >= first_rope_blk:
            t = rope(t)
        k_ref[0, rows, c * LANES:(c + 1) * LANES] = t.astype(BF16)
        scatter_residues(t, 1, c)
        if c >= first_moba_blk:
            cm = c - first_moba_blk
            kmean_ref[0, 0, n:n + 1, cm * LANES:(cm + 1) * LANES] = jnp.mean(t, axis=0, keepdims=True)

    v = _dot(hb, wv_ref[...])
    yield
    v_ref[0, rows, :] = v.astype(BF16)
    for c in range(N_LANE_BLOCKS):
        scatter_residues(v[:, c * LANES:(c + 1) * LANES], 2, c)
    vt_ref[0, :, rows] = v.T.astype(BF16)

    for c in range(N_GATE_COLS // D_MODEL):
        z = _dot(hb, wg_ref[:, c * D_MODEL:(c + 1) * D_MODEL])
        yield
        gates_ref[0, rows, c * D_MODEL:(c + 1) * D_MODEL] = jax.nn.sigmoid(z).astype(BF16)


def _proj_kernel(*refs):
    *io_refs, carry_ref, perm_sc = refs
    n_res = 3 * len(STRIDED_GROUPS)
    main_refs, res_refs = io_refs[:len(io_refs) - n_res], io_refs[len(io_refs) - n_res:]

    @pl.when(pl.program_id(1) == 0)
    def _():
        carry_ref[...] = jnp.zeros_like(carry_ref)

    carry = [carry_ref[...]]
    chains = [_proj_slab(n, carry, *main_refs, res_refs, perm_sc) for n in range(TM // PROJ_SLAB)]
    while chains:
        chains = [c for c in chains if next(c, StopIteration) is not StopIteration]
    carry_ref[...] = carry[-1]


def _proj(x, g, wq, wk, wv, wf, bf, wg, cos_t, sin_t):
    B, S, D = x.shape
    n_t = S // TM
    tok = lambda w: pl.BlockSpec((1, TM, w), lambda b, i: (b, i, 0))
    res_specs, res_shapes = [], []
    for _, dil in STRIDED_GROUPS:
        assert TM % (dil * 2 * SUBLANES) == 0 and S % dil == 0
        res_specs += [pl.BlockSpec((1, TM // dil, dil * LANES), lambda b, i: (b, i, 0))] * 3
        res_shapes += [jax.ShapeDtypeStruct((B, S // dil, dil * LANES), BF16)] * 3
    outs = pl.pallas_call(
        _proj_kernel,
        grid=(B, n_t),
        in_specs=[tok(D), _resident((1, D)), _resident((D, MIX_WIDTH)), _resident((D, MIX_WIDTH)),
                  _resident((D, MIX_WIDTH)),
                  _resident((D, LANES)), _resident((1, LANES)), _resident((D, N_GATE_COLS)),
                  pl.BlockSpec((TM, LANES), lambda b, i: (i, 0)),
                  pl.BlockSpec((TM, LANES), lambda b, i: (i, 0))],
        out_specs=[tok(MIX_WIDTH), tok(MIX_WIDTH), tok(MIX_WIDTH),
                   pl.BlockSpec((1, MIX_WIDTH, TM), lambda b, i: (b, 0, i)),
                   tok(LANES), tok(N_GATE_COLS),
                   pl.BlockSpec((1, 1, TM // MOBA_BLOCK, MOBA_COLS), lambda b, i: (b, i, 0, 0))] + res_specs,
        out_shape=[jax.ShapeDtypeStruct((B, S, MIX_WIDTH), BF16)] * 3
        + [jax.ShapeDtypeStruct((B, MIX_WIDTH, S), BF16),
           jax.ShapeDtypeStruct((B, S, LANES), BF16),
           jax.ShapeDtypeStruct((B, S, N_GATE_COLS), BF16),
           jax.ShapeDtypeStruct((B, n_t, TM // MOBA_BLOCK, MOBA_COLS), F32)] + res_shapes,
        scratch_shapes=[pltpu.VMEM((1, LANES), F32), pltpu.VMEM((len(res_specs), TM, LANES), F32)],
        compiler_params=_params("arbitrary", "arbitrary"),
        name="proj",
    )(x, g, wq, wk, wv, wf, bf, wg, cos_t, sin_t)
    return outs[:7], outs[7:]


def _reduce_keys(x, op, reduce):
    n = x.shape[0]
    while n > SUBLANES and n % (2 * SUBLANES) == 0:
        n //= 2
        x = op(x[:n], x[n:])
    return reduce(x, axis=0, keepdims=True)


def _key_max(s):
    return _reduce_keys(s, jnp.maximum, jnp.max)


def _probabilities(s, m):
    return jnp.exp2((s - m).astype(BF16))


ACC_ROWS = HEAD_DIM + 2 * SUBLANES


def _values_with_ones(vt):
    return jnp.concatenate([vt, jnp.ones((ACC_ROWS - HEAD_DIM, vt.shape[1]), BF16)], axis=0)


def _chunked_attention(i, n_heads, scores, values, s_sc, p_sc):
    heads = range(n_heads)
    n_full = (i * TQ) // CHUNK

    causal = _chunk_causal(n_full * CHUNK, i * TQ)
    s_diag = [score() for score in scores(n_full)]
    s_first = [score() for score in scores(0)]
    m, alpha = [], []
    for hl in heads:
        sd = jnp.where(causal, s_diag[hl], NEG_INF)
        m_diag = _key_max(sd)
        p_sc[hl] = _probabilities(sd, m_diag)
        s_sc[hl] = s_first[hl]
        m.append(jnp.where(n_full > 0, jnp.maximum(m_diag, _key_max(s_first[hl])), m_diag))
        alpha.append(jnp.exp2(m_diag - m[hl]))

    def product(c, p):
        return [_dot(_values_with_ones(values(c, hl)), p[hl]) for hl in heads]

    def before(c):
        return jnp.where(c <= 0, n_full, c - 1)

    def body(c, carry):
        m, alpha, acc = carry
        pv = product(before(c), [p_sc[hl] for hl in heads])
        s_next = [score() for score in scores(c + 1)]
        m_next = tuple(jnp.maximum(m[hl], _key_max(s_next[hl])) for hl in heads)
        alpha_next = tuple(jnp.exp2(m[hl] - m_next[hl]) for hl in heads)
        for hl in heads:
            p_sc[hl] = _probabilities(s_sc[hl], m[hl])
            s_sc[hl] = s_next[hl]
        acc = tuple(alpha[hl] * (acc[hl] + pv[hl]) for hl in heads)
        return m_next, alpha_next, acc

    init = (tuple(m), tuple(alpha), tuple(jnp.zeros((ACC_ROWS, TQ), F32) for _ in heads))
    m, alpha, acc = lax.fori_loop(0, n_full - 1, body, init)

    last = n_full - 1
    pv = product(before(last), [p_sc[hl] for hl in heads])
    m_tail = [jnp.where(n_full > 0, m[hl], -NEG_INF) for hl in heads]
    tail = product(jnp.maximum(last, 0), [_probabilities(s_sc[hl], m_tail[hl]) for hl in heads])
    return [alpha[hl] * (acc[hl] + pv[hl]) + tail[hl] for hl in heads]


def _query_t(q):
    return q.astype(F32).T


def _head_rows(q_t, hl):
    sub = lax.broadcasted_iota(jnp.int32, q_t.shape, 0)
    return jnp.where((sub >= hl * HEAD_DIM) & (sub < (hl + 1) * HEAD_DIM), q_t, 0.0)


def _finish(accs):
    blocks = []
    for p in range(len(accs) // HEADS_PER_LANE_BLOCK):
        pair = accs[p * HEADS_PER_LANE_BLOCK:(p + 1) * HEADS_PER_LANE_BLOCK]
        out_t = jnp.concatenate([acc[:HEAD_DIM] / acc[HEAD_DIM:HEAD_DIM + 1] for acc in pair], axis=0)
        blocks.append(out_t.T)
    return jnp.concatenate(blocks, axis=1)


def _chunk_scratch(n_heads):
    return [pltpu.VMEM((n_heads, CHUNK, TQ), F32), pltpu.VMEM((n_heads, CHUNK, TQ), BF16)]


def _chunk_causal(first_key, first_query):
    kpos = first_key + lax.broadcasted_iota(jnp.int32, (CHUNK, TQ), 0)
    qpos = first_query + lax.broadcasted_iota(jnp.int32, (CHUNK, TQ), 1)
    return kpos <= qpos


def _fox_kernel(q_ref, k_ref, kx_ref, vt_ref, o_ref, s_sc, p_sc):
    i = pl.program_id(1)
    n_pairs = FOX_COLS // LANES
    sub = lax.broadcasted_iota(jnp.int32, (LANES, TQ), 0)
    q_aug = []
    for p in range(n_pairs):
        q_t = _query_t(q_ref[0, :, p * LANES:(p + 1) * LANES])
        for hl in range(HEADS_PER_LANE_BLOCK):
            head = HEADS_PER_LANE_BLOCK * p + hl
            ones = jnp.where((sub % N_HEADS_FOX == head) & (sub < FORGET_PIECES * N_HEADS_FOX), 1.0, 0.0)
            q_aug.append(jnp.concatenate([_head_rows(q_t, hl), ones], axis=0).astype(BF16))

    def scores(c):
        off = pl.multiple_of(c * CHUNK, CHUNK)
        kx = kx_ref[0, pl.ds(off, CHUNK), :]
        out = []
        for p in range(n_pairs):
            k_aug = jnp.concatenate([k_ref[0, pl.ds(off, CHUNK), p * LANES:(p + 1) * LANES], kx], axis=1)
            out += [functools.partial(_dot, k_aug, qa)
                    for qa in q_aug[p * HEADS_PER_LANE_BLOCK:(p + 1) * HEADS_PER_LANE_BLOCK]]
        return out

    def values(c, h):
        off = pl.multiple_of(c * CHUNK, CHUNK)
        return vt_ref[0, h * HEAD_DIM:(h + 1) * HEAD_DIM, pl.ds(off, CHUNK)]

    o_ref[0] = _finish(_chunked_attention(i, N_HEADS_FOX, scores, values, s_sc, p_sc)).astype(BF16)


def _fox(q, k, kx, vt):
    B, S, _ = q.shape
    assert S % CHUNK == 0
    return pl.pallas_call(
        _fox_kernel,
        grid=(B, S // TQ),
        in_specs=[pl.BlockSpec((1, TQ, FOX_COLS), lambda b, i: (b, i, 0)),
                  pl.BlockSpec((1, S, FOX_COLS), lambda b, i: (b, 0, 0)),
                  pl.BlockSpec((1, S, LANES), lambda b, i: (b, 0, 0)),
                  pl.BlockSpec((1, FOX_COLS, S), lambda b, i: (b, 0, 0))],
        out_specs=pl.BlockSpec((1, TQ, FOX_COLS), lambda b, i: (b, i, 0)),
        out_shape=jax.ShapeDtypeStruct((B, S, FOX_COLS), BF16),
        scratch_shapes=_chunk_scratch(N_HEADS_FOX),
        compiler_params=_params("parallel", "arbitrary"),
        name="fox",
    )(q, k, kx, vt)


WINDOW_KEYS = DIL_PATTERNS[0][0] // DIL_PATTERNS[0][1]
WQ = 256
assert all(w // d == WINDOW_KEYS for w, d in DIL_PATTERNS) and WINDOW_KEYS <= WQ


WINDOW_CHAINS = 4


def _window_kernel(q_ref, kc_ref, kp_ref, vc_ref, vp_ref, o_ref, lse_ref, *, dil, n_res, n_sub):
    a = pl.program_id(1)
    rg = pl.program_id(2)
    kr = lax.broadcasted_iota(jnp.int32, (WQ, WQ), 0)
    qc = lax.broadcasted_iota(jnp.int32, (WQ, WQ), 1)
    valid_d = (kr <= qc) & (qc - kr <= WINDOW_KEYS)
    krp = lax.broadcasted_iota(jnp.int32, (WINDOW_KEYS, WQ), 0)
    qcp = lax.broadcasted_iota(jnp.int32, (WINDOW_KEYS, WQ), 1)
    valid_p = krp >= qcp
    valid_first = valid_p & (a > 0)

    chains = [(rr, t) for rr in range(n_res) for t in range(n_sub)]
    heads = range(HEADS_PER_LANE_BLOCK)
    raw, vals = [], []
    for rr, t in chains:
        cols = slice(rr * LANES, (rr + 1) * LANES)
        sub = slice(t * WQ, (t + 1) * WQ)
        q_t = _query_t(q_ref[0, sub, cols])
        if t == 0:
            kp, vp = kp_ref[0, :, cols], vp_ref[0, :, cols]
        else:
            before = slice(t * WQ - WINDOW_KEYS, t * WQ)
            kp, vp = kc_ref[0, before, cols], vc_ref[0, before, cols]
        kd = kc_ref[0, sub, cols]
        vals.append((vc_ref[0, sub, cols], vp))
        q_m = [_head_rows(q_t, hl).astype(BF16) for hl in heads]
        raw.append([(_dot(kd, qm), _dot(kp, qm)) for qm in q_m])

    probs, stats = [], []
    for n, (rr, t) in enumerate(chains):
        for hl in heads:
            s_d = jnp.where(valid_d, raw[n][hl][0], NEG_INF)
            s_p = jnp.where(valid_first if t == 0 else valid_p, raw[n][hl][1], NEG_INF)
            m = jnp.maximum(_key_max(s_d), _key_max(s_p))
            p_d = jnp.exp2(s_d - m)
            p_p = jnp.exp2(s_p - m)
            l = _reduce_keys(p_d, jnp.add, jnp.sum) + _reduce_keys(p_p, jnp.add, jnp.sum)
            probs.append((p_d.astype(BF16), p_p.astype(BF16)))
            stats.append((m, l))

    accs = [_dot_tn(vals[n][0], probs[n * len(heads) + hl][0]) + _dot_tn(vals[n][1], probs[n * len(heads) + hl][1])
            for n in range(len(chains)) for hl in heads]

    for n, (rr, t) in enumerate(chains):
        outs, lses = [], []
        for hl in heads:
            m, l = stats[n * len(heads) + hl]
            outs.append(accs[n * len(heads) + hl][hl * HEAD_DIM:(hl + 1) * HEAD_DIM] / l)
            lses.append(jnp.broadcast_to(m + jnp.log2(l), (HEAD_DIM, WQ)))
        r = rg * n_res + rr
        rows = pl.ds(t * WQ * dil + r, WQ, stride=dil) if dil > 1 else pl.ds(t * WQ, WQ)
        o_ref[0, rows, :] = jnp.concatenate(outs, axis=0).T
        lse_ref[0, rows, :] = jnp.concatenate(lses, axis=0).T


def _window(q, k, v, group):
    dil = DIL_PATTERNS[group][1]
    B, rows, width = q.shape
    S = rows * dil
    n_res = min(dil, 2)
    n_sub = WINDOW_CHAINS // n_res
    tile = n_sub * WQ
    assert rows % tile == 0 and tile % WINDOW_KEYS == 0 and dil % n_res == 0
    col0, col_step = (FOX_COLS // LANES + group, 0) if dil == 1 else (0, 1)
    assert width == (MIX_WIDTH if dil == 1 else dil * LANES)
    cur = pl.BlockSpec((1, tile, n_res * LANES), lambda b, a, rg: (b, a, col0 + col_step * rg))
    per = tile // WINDOW_KEYS
    prev = pl.BlockSpec((1, WINDOW_KEYS, n_res * LANES),
                        lambda b, a, rg: (b, jnp.maximum(a * per - 1, 0), col0 + col_step * rg))
    out = pl.BlockSpec((1, dil * tile, LANES), lambda b, a, rg: (b, a, 0))
    o, lse = pl.pallas_call(
        functools.partial(_window_kernel, dil=dil, n_res=n_res, n_sub=n_sub),
        grid=(B, rows // tile, dil // n_res),
        in_specs=[cur, cur, prev, cur, prev],
        out_specs=[out, out],
        out_shape=[jax.ShapeDtypeStruct((B, S, LANES), F32)] * 2,
        compiler_params=_params("parallel", "arbitrary", "arbitrary"),
        name=f"window{group}",
    )(q, k, k, v, v)
    return o.reshape(B * S, LANES), lse.reshape(B * S, LANES)


MOBA_PAIRS = MOBA_COLS // LANES


def _moba_kernel(*refs):
    q_refs, k_refs, vt_refs = refs[:MOBA_PAIRS], refs[MOBA_PAIRS:2 * MOBA_PAIRS], refs[2 * MOBA_PAIRS:3 * MOBA_PAIRS]
    km_ref, o_ref, s_sc, p_sc = refs[3 * MOBA_PAIRS:]
    i = pl.program_id(1)
    n_blocks = km_ref.shape[1]
    blk = lax.broadcasted_iota(jnp.int32, (n_blocks, TQ), 0)
    own = i * (TQ // MOBA_BLOCK) + lax.broadcasted_iota(jnp.int32, (n_blocks, TQ), 1) // MOBA_BLOCK
    past = blk < own
    never = jnp.full((LANES - n_blocks, TQ), NEG_INF, F32)
    q_aug = []
    for p in range(MOBA_PAIRS):
        q_t = _query_t(q_refs[p][0])
        km = km_ref[0, :, p * LANES:(p + 1) * LANES]
        for hl in range(HEADS_PER_LANE_BLOCK):
            qm = _head_rows(q_t, hl)
            gate = jnp.where(past, _dot(km, qm.astype(BF16)), NEG_INF)
            sel = blk == own
            for _ in range(MOBA_TOPK):
                top = jnp.max(gate, axis=0, keepdims=True)
                idx = jnp.min(jnp.where(gate == top, blk, n_blocks), axis=0, keepdims=True)
                hit = blk == idx
                sel = sel | (hit & past)
                gate = jnp.where(hit, -jnp.inf, gate)
            sel_bias = jnp.where(sel, 0.0, NEG_INF)
            q_aug.append(jnp.concatenate([qm, sel_bias, never], axis=0).astype(BF16))

    row = lax.broadcasted_iota(jnp.int32, (CHUNK, LANES), 0)
    lane = lax.broadcasted_iota(jnp.int32, (CHUNK, LANES), 1)
    blocks_per_chunk = CHUNK // MOBA_BLOCK

    def scores(c):
        off = pl.multiple_of(c * CHUNK, CHUNK)
        onehot = jnp.where(lane == c * blocks_per_chunk + row // MOBA_BLOCK, 1.0, 0.0).astype(BF16)
        out = []
        for p in range(MOBA_PAIRS):
            k_aug = jnp.concatenate([k_refs[p][0, pl.ds(off, CHUNK), :], onehot], axis=1)
            out += [functools.partial(_dot, k_aug, qa)
                    for qa in q_aug[p * HEADS_PER_LANE_BLOCK:(p + 1) * HEADS_PER_LANE_BLOCK]]
        return out

    def values(c, h):
        off = pl.multiple_of(c * CHUNK, CHUNK)
        p, hl = divmod(h, HEADS_PER_LANE_BLOCK)
        return vt_refs[p][0, hl * HEAD_DIM:(hl + 1) * HEAD_DIM, pl.ds(off, CHUNK)]

    o_ref[0] = _finish(_chunked_attention(i, N_HEADS_MOBA, scores, values, s_sc, p_sc)).astype(BF16)


def _moba(q, k, vt, km):
    B, S, _ = q.shape
    n_blocks = S // MOBA_BLOCK
    assert TQ % MOBA_BLOCK == 0 and CHUNK % MOBA_BLOCK == 0 and S % CHUNK == 0
    assert n_blocks <= LANES and n_blocks % (2 * SUBLANES) == 0
    first = (FOX_COLS + DIL_COLS) // LANES
    at = lambda f: [pl.BlockSpec(*f(first + p)) for p in range(MOBA_PAIRS)]
    return pl.pallas_call(
        _moba_kernel,
        grid=(B, S // TQ),
        in_specs=at(lambda c: ((1, TQ, LANES), lambda b, i: (b, i, c)))
        + at(lambda c: ((1, S, LANES), lambda b, i: (b, 0, c)))
        + at(lambda c: ((1, LANES, S), lambda b, i: (b, c, 0)))
        + [pl.BlockSpec((1, n_blocks, MOBA_COLS), lambda b, i: (b, 0, 0))],
        out_specs=pl.BlockSpec((1, TQ, MOBA_COLS), lambda b, i: (b, i, 0)),
        out_shape=jax.ShapeDtypeStruct((B, S, MOBA_COLS), BF16),
        scratch_shapes=_chunk_scratch(N_HEADS_MOBA),
        compiler_params=_params("parallel", "arbitrary"),
        name="moba",
    )(*([q] * MOBA_PAIRS + [k] * MOBA_PAIRS + [vt] * MOBA_PAIRS + [km]))


MXU_TILE = 256
assert D_FF % MXU_TILE == 0
FF_SPLITS = (0, (D_FF // MXU_TILE + 1) // 2 * MXU_TILE, D_FF)


MIX_SPLIT = 2


def _mix_slab(rows, ya_ref, o_refs, l_refs, yc_ref, gates_ref, x_ref, p_ref,
              wa_ref, wb_ref, wc_ref, wo_ref, g_mix_ref,
              g_pre_ref, wgate_ref, wup_ref, wdown_ref, g_post_ref, wple_ref, wpg_ref, g_ple_ref, o_ref):
    lses = [l_ref[rows, :] for l_ref in l_refs]
    top = jnp.maximum(jnp.maximum(lses[0], lses[1]), lses[2])
    wts = [jnp.exp2(t - top) for t in lses]
    den = wts[0] + wts[1] + wts[2]
    yb = (wts[0] * o_refs[0][rows, :] + wts[1] * o_refs[1][rows, :] + wts[2] * o_refs[2][rows, :]) / den
    merged = None
    for c, (y, w_ref) in enumerate(((ya_ref[rows, :], wa_ref), (yb.astype(BF16), wb_ref), (yc_ref[rows, :], wc_ref))):
        branch = _dot(y, w_ref[...])
        yield
        term = gates_ref[rows, c * D_MODEL:(c + 1) * D_MODEL].astype(F32) * branch
        merged = term if merged is None else merged + term
    out = _dot(merged.astype(BF16), wo_ref[...])
    yield
    x = x_ref[rows, :] + _rms(out, g_mix_ref[...])

    hb = _rms(x, g_pre_ref[...]).astype(BF16)
    down = None
    for lo, hi in zip(FF_SPLITS[:-1], FF_SPLITS[1:]):
        cols = slice(lo, hi)
        gate = _dot(hb, wgate_ref[:, cols])
        yield
        up = _dot(hb, wup_ref[:, cols])
        yield
        ff = (gate * jax.nn.sigmoid(gate) * up).astype(BF16)
        part = _dot(ff, wdown_ref[cols, :])
        yield
        down = part if down is None else down + part
    x = x + _rms(down, g_post_ref[...])

    emb = _dot(p_ref[rows, :].astype(BF16), wple_ref[...])
    yield
    ple = emb * jax.nn.sigmoid(_dot(x.astype(BF16), wpg_ref[...]))
    yield
    o_ref[rows, :] = x + _rms(ple, g_ple_ref[...])


def _mix_kernel(ya_ref, o0_ref, o1_ref, o2_ref, l0_ref, l1_ref, l2_ref, *rest):
    slab = TM // MIX_SPLIT
    chains = [_mix_slab(pl.ds(n * slab, slab), ya_ref, (o0_ref, o1_ref, o2_ref), (l0_ref, l1_ref, l2_ref), *rest)
              for n in range(MIX_SPLIT)]
    while chains:
        chains = [c for c in chains if next(c, StopIteration) is not StopIteration]


def _mix(ya, ob, lb, yc, gates, x, p, mix_weights, g_mix, g_pre, ffn_weights, g_post, ple_weights, g_ple):
    T, D = x.shape
    tok = lambda w: pl.BlockSpec((TM, w), lambda i: (i, 0))
    vec = _resident((1, D))
    whole = lambda ws: [_resident(w.shape) for w in ws]
    return pl.pallas_call(
        _mix_kernel,
        grid=(T // TM,),
        in_specs=[tok(FOX_COLS)] + [tok(LANES)] * 6 + [tok(MOBA_COLS), tok(N_GATE_COLS), tok(D), tok(PLE_DIM)]
        + whole(mix_weights) + [vec, vec] + whole(ffn_weights) + [vec] + whole(ple_weights) + [vec],
        out_specs=tok(D),
        out_shape=jax.ShapeDtypeStruct((T, D), F32),
        compiler_params=_params("parallel"),
        name="mix",
    )(ya, *ob, *lb, yc, gates, x, p, *mix_weights, g_mix, g_pre, *ffn_weights, g_post, *ple_weights, g_ple)


def _rope_tables(seq):
    inv = 1.0 / (ROPE_THETA ** (jnp.arange(0, HEAD_DIM, 2, dtype=F32) / HEAD_DIM))
    ang = jnp.arange(seq, dtype=F32)[:, None] * inv[None, :]
    cos, sin = jnp.cos(ang), jnp.sin(ang)
    reps = LANES // HEAD_DIM
    cos_t = jnp.tile(jnp.concatenate([cos, cos], axis=1), (1, reps))
    sin_t = jnp.tile(jnp.concatenate([-sin, sin], axis=1), (1, reps))
    return cos_t, sin_t


def kernel(x, p, g_mix_pre, w_in, b_f, w_br_a, w_br_b, w_br_c, w_out, g_mix_post, g_ffn_pre,
           w_ffn_gate, w_ffn_up, w_ffn_down, g_ffn_post, w_ple, w_ple_gate, g_ple_post):
    B, S, D = x.shape
    T = B * S
    depth = w_in.shape[0]
    cos_t, sin_t = _rope_tables(S)
    row = lambda g: g.reshape(1, -1)
    for i in range(depth):
        wq, wk, wv, wf, wg = _split_w_in(w_in, i)
        bf = jnp.pad(b_f[i], (0, LANES - N_HEADS_FOX)).reshape(1, LANES)

        (q, k, v, vt, kx, gates, kmean), residue_views = _proj(
            x, row(g_mix_pre[i]), wq, wk, wv, wf, bf, wg, cos_t, sin_t)
        km = kmean.reshape(B, S // MOBA_BLOCK, MOBA_COLS).astype(BF16)
        y_a = _fox(q, k, kx, vt)
        win_in = {g: residue_views[3 * n:3 * n + 3] for n, (g, _) in enumerate(STRIDED_GROUPS)}
        win = [_window(*win_in.get(g, (q, k, v)), g) for g in range(len(DIL_PATTERNS))]
        y_c = _moba(q, k, vt, km)

        bf16 = lambda *ws: [w[i].astype(BF16) for w in ws]
        x = _mix(y_a.reshape(T, -1), [o for o, _ in win], [l for _, l in win], y_c.reshape(T, -1),
                 gates.reshape(T, -1), x.reshape(T, D), p[i].reshape(T, PLE_DIM),
                 bf16(w_br_a, w_br_b, w_br_c, w_out), row(g_mix_post[i]),
                 row(g_ffn_pre[i]), bf16(w_ffn_gate, w_ffn_up, w_ffn_down), row(g_ffn_post[i]),
                 bf16(w_ple, w_ple_gate), row(g_ple_post[i])).reshape(B, S, D)
    return x
```

```python
import functools
import math

import numpy as np
import jax
import jax.numpy as jnp
from jax import lax
from jax.experimental import pallas as pl
from jax.experimental.pallas import tpu as pltpu

D_MODEL = 1024
HEAD_DIM = 64
N_HEADS_FOX = 4
DIL_PATTERNS = ((128, 1), (512, 4), (2048, 16))
N_HEADS_MOBA = 6
MIX_WIDTH = 1024
MOBA_BLOCK = 256
MOBA_TOPK = 3
PLE_DIM = 256
D_FF = 2816
ROPE_THETA = 10000.0
RMS_EPS = 1e-6
NEG_INF = -1e30
LOG2E = math.log2(math.e)
Q_SCALE = HEAD_DIM ** -0.5 * LOG2E

LANES = 128
SUBLANES = 8
HEADS_PER_LANE_BLOCK = LANES // HEAD_DIM
N_LANE_BLOCKS = MIX_WIDTH // LANES
FOX_COLS = N_HEADS_FOX * HEAD_DIM
DIL_COLS = len(DIL_PATTERNS) * 2 * HEAD_DIM
MOBA_COLS = N_HEADS_MOBA * HEAD_DIM
N_GATE_COLS = 3 * D_MODEL
FORGET_PIECES = 3

TM = 512
TQ = 512
CHUNK = 512
VMEM_LIMIT = 56 * 1024 * 1024

F32 = jnp.float32
BF16 = jnp.bfloat16


def _dot(a, b):
    return jnp.dot(a, b, preferred_element_type=F32)


def _dot_nt(a, b):
    return lax.dot_general(a, b, (((1,), (1,)), ((), ())), preferred_element_type=F32)


def _dot_tn(a, b):
    return lax.dot_general(a, b, (((0,), (0,)), ((), ())), preferred_element_type=F32)


def _rms(x, g):
    var = jnp.mean(x * x, axis=-1, keepdims=True)
    return x * lax.rsqrt(var + RMS_EPS) * g


def _split3(x):
    hi = x.astype(BF16).astype(F32)
    r1 = x - hi
    mid = r1.astype(BF16).astype(F32)
    lo = (r1 - mid).astype(BF16).astype(F32)
    return hi, mid, lo


def _resident(shape):
    nd = len(shape)
    return pl.BlockSpec(shape, lambda *_: (0,) * nd, pipeline_mode=pl.Buffered(1))


def _params(*sem):
    return pltpu.CompilerParams(dimension_semantics=sem, vmem_limit_bytes=VMEM_LIMIT)


STRIDED_GROUPS = tuple((g, dil) for g, (_, dil) in enumerate(DIL_PATTERNS) if dil > 1)


PROJ_SLAB = MOBA_BLOCK


def _proj_slab(n, carry, x_ref, g_ref, wq_ref, wk_ref, wv_ref, wf_ref, bf_ref, wg_ref, cos_ref, sin_ref,
               q_ref, k_ref, v_ref, vt_ref, kx_ref, gates_ref, kmean_ref, res_refs, perm_sc):
    rows = pl.ds(n * PROJ_SLAB, PROJ_SLAB)

    def scatter_residues(t, which, blk):
        for m, (g, dil) in enumerate(STRIDED_GROUPS):
            if blk == FOX_COLS // LANES + g:
                slot = 3 * m + which
                per = PROJ_SLAB // dil
                perm_sc[slot, rows, :] = t
                for r in range(dil):
                    res_refs[slot][0, n * per:(n + 1) * per, r * LANES:(r + 1) * LANES] = perm_sc[
                        slot, pl.ds(n * PROJ_SLAB + r, per, stride=dil), :].astype(BF16)

    hb = _rms(x_ref[0, rows, :], g_ref[...]).astype(BF16)
    cos = cos_ref[rows, :]
    sin = sin_ref[rows, :]
    lane = lax.broadcasted_iota(jnp.int32, (PROJ_SLAB, LANES), 1)
    first_half = (lane % HEAD_DIM) < (HEAD_DIM // 2)

    def rope(t):
        rot = jnp.where(first_half, pltpu.roll(t, LANES - HEAD_DIM // 2, 1),
                        pltpu.roll(t, HEAD_DIM // 2, 1))
        return t * cos + rot * sin

    first_rope_blk = FOX_COLS // LANES
    first_moba_blk = (FOX_COLS + DIL_COLS) // LANES

    f = _dot(hb, wf_ref[...]) + bf_ref[...]
    ls = jnp.minimum(f, 0.0) - jnp.log1p(jnp.exp(-jnp.abs(f)))
    ls = jnp.where(lane < N_HEADS_FOX, ls, 0.0)
    row = lax.broadcasted_iota(jnp.int32, (PROJ_SLAB, PROJ_SLAB), 0)
    col = lax.broadcasted_iota(jnp.int32, (PROJ_SLAB, PROJ_SLAB), 1)
    tri = jnp.where(col <= row, 1.0, 0.0).astype(BF16)
    ls_hi, ls_mid, ls_lo = _split3(ls)

    q = _dot(hb, wq_ref[...])
    yield
    cs = _dot(tri, ls_hi.astype(BF16)) + _dot(tri, ls_mid.astype(BF16)) + _dot(tri, ls_lo.astype(BF16))
    for c in range(N_LANE_BLOCKS):
        t = q[:, c * LANES:(c + 1) * LANES]
        if c >= first_rope_blk:
            t = rope(t)
        t = t * Q_SCALE
        q_ref[0, rows, c * LANES:(c + 1) * LANES] = t.astype(BF16)
        scatter_residues(t, 0, c)

    k = _dot(hb, wk_ref[...])
    yield
    cs = cs + carry[n]
    carry.append(cs[PROJ_SLAB - 1:PROJ_SLAB, :])
    hi, mid, lo = _split3(-LOG2E * cs)
    pieces = jnp.where(lane < N_HEADS_FOX, hi,
                       jnp.where(lane < 2 * N_HEADS_FOX, pltpu.roll(mid, N_HEADS_FOX, 1),
                                 pltpu.roll(lo, 2 * N_HEADS_FOX, 1)))
    kx_ref[0, rows, :] = pieces.astype(BF16)

    for c in range(N_LANE_BLOCKS):
        t = k[:, c * LANES:(c + 1) * LANES]
        if c >= first_rope_blk:
            t = rope(t)
        k_ref[0, rows, c * LANES:(c + 1) * LANES] = t.astype(BF16)
        scatter_residues(t, 1, c)
        if c >= first_moba_blk:
            cm = c - first_moba_blk
            kmean_ref[0, 0, n:n + 1, cm * LANES:(cm + 1) * LANES] = jnp.mean(t, axis=0, keepdims=True)

    v = _dot(hb, wv_ref[...])
    yield
    v_ref[0, rows, :] = v.astype(BF16)
    for c in range(N_LANE_BLOCKS):
        scatter_residues(v[:, c * LANES:(c + 1) * LANES], 2, c)
    vt_ref[0, :, rows] = v.T.astype(BF16)

    for c in range(N_GATE_COLS // D_MODEL):
        z = _dot(hb, wg_ref[:, c * D_MODEL:(c + 1) * D_MODEL])
        yield
        gates_ref[0, rows, c * D_MODEL:(c + 1) * D_MODEL] = jax.nn.sigmoid(z).astype(BF16)


def _proj_kernel(*refs):
    *io_refs, carry_ref, perm_sc = refs
    n_res = 3 * len(STRIDED_GROUPS)
    main_refs, res_refs = io_refs[:len(io_refs) - n_res], io_refs[len(io_refs) - n_res:]

    @pl.when(pl.program_id(1) == 0)
    def _():
        carry_ref[...] = jnp.zeros_like(carry_ref)

    carry = [carry_ref[...]]
    chains = [_proj_slab(n, carry, *main_refs, res_refs, perm_sc) for n in range(TM // PROJ_SLAB)]
    while chains:
        chains = [c for c in chains if next(c, StopIteration) is not StopIteration]
    carry_ref[...] = carry[-1]


def _proj(x, g, wq, wk, wv, wf, bf, wg, cos_t, sin_t):
    B, S, D = x.shape
    n_t = S // TM
    tok = lambda w: pl.BlockSpec((1, TM, w), lambda b, i: (b, i, 0))
    res_specs, res_shapes = [], []
    for _, dil in STRIDED_GROUPS:
        assert TM % (dil * 2 * SUBLANES) == 0 and S % dil == 0
        res_specs += [pl.BlockSpec((1, TM // dil, dil * LANES), lambda b, i: (b, i, 0))] * 3
        res_shapes += [jax.ShapeDtypeStruct((B, S // dil, dil * LANES), BF16)] * 3
    outs = pl.pallas_call(
        _proj_kernel,
        grid=(B, n_t),
        in_specs=[tok(D), _resident((1, D)), _resident((D, MIX_WIDTH)), _resident((D, MIX_WIDTH)),
                  _resident((D, MIX_WIDTH)),
                  _resident((D, LANES)), _resident((1, LANES)), _resident((D, N_GATE_COLS)),
                  pl.BlockSpec((TM, LANES), lambda b, i: (i, 0)),
                  pl.BlockSpec((TM, LANES), lambda b, i: (i, 0))],
        out_specs=[tok(MIX_WIDTH), tok(MIX_WIDTH), tok(MIX_WIDTH),
                   pl.BlockSpec((1, MIX_WIDTH, TM), lambda b, i: (b, 0, i)),
                   tok(LANES), tok(N_GATE_COLS),
                   pl.BlockSpec((1, 1, TM // MOBA_BLOCK, MOBA_COLS), lambda b, i: (b, i, 0, 0))] + res_specs,
        out_shape=[jax.ShapeDtypeStruct((B, S, MIX_WIDTH), BF16)] * 3
        + [jax.ShapeDtypeStruct((B, MIX_WIDTH, S), BF16),
           jax.ShapeDtypeStruct((B, S, LANES), BF16),
           jax.ShapeDtypeStruct((B, S, N_GATE_COLS), BF16),
           jax.ShapeDtypeStruct((B, n_t, TM // MOBA_BLOCK, MOBA_COLS), F32)] + res_shapes,
        scratch_shapes=[pltpu.VMEM((1, LANES), F32), pltpu.VMEM((len(res_specs), TM, LANES), F32)],
        compiler_params=_params("arbitrary", "arbitrary"),
        name="proj",
    )(x, g, wq, wk, wv, wf, bf, wg, cos_t, sin_t)
    return outs[:7], outs[7:]


def _reduce_keys(x, op, reduce):
    n = x.shape[0]
    while n > SUBLANES and n % (2 * SUBLANES) == 0:
        n //= 2
        x = op(x[:n], x[n:])
    return reduce(x, axis=0, keepdims=True)


def _key_max(s):
    return _reduce_keys(s, jnp.maximum, jnp.max)


def _probabilities(s, m):
    return jnp.exp2((s - m).astype(BF16))


ACC_ROWS = HEAD_DIM + 2 * SUBLANES


def _values_with_ones(vt):
    return jnp.concatenate([vt, jnp.ones((ACC_ROWS - HEAD_DIM, vt.shape[1]), BF16)], axis=0)


def _chunked_attention(i, n_heads, scores, values, s_sc, p_sc):
    heads = range(n_heads)
    n_full = (i * TQ) // CHUNK

    causal = _chunk_causal(n_full * CHUNK, i * TQ)
    s_diag = [score() for score in scores(n_full)]
    s_first = [score() for score in scores(0)]
    m, alpha = [], []
    for hl in heads:
        sd = jnp.where(causal, s_diag[hl], NEG_INF)
        m_diag = _key_max(sd)
        p_sc[hl] = _probabilities(sd, m_diag)
        s_sc[hl] = s_first[hl]
        m.append(jnp.where(n_full > 0, jnp.maximum(m_diag, _key_max(s_first[hl])), m_diag))
        alpha.append(jnp.exp2(m_diag - m[hl]))

    def product(c, p):
        return [_dot(_values_with_ones(values(c, hl)), p[hl]) for hl in heads]

    def before(c):
        return jnp.where(c <= 0, n_full, c - 1)

    def body(c, carry):
        m, alpha, acc = carry
        pv = product(before(c), [p_sc[hl] for hl in heads])
        s_next = [score() for score in scores(c + 1)]
        m_next = tuple(jnp.maximum(m[hl], _key_max(s_next[hl])) for hl in heads)
        alpha_next = tuple(jnp.exp2(m[hl] - m_next[hl]) for hl in heads)
        for hl in heads:
            p_sc[hl] = _probabilities(s_sc[hl], m[hl])
            s_sc[hl] = s_next[hl]
        acc = tuple(alpha[hl] * (acc[hl] + pv[hl]) for hl in heads)
        return m_next, alpha_next, acc

    init = (tuple(m), tuple(alpha), tuple(jnp.zeros((ACC_ROWS, TQ), F32) for _ in heads))
    m, alpha, acc = lax.fori_loop(0, n_full - 1, body, init)

    last = n_full - 1
    pv = product(before(last), [p_sc[hl] for hl in heads])
    m_tail = [jnp.where(n_full > 0, m[hl], -NEG_INF) for hl in heads]
    tail = product(jnp.maximum(last, 0), [_probabilities(s_sc[hl], m_tail[hl]) for hl in heads])
    return [alpha[hl] * (acc[hl] + pv[hl]) + tail[hl] for hl in heads]


def _query_t(q):
    return q.astype(F32).T


def _head_rows(q_t, hl):
    sub = lax.broadcasted_iota(jnp.int32, q_t.shape, 0)
    return jnp.where((sub >= hl * HEAD_DIM) & (sub < (hl + 1) * HEAD_DIM), q_t, 0.0)


def _finish(accs):
    blocks = []
    for p in range(len(accs) // HEADS_PER_LANE_BLOCK):
        pair = accs[p * HEADS_PER_LANE_BLOCK:(p + 1) * HEADS_PER_LANE_BLOCK]
        out_t = jnp.concatenate([acc[:HEAD_DIM] / acc[HEAD_DIM:HEAD_DIM + 1] for acc in pair], axis=0)
        blocks.append(out_t.T)
    return jnp.concatenate(blocks, axis=1)


def _chunk_scratch(n_heads):
    return [pltpu.VMEM((n_heads, CHUNK, TQ), F32), pltpu.VMEM((n_heads, CHUNK, TQ), BF16)]


def _chunk_causal(first_key, first_query):
    kpos = first_key + lax.broadcasted_iota(jnp.int32, (CHUNK, TQ), 0)
    qpos = first_query + lax.broadcasted_iota(jnp.int32, (CHUNK, TQ), 1)
    return kpos <= qpos


def _fox_kernel(q_ref, k_ref, kx_ref, vt_ref, o_ref, s_sc, p_sc):
    i = pl.program_id(1)
    n_pairs = FOX_COLS // LANES
    sub = lax.broadcasted_iota(jnp.int32, (LANES, TQ), 0)
    q_aug = []
    for p in range(n_pairs):
        q_t = _query_t(q_ref[0, :, p * LANES:(p + 1) * LANES])
        for hl in range(HEADS_PER_LANE_BLOCK):
            head = HEADS_PER_LANE_BLOCK * p + hl
            ones = jnp.where((sub % N_HEADS_FOX == head) & (sub < FORGET_PIECES * N_HEADS_FOX), 1.0, 0.0)
            q_aug.append(jnp.concatenate([_head_rows(q_t, hl), ones], axis=0).astype(BF16))

    def scores(c):
        off = pl.multiple_of(c * CHUNK, CHUNK)
        kx = kx_ref[0, pl.ds(off, CHUNK), :]
        out = []
        for p in range(n_pairs):
            k_aug = jnp.concatenate([k_ref[0, pl.ds(off, CHUNK), p * LANES:(p + 1) * LANES], kx], axis=1)
            out += [functools.partial(_dot, k_aug, qa)
                    for qa in q_aug[p * HEADS_PER_LANE_BLOCK:(p + 1) * HEADS_PER_LANE_BLOCK]]
        return out

    def values(c, h):
        off = pl.multiple_of(c * CHUNK, CHUNK)
        return vt_ref[0, h * HEAD_DIM:(h + 1) * HEAD_DIM, pl.ds(off, CHUNK)]

    o_ref[0] = _finish(_chunked_attention(i, N_HEADS_FOX, scores, values, s_sc, p_sc)).astype(BF16)


def _fox(q, k, kx, vt):
    B, S, _ = q.shape
    assert S % CHUNK == 0
    return pl.pallas_call(
        _fox_kernel,
        grid=(B, S // TQ),
        in_specs=[pl.BlockSpec((1, TQ, FOX_COLS), lambda b, i: (b, i, 0)),
                  pl.BlockSpec((1, S, FOX_COLS), lambda b, i: (b, 0, 0)),
                  pl.BlockSpec((1, S, LANES), lambda b, i: (b, 0, 0)),
                  pl.BlockSpec((1, FOX_COLS, S), lambda b, i: (b, 0, 0))],
        out_specs=pl.BlockSpec((1, TQ, FOX_COLS), lambda b, i: (b, i, 0)),
        out_shape=jax.ShapeDtypeStruct((B, S, FOX_COLS), BF16),
        scratch_shapes=_chunk_scratch(N_HEADS_FOX),
        compiler_params=_params("parallel", "arbitrary"),
        name="fox",
    )(q, k, kx, vt)


WINDOW_KEYS = DIL_PATTERNS[0][0] // DIL_PATTERNS[0][1]
WQ = 256
assert all(w // d == WINDOW_KEYS for w, d in DIL_PATTERNS) and WINDOW_KEYS <= WQ


WINDOW_CHAINS = 4


def _window_kernel(q_ref, kc_ref, kp_ref, vc_ref, vp_ref, o_ref, lse_ref, *, dil, n_res, n_sub):
    a = pl.program_id(1)
    rg = pl.program_id(2)
    kr = lax.broadcasted_iota(jnp.int32, (WQ, WQ), 0)
    qc = lax.broadcasted_iota(jnp.int32, (WQ, WQ), 1)
    valid_d = (kr <= qc) & (qc - kr <= WINDOW_KEYS)
    krp = lax.broadcasted_iota(jnp.int32, (WINDOW_KEYS, WQ), 0)
    qcp = lax.broadcasted_iota(jnp.int32, (WINDOW_KEYS, WQ), 1)
    valid_p = krp >= qcp
    valid_first = valid_p & (a > 0)

    chains = [(rr, t) for rr in range(n_res) for t in range(n_sub)]
    heads = range(HEADS_PER_LANE_BLOCK)
    raw, vals = [], []
    for rr, t in chains:
        cols = slice(rr * LANES, (rr + 1) * LANES)
        sub = slice(t * WQ, (t + 1) * WQ)
        q_t = _query_t(q_ref[0, sub, cols])
        if t == 0:
            kp, vp = kp_ref[0, :, cols], vp_ref[0, :, cols]
        else:
            before = slice(t * WQ - WINDOW_KEYS, t * WQ)
            kp, vp = kc_ref[0, before, cols], vc_ref[0, before, cols]
        kd = kc_ref[0, sub, cols]
        vals.append((vc_ref[0, sub, cols], vp))
        q_m = [_head_rows(q_t, hl).astype(BF16) for hl in heads]
        raw.append([(_dot(kd, qm), _dot(kp, qm)) for qm in q_m])

    probs, stats = [], []
    for n, (rr, t) in enumerate(chains):
        for hl in heads:
            s_d = jnp.where(valid_d, raw[n][hl][0], NEG_INF)
            s_p = jnp.where(valid_first if t == 0 else valid_p, raw[n][hl][1], NEG_INF)
            m = jnp.maximum(_key_max(s_d), _key_max(s_p))
            p_d = jnp.exp2(s_d - m)
            p_p = jnp.exp2(s_p - m)
            l = _reduce_keys(p_d, jnp.add, jnp.sum) + _reduce_keys(p_p, jnp.add, jnp.sum)
            probs.append((p_d.astype(BF16), p_p.astype(BF16)))
            stats.append((m, l))

    accs = [_dot_tn(vals[n][0], probs[n * len(heads) + hl][0]) + _dot_tn(vals[n][1], probs[n * len(heads) + hl][1])
            for n in range(len(chains)) for hl in heads]

    for n, (rr, t) in enumerate(chains):
        outs, lses = [], []
        for hl in heads:
            m, l = stats[n * len(heads) + hl]
            outs.append(accs[n * len(heads) + hl][hl * HEAD_DIM:(hl + 1) * HEAD_DIM] / l)
            lses.append(jnp.broadcast_to(m + jnp.log2(l), (HEAD_DIM, WQ)))
        r = rg * n_res + rr
        rows = pl.ds(t * WQ * dil + r, WQ, stride=dil) if dil > 1 else pl.ds(t * WQ, WQ)
        o_ref[0, rows, :] = jnp.concatenate(outs, axis=0).T
        lse_ref[0, rows, :] = jnp.concatenate(lses, axis=0).T


def _window(q, k, v, group):
    dil = DIL_PATTERNS[group][1]
    B, rows, width = q.shape
    S = rows * dil
    n_res = min(dil, 2)
    n_sub = WINDOW_CHAINS // n_res
    tile = n_sub * WQ
    assert rows % tile == 0 and tile % WINDOW_KEYS == 0 and dil % n_res == 0
    col0, col_step = (FOX_COLS // LANES + group, 0) if dil == 1 else (0, 1)
    assert width == (MIX_WIDTH if dil == 1 else dil * LANES)
    cur = pl.BlockSpec((1, tile, n_res * LANES), lambda b, a, rg: (b, a, col0 + col_step * rg))
    per = tile // WINDOW_KEYS
    prev = pl.BlockSpec((1, WINDOW_KEYS, n_res * LANES),
                        lambda b, a, rg: (b, jnp.maximum(a * per - 1, 0), col0 + col_step * rg))
    out = pl.BlockSpec((1, dil * tile, LANES), lambda b, a, rg: (b, a, 0))
    o, lse = pl.pallas_call(
        functools.partial(_window_kernel, dil=dil, n_res=n_res, n_sub=n_sub),
        grid=(B, rows // tile, dil // n_res),
        in_specs=[cur, cur, prev, cur, prev],
        out_specs=[out, out],
        out_shape=[jax.ShapeDtypeStruct((B, S, LANES), F32)] * 2,
        compiler_params=_params("parallel", "arbitrary", "arbitrary"),
        name=f"window{group}",
    )(q, k, k, v, v)
    return o.reshape(B * S, LANES), lse.reshape(B * S, LANES)


MOBA_PAIRS = MOBA_COLS // LANES


def _moba_kernel(*refs):
    q_refs, k_refs, vt_refs = refs[:MOBA_PAIRS], refs[MOBA_PAIRS:2 * MOBA_PAIRS], refs[2 * MOBA_PAIRS:3 * MOBA_PAIRS]
    km_ref, o_ref, s_sc, p_sc = refs[3 * MOBA_PAIRS:]
    i = pl.program_id(1)
    n_blocks = km_ref.shape[1]
    blk = lax.broadcasted_iota(jnp.int32, (n_blocks, TQ), 0)
    own = i * (TQ // MOBA_BLOCK) + lax.broadcasted_iota(jnp.int32, (n_blocks, TQ), 1) // MOBA_BLOCK
    past = blk < own
    never = jnp.full((LANES - n_blocks, TQ), NEG_INF, F32)
    q_aug = []
    for p in range(MOBA_PAIRS):
        q_t = _query_t(q_refs[p][0])
        km = km_ref[0, :, p * LANES:(p + 1) * LANES]
        for hl in range(HEADS_PER_LANE_BLOCK):
            qm = _head_rows(q_t, hl)
            gate = jnp.where(past, _dot(km, qm.astype(BF16)), NEG_INF)
            sel = blk == own
            for _ in range(MOBA_TOPK):
                top = jnp.max(gate, axis=0, keepdims=True)
                idx = jnp.min(jnp.where(gate == top, blk, n_blocks), axis=0, keepdims=True)
                hit = blk == idx
                sel = sel | (hit & past)
                gate = jnp.where(hit, -jnp.inf, gate)
            sel_bias = jnp.where(sel, 0.0, NEG_INF)
            q_aug.append(jnp.concatenate([qm, sel_bias, never], axis=0).astype(BF16))

    row = lax.broadcasted_iota(jnp.int32, (CHUNK, LANES), 0)
    lane = lax.broadcasted_iota(jnp.int32, (CHUNK, LANES), 1)
    blocks_per_chunk = CHUNK // MOBA_BLOCK

    def scores(c):
        off = pl.multiple_of(c * CHUNK, CHUNK)
        onehot = jnp.where(lane == c * blocks_per_chunk + row // MOBA_BLOCK, 1.0, 0.0).astype(BF16)
        out = []
        for p in range(MOBA_PAIRS):
            k_aug = jnp.concatenate([k_refs[p][0, pl.ds(off, CHUNK), :], onehot], axis=1)
            out += [functools.partial(_dot, k_aug, qa)
                    for qa in q_aug[p * HEADS_PER_LANE_BLOCK:(p + 1) * HEADS_PER_LANE_BLOCK]]
        return out

    def values(c, h):
        off = pl.multiple_of(c * CHUNK, CHUNK)
        p, hl = divmod(h, HEADS_PER_LANE_BLOCK)
        return vt_refs[p][0, hl * HEAD_DIM:(hl + 1) * HEAD_DIM, pl.ds(off, CHUNK)]

    o_ref[0] = _finish(_chunked_attention(i, N_HEADS_MOBA, scores, values, s_sc, p_sc)).astype(BF16)


def _moba(q, k, vt, km):
    B, S, _ = q.shape
    n_blocks = S // MOBA_BLOCK
    assert TQ % MOBA_BLOCK == 0 and CHUNK % MOBA_BLOCK == 0 and S % CHUNK == 0
    assert n_blocks <= LANES and n_blocks % (2 * SUBLANES) == 0
    first = (FOX_COLS + DIL_COLS) // LANES
    at = lambda f: [pl.BlockSpec(*f(first + p)) for p in range(MOBA_PAIRS)]
    return pl.pallas_call(
        _moba_kernel,
        grid=(B, S // TQ),
        in_specs=at(lambda c: ((1, TQ, LANES), lambda b, i: (b, i, c)))
        + at(lambda c: ((1, S, LANES), lambda b, i: (b, 0, c)))
        + at(lambda c: ((1, LANES, S), lambda b, i: (b, c, 0)))
        + [pl.BlockSpec((1, n_blocks, MOBA_COLS), lambda b, i: (b, 0, 0))],
        out_specs=pl.BlockSpec((1, TQ, MOBA_COLS), lambda b, i: (b, i, 0)),
        out_shape=jax.ShapeDtypeStruct((B, S, MOBA_COLS), BF16),
        scratch_shapes=_chunk_scratch(N_HEADS_MOBA),
        compiler_params=_params("parallel", "arbitrary"),
        name="moba",
    )(*([q] * MOBA_PAIRS + [k] * MOBA_PAIRS + [vt] * MOBA_PAIRS + [km]))


MXU_TILE = 256
assert D_FF % MXU_TILE == 0
FF_SPLITS = (0, (D_FF // MXU_TILE + 1) // 2 * MXU_TILE, D_FF)


MIX_SPLIT = 2


def _mix_slab(rows, ya_ref, o_refs, l_refs, yc_ref, gates_ref, x_ref, p_ref,
              wa_ref, wb_ref, wc_ref, wo_ref, g_mix_ref,
              g_pre_ref, wgate_ref, wup_ref, wdown_ref, g_post_ref, wple_ref, wpg_ref, g_ple_ref, o_ref):
    lses = [l_ref[rows, :] for l_ref in l_refs]
    top = jnp.maximum(jnp.maximum(lses[0], lses[1]), lses[2])
    wts = [jnp.exp2(t - top) for t in lses]
    den = wts[0] + wts[1] + wts[2]
    yb = (wts[0] * o_refs[0][rows, :] + wts[1] * o_refs[1][rows, :] + wts[2] * o_refs[2][rows, :]) / den
    merged = None
    for c, (y, w_ref) in enumerate(((ya_ref[rows, :], wa_ref), (yb.astype(BF16), wb_ref), (yc_ref[rows, :], wc_ref))):
        branch = _dot(y, w_ref[...])
        yield
        term = gates_ref[rows, c * D_MODEL:(c + 1) * D_MODEL].astype(F32) * branch
        merged = term if merged is None else merged + term
    out = _dot(merged.astype(BF16), wo_ref[...])
    yield
    x = x_ref[rows, :] + _rms(out, g_mix_ref[...])

    hb = _rms(x, g_pre_ref[...]).astype(BF16)
    down = None
    for lo, hi in zip(FF_SPLITS[:-1], FF_SPLITS[1:]):
        cols = slice(lo, hi)
        gate = _dot(hb, wgate_ref[:, cols])
        yield
        up = _dot(hb, wup_ref[:, cols])
        yield
        ff = (gate * jax.nn.sigmoid(gate) * up).astype(BF16)
        part = _dot(ff, wdown_ref[cols, :])
        yield
        down = part if down is None else down + part
    x = x + _rms(down, g_post_ref[...])

    emb = _dot(p_ref[0, rows, :].astype(BF16), wple_ref[...])
    yield
    ple = emb * jax.nn.sigmoid(_dot(x.astype(BF16), wpg_ref[...]))
    yield
    o_ref[rows, :] = x + _rms(ple, g_ple_ref[...])


def _mix_kernel(ya_ref, o0_ref, o1_ref, o2_ref, l0_ref, l1_ref, l2_ref, *rest):
    slab = TM // MIX_SPLIT
    chains = [_mix_slab(pl.ds(n * slab, slab), ya_ref, (o0_ref, o1_ref, o2_ref), (l0_ref, l1_ref, l2_ref), *rest)
              for n in range(MIX_SPLIT)]
    while chains:
        chains = [c for c in chains if next(c, StopIteration) is not StopIteration]


def _mix(ya, ob, lb, yc, gates, x, p, layer, mix_weights, g_mix, g_pre, ffn_weights, g_post, ple_weights, g_ple):
    T, D = x.shape
    tok = lambda w: pl.BlockSpec((TM, w), lambda i: (i, 0))
    vec = _resident((1, D))
    whole = lambda ws: [_resident(w.shape) for w in ws]
    return pl.pallas_call(
        _mix_kernel,
        grid=(T // TM,),
        in_specs=[tok(FOX_COLS)] + [tok(LANES)] * 6 + [tok(MOBA_COLS), tok(N_GATE_COLS), tok(D),
                  pl.BlockSpec((1, TM, PLE_DIM), lambda i: (layer, i, 0))]
        + whole(mix_weights) + [vec, vec] + whole(ffn_weights) + [vec] + whole(ple_weights) + [vec],
        out_specs=tok(D),
        out_shape=jax.ShapeDtypeStruct((T, D), F32),
        compiler_params=_params("parallel"),
        name="mix",
    )(ya, *ob, *lb, yc, gates, x, p, *mix_weights, g_mix, g_pre, *ffn_weights, g_post, *ple_weights, g_ple)


def _rope_tables(seq):
    f32 = np.float32
    inv = (f32(1.0) / (f32(ROPE_THETA) ** (np.arange(0, HEAD_DIM, 2, dtype=f32) / f32(HEAD_DIM)))).astype(f32)
    ang = np.arange(seq, dtype=f32)[:, None] * inv[None, :]
    cos, sin = np.cos(ang).astype(f32), np.sin(ang).astype(f32)
    reps = LANES // HEAD_DIM
    cos_t = np.tile(np.concatenate([cos, cos], axis=1), (1, reps))
    sin_t = np.tile(np.concatenate([-sin, sin], axis=1), (1, reps))
    return jnp.asarray(cos_t), jnp.asarray(sin_t)


def kernel(x, p, g_mix_pre, w_in, b_f, w_br_a, w_br_b, w_br_c, w_out, g_mix_post, g_ffn_pre,
           w_ffn_gate, w_ffn_up, w_ffn_down, g_ffn_post, w_ple, w_ple_gate, g_ple_post):
    B, S, D = x.shape
    T = B * S
    depth = w_in.shape[0]
    cos_t, sin_t = _rope_tables(S)
    row = lambda g: g.reshape(1, -1)
    f0 = 3 * MIX_WIDTH
    for i in range(depth):
        wq = w_in[i, :, :MIX_WIDTH].astype(BF16)
        wk = w_in[i, :, MIX_WIDTH:2 * MIX_WIDTH].astype(BF16)
        wv = w_in[i, :, 2 * MIX_WIDTH:f0].astype(BF16)
        wf = jnp.pad(w_in[i, :, f0:f0 + N_HEADS_FOX], ((0, 0), (0, LANES - N_HEADS_FOX))).astype(BF16)
        bf = jnp.pad(b_f[i], (0, LANES - N_HEADS_FOX)).reshape(1, LANES)
        wg = w_in[i, :, f0 + N_HEADS_FOX:].astype(BF16)

        (q, k, v, vt, kx, gates, kmean), residue_views = _proj(
            x, row(g_mix_pre[i]), wq, wk, wv, wf, bf, wg, cos_t, sin_t)
        km = kmean.reshape(B, S // MOBA_BLOCK, MOBA_COLS).astype(BF16)
        y_a = _fox(q, k, kx, vt)
        win_in = {g: residue_views[3 * n:3 * n + 3] for n, (g, _) in enumerate(STRIDED_GROUPS)}
        win = [_window(*win_in.get(g, (q, k, v)), g) for g in range(len(DIL_PATTERNS))]
        y_c = _moba(q, k, vt, km)

        bf16 = lambda *ws: [w[i].astype(BF16) for w in ws]
        x = _mix(y_a.reshape(T, -1), [o for o, _ in win], [l for _, l in win], y_c.reshape(T, -1),
                 gates.reshape(T, -1), x.reshape(T, D), p.reshape(depth, T, PLE_DIM), i,
                 bf16(w_br_a, w_br_b, w_br_c, w_out), row(g_mix_post[i]),
                 row(g_ffn_pre[i]), bf16(w_ffn_gate, w_ffn_up, w_ffn_down), row(g_ffn_post[i]),
                 bf16(w_ple, w_ple_gate), row(g_ple_post[i])).reshape(B, S, D)
    return x
```

```python
import functools
import math

import numpy as np
import jax
import jax.numpy as jnp
from jax import lax
from jax.experimental import pallas as pl
from jax.experimental.pallas import tpu as pltpu

D_MODEL = 1024
HEAD_DIM = 64
N_HEADS_FOX = 4
DIL_PATTERNS = ((128, 1), (512, 4), (2048, 16))
N_HEADS_MOBA = 6
MIX_WIDTH = 1024
MOBA_BLOCK = 256
MOBA_TOPK = 3
PLE_DIM = 256
D_FF = 2816
ROPE_THETA = 10000.0
RMS_EPS = 1e-6
NEG_INF = -1e30
LOG2E = math.log2(math.e)
Q_SCALE = HEAD_DIM ** -0.5 * LOG2E

LANES = 128
SUBLANES = 8
HEADS_PER_LANE_BLOCK = LANES // HEAD_DIM
N_LANE_BLOCKS = MIX_WIDTH // LANES
FOX_COLS = N_HEADS_FOX * HEAD_DIM
DIL_COLS = len(DIL_PATTERNS) * 2 * HEAD_DIM
MOBA_COLS = N_HEADS_MOBA * HEAD_DIM
N_GATE_COLS = 3 * D_MODEL
FORGET_PIECES = 3

TM = 512
TQ = 512
CHUNK = 512
VMEM_LIMIT = 56 * 1024 * 1024

F32 = jnp.float32
BF16 = jnp.bfloat16


def _dot(a, b):
    return jnp.dot(a, b, preferred_element_type=F32)


def _dot_nt(a, b):
    return lax.dot_general(a, b, (((1,), (1,)), ((), ())), preferred_element_type=F32)


def _dot_tn(a, b):
    return lax.dot_general(a, b, (((0,), (0,)), ((), ())), preferred_element_type=F32)


def _rms(x, g):
    var = jnp.mean(x * x, axis=-1, keepdims=True)
    return x * lax.rsqrt(var + RMS_EPS) * g


def _split3(x):
    hi = x.astype(BF16).astype(F32)
    r1 = x - hi
    mid = r1.astype(BF16).astype(F32)
    lo = (r1 - mid).astype(BF16).astype(F32)
    return hi, mid, lo


def _resident(shape):
    nd = len(shape)
    return pl.BlockSpec(shape, lambda *_: (0,) * nd, pipeline_mode=pl.Buffered(1))


def _params(*sem):
    return pltpu.CompilerParams(dimension_semantics=sem, vmem_limit_bytes=VMEM_LIMIT)


STRIDED_GROUPS = tuple((g, dil) for g, (_, dil) in enumerate(DIL_PATTERNS) if dil > 1)


PROJ_SLAB = MOBA_BLOCK


def _proj_slab(n, carry, x_ref, g_ref, wq_ref, wk_ref, wv_ref, wf_ref, bf_ref, wg_ref, cos_ref, sin_ref,
               q_ref, k_ref, v_ref, vt_ref, kx_ref, gates_ref, kmean_ref, res_refs, perm_sc):
    rows = pl.ds(n * PROJ_SLAB, PROJ_SLAB)

    def scatter_residues(t, which, blk):
        for m, (g, dil) in enumerate(STRIDED_GROUPS):
            if blk == FOX_COLS // LANES + g:
                slot = 3 * m + which
                per = PROJ_SLAB // dil
                perm_sc[slot, rows, :] = t
                for r in range(dil):
                    res_refs[slot][0, n * per:(n + 1) * per, r * LANES:(r + 1) * LANES] = perm_sc[
                        slot, pl.ds(n * PROJ_SLAB + r, per, stride=dil), :].astype(BF16)

    hb = _rms(x_ref[0, rows, :], g_ref[...]).astype(BF16)
    cos = cos_ref[rows, :]
    sin = sin_ref[rows, :]
    lane = lax.broadcasted_iota(jnp.int32, (PROJ_SLAB, LANES), 1)
    first_half = (lane % HEAD_DIM) < (HEAD_DIM // 2)

    def rope(t):
        rot = jnp.where(first_half, pltpu.roll(t, LANES - HEAD_DIM // 2, 1),
                        pltpu.roll(t, HEAD_DIM // 2, 1))
        return t * cos + rot * sin

    first_rope_blk = FOX_COLS // LANES
    first_moba_blk = (FOX_COLS + DIL_COLS) // LANES

    f = _dot(hb, wf_ref[...]) + bf_ref[...]
    ls = jnp.minimum(f, 0.0) - jnp.log1p(jnp.exp(-jnp.abs(f)))
    ls = jnp.where(lane < N_HEADS_FOX, ls, 0.0)
    row = lax.broadcasted_iota(jnp.int32, (PROJ_SLAB, PROJ_SLAB), 0)
    col = lax.broadcasted_iota(jnp.int32, (PROJ_SLAB, PROJ_SLAB), 1)
    tri = jnp.where(col <= row, 1.0, 0.0).astype(BF16)
    ls_hi, ls_mid, ls_lo = _split3(ls)

    q = _dot(hb, wq_ref[...])
    yield
    cs = _dot(tri, ls_hi.astype(BF16)) + _dot(tri, ls_mid.astype(BF16)) + _dot(tri, ls_lo.astype(BF16))
    for c in range(N_LANE_BLOCKS):
        t = q[:, c * LANES:(c + 1) * LANES]
        if c >= first_rope_blk:
            t = rope(t)
        t = t * Q_SCALE
        q_ref[0, rows, c * LANES:(c + 1) * LANES] = t.astype(BF16)
        scatter_residues(t, 0, c)

    k = _dot(hb, wk_ref[...])
    yield
    cs = cs + carry[n]
    carry.append(cs[PROJ_SLAB - 1:PROJ_SLAB, :])
    hi, mid, lo = _split3(-LOG2E * cs)
    pieces = jnp.where(lane < N_HEADS_FOX, hi,
                       jnp.where(lane < 2 * N_HEADS_FOX, pltpu.roll(mid, N_HEADS_FOX, 1),
                                 pltpu.roll(lo, 2 * N_HEADS_FOX, 1)))
    kx_ref[0, rows, :] = pieces.astype(BF16)

    for c in range(N_LANE_BLOCKS):
        t = k[:, c * LANES:(c + 1) * LANES]
        if c >= first_rope_blk:
            t = rope(t)
        k_ref[0, rows, c * LANES:(c + 1) * LANES] = t.astype(BF16)
        scatter_residues(t, 1, c)
        if c >= first_moba_blk:
            cm = c - first_moba_blk
            kmean_ref[0, 0, n:n + 1, cm * LANES:(cm + 1) * LANES] = jnp.mean(t, axis=0, keepdims=True)

    v = _dot(hb, wv_ref[...])
    yield
    v_ref[0, rows, :] = v.astype(BF16)
    for c in range(N_LANE_BLOCKS):
        scatter_residues(v[:, c * LANES:(c + 1) * LANES], 2, c)
    vt_ref[0, :, rows] = v.T.astype(BF16)

    for c in range(N_GATE_COLS // D_MODEL):
        z = _dot(hb, wg_ref[:, c * D_MODEL:(c + 1) * D_MODEL])
        yield
        gates_ref[0, rows, c * D_MODEL:(c + 1) * D_MODEL] = jax.nn.sigmoid(z).astype(BF16)


def _proj_kernel(*refs):
    *io_refs, carry_ref, perm_sc = refs
    n_res = 3 * len(STRIDED_GROUPS)
    main_refs, res_refs = io_refs[:len(io_refs) - n_res], io_refs[len(io_refs) - n_res:]

    @pl.when(pl.program_id(1) == 0)
    def _():
        carry_ref[...] = jnp.zeros_like(carry_ref)

    carry = [carry_ref[...]]
    chains = [_proj_slab(n, carry, *main_refs, res_refs, perm_sc) for n in range(TM // PROJ_SLAB)]
    while chains:
        chains = [c for c in chains if next(c, StopIteration) is not StopIteration]
    carry_ref[...] = carry[-1]


def _proj(x, g, wq, wk, wv, wf, bf, wg, cos_t, sin_t):
    B, S, D = x.shape
    n_t = S // TM
    tok = lambda w: pl.BlockSpec((1, TM, w), lambda b, i: (b, i, 0))
    res_specs, res_shapes = [], []
    for _, dil in STRIDED_GROUPS:
        assert TM % (dil * 2 * SUBLANES) == 0 and S % dil == 0
        res_specs += [pl.BlockSpec((1, TM // dil, dil * LANES), lambda b, i: (b, i, 0))] * 3
        res_shapes += [jax.ShapeDtypeStruct((B, S // dil, dil * LANES), BF16)] * 3
    outs = pl.pallas_call(
        _proj_kernel,
        grid=(B, n_t),
        in_specs=[tok(D), _resident((1, D)), _resident((D, MIX_WIDTH)), _resident((D, MIX_WIDTH)),
                  _resident((D, MIX_WIDTH)),
                  _resident((D, LANES)), _resident((1, LANES)), _resident((D, N_GATE_COLS)),
                  pl.BlockSpec((TM, LANES), lambda b, i: (i, 0)),
                  pl.BlockSpec((TM, LANES), lambda b, i: (i, 0))],
        out_specs=[tok(MIX_WIDTH), tok(MIX_WIDTH), tok(MIX_WIDTH),
                   pl.BlockSpec((1, MIX_WIDTH, TM), lambda b, i: (b, 0, i)),
                   tok(LANES), tok(N_GATE_COLS),
                   pl.BlockSpec((1, 1, TM // MOBA_BLOCK, MOBA_COLS), lambda b, i: (b, i, 0, 0))] + res_specs,
        out_shape=[jax.ShapeDtypeStruct((B, S, MIX_WIDTH), BF16)] * 3
        + [jax.ShapeDtypeStruct((B, MIX_WIDTH, S), BF16),
           jax.ShapeDtypeStruct((B, S, LANES), BF16),
           jax.ShapeDtypeStruct((B, S, N_GATE_COLS), BF16),
           jax.ShapeDtypeStruct((B, n_t, TM // MOBA_BLOCK, MOBA_COLS), F32)] + res_shapes,
        scratch_shapes=[pltpu.VMEM((1, LANES), F32), pltpu.VMEM((len(res_specs), TM, LANES), F32)],
        compiler_params=_params("arbitrary", "arbitrary"),
        name="proj",
    )(x, g, wq, wk, wv, wf, bf, wg, cos_t, sin_t)
    return outs[:7], outs[7:]


def _reduce_keys(x, op, reduce):
    n = x.shape[0]
    while n > SUBLANES and n % (2 * SUBLANES) == 0:
        n //= 2
        x = op(x[:n], x[n:])
    return reduce(x, axis=0, keepdims=True)


def _key_max(s):
    return _reduce_keys(s, jnp.maximum, jnp.max)


def _probabilities(s, m):
    return jnp.exp2((s - m).astype(BF16))


ACC_ROWS = HEAD_DIM + 2 * SUBLANES


def _values_with_ones(vt):
    return jnp.concatenate([vt, jnp.ones((ACC_ROWS - HEAD_DIM, vt.shape[1]), BF16)], axis=0)


def _chunked_attention(i, n_heads, scores, values, s_sc, p_sc):
    heads = range(n_heads)
    n_full = (i * TQ) // CHUNK

    causal = _chunk_causal(n_full * CHUNK, i * TQ)
    s_diag = [score() for score in scores(n_full)]
    s_first = [score() for score in scores(0)]
    m, alpha = [], []
    for hl in heads:
        sd = jnp.where(causal, s_diag[hl], NEG_INF)
        m_diag = _key_max(sd)
        p_sc[hl] = _probabilities(sd, m_diag)
        s_sc[hl] = s_first[hl]
        m.append(jnp.where(n_full > 0, jnp.maximum(m_diag, _key_max(s_first[hl])), m_diag))
        alpha.append(jnp.exp2(m_diag - m[hl]))

    def product(c, p):
        return [_dot(_values_with_ones(values(c, hl)), p[hl]) for hl in heads]

    def before(c):
        return jnp.where(c <= 0, n_full, c - 1)

    def body(c, carry):
        m, alpha, acc = carry
        pv = product(before(c), [p_sc[hl] for hl in heads])
        s_next = [score() for score in scores(c + 1)]
        m_next = tuple(jnp.maximum(m[hl], _key_max(s_next[hl])) for hl in heads)
        alpha_next = tuple(jnp.exp2(m[hl] - m_next[hl]) for hl in heads)
        for hl in heads:
            p_sc[hl] = _probabilities(s_sc[hl], m[hl])
            s_sc[hl] = s_next[hl]
        acc = tuple(alpha[hl] * (acc[hl] + pv[hl]) for hl in heads)
        return m_next, alpha_next, acc

    init = (tuple(m), tuple(alpha), tuple(jnp.zeros((ACC_ROWS, TQ), F32) for _ in heads))
    m, alpha, acc = lax.fori_loop(0, n_full - 1, body, init)

    last = n_full - 1
    pv = product(before(last), [p_sc[hl] for hl in heads])
    m_tail = [jnp.where(n_full > 0, m[hl], -NEG_INF) for hl in heads]
    tail = product(jnp.maximum(last, 0), [_probabilities(s_sc[hl], m_tail[hl]) for hl in heads])
    return [alpha[hl] * (acc[hl] + pv[hl]) + tail[hl] for hl in heads]


def _query_t(q):
    return q.astype(F32).T


def _head_rows(q_t, hl):
    sub = lax.broadcasted_iota(jnp.int32, q_t.shape, 0)
    return jnp.where((sub >= hl * HEAD_DIM) & (sub < (hl + 1) * HEAD_DIM), q_t, 0.0)


def _finish(accs):
    blocks = []
    for p in range(len(accs) // HEADS_PER_LANE_BLOCK):
        pair = accs[p * HEADS_PER_LANE_BLOCK:(p + 1) * HEADS_PER_LANE_BLOCK]
        out_t = jnp.concatenate([acc[:HEAD_DIM] / acc[HEAD_DIM:HEAD_DIM + 1] for acc in pair], axis=0)
        blocks.append(out_t.T)
    return jnp.concatenate(blocks, axis=1)


def _chunk_scratch(n_heads):
    return [pltpu.VMEM((n_heads, CHUNK, TQ), F32), pltpu.VMEM((n_heads, CHUNK, TQ), BF16)]


def _chunk_causal(first_key, first_query):
    kpos = first_key + lax.broadcasted_iota(jnp.int32, (CHUNK, TQ), 0)
    qpos = first_query + lax.broadcasted_iota(jnp.int32, (CHUNK, TQ), 1)
    return kpos <= qpos


def _fox_kernel(q_ref, k_ref, kx_ref, vt_ref, o_ref, s_sc, p_sc):
    i = pl.program_id(1)
    n_pairs = FOX_COLS // LANES
    sub = lax.broadcasted_iota(jnp.int32, (LANES, TQ), 0)
    q_aug = []
    for p in range(n_pairs):
        q_t = _query_t(q_ref[0, :, p * LANES:(p + 1) * LANES])
        for hl in range(HEADS_PER_LANE_BLOCK):
            head = HEADS_PER_LANE_BLOCK * p + hl
            ones = jnp.where((sub % N_HEADS_FOX == head) & (sub < FORGET_PIECES * N_HEADS_FOX), 1.0, 0.0)
            q_aug.append(jnp.concatenate([_head_rows(q_t, hl), ones], axis=0).astype(BF16))

    def scores(c):
        off = pl.multiple_of(c * CHUNK, CHUNK)
        kx = kx_ref[0, pl.ds(off, CHUNK), :]
        out = []
        for p in range(n_pairs):
            k_aug = jnp.concatenate([k_ref[0, pl.ds(off, CHUNK), p * LANES:(p + 1) * LANES], kx], axis=1)
            out += [functools.partial(_dot, k_aug, qa)
                    for qa in q_aug[p * HEADS_PER_LANE_BLOCK:(p + 1) * HEADS_PER_LANE_BLOCK]]
        return out

    def values(c, h):
        off = pl.multiple_of(c * CHUNK, CHUNK)
        return vt_ref[0, h * HEAD_DIM:(h + 1) * HEAD_DIM, pl.ds(off, CHUNK)]

    o_ref[0] = _finish(_chunked_attention(i, N_HEADS_FOX, scores, values, s_sc, p_sc)).astype(BF16)


def _fox(q, k, kx, vt):
    B, S, _ = q.shape
    assert S % CHUNK == 0
    return pl.pallas_call(
        _fox_kernel,
        grid=(B, S // TQ),
        in_specs=[pl.BlockSpec((1, TQ, FOX_COLS), lambda b, i: (b, i, 0)),
                  pl.BlockSpec((1, S, FOX_COLS), lambda b, i: (b, 0, 0)),
                  pl.BlockSpec((1, S, LANES), lambda b, i: (b, 0, 0)),
                  pl.BlockSpec((1, FOX_COLS, S), lambda b, i: (b, 0, 0))],
        out_specs=pl.BlockSpec((1, TQ, FOX_COLS), lambda b, i: (b, i, 0)),
        out_shape=jax.ShapeDtypeStruct((B, S, FOX_COLS), BF16),
        scratch_shapes=_chunk_scratch(N_HEADS_FOX),
        compiler_params=_params("parallel", "arbitrary"),
        name="fox",
    )(q, k, kx, vt)


WINDOW_KEYS = DIL_PATTERNS[0][0] // DIL_PATTERNS[0][1]
WQ = 256
assert all(w // d == WINDOW_KEYS for w, d in DIL_PATTERNS) and WINDOW_KEYS <= WQ


WINDOW_CHAINS = 8


def _window_kernel(q_ref, kc_ref, kp_ref, vc_ref, vp_ref, o_ref, lse_ref, *, dil, n_res, n_sub):
    a = pl.program_id(1)
    rg = pl.program_id(2)
    kr = lax.broadcasted_iota(jnp.int32, (WQ, WQ), 0)
    qc = lax.broadcasted_iota(jnp.int32, (WQ, WQ), 1)
    valid_d = (kr <= qc) & (qc - kr <= WINDOW_KEYS)
    krp = lax.broadcasted_iota(jnp.int32, (WINDOW_KEYS, WQ), 0)
    qcp = lax.broadcasted_iota(jnp.int32, (WINDOW_KEYS, WQ), 1)
    valid_p = krp >= qcp
    valid_first = valid_p & (a > 0)

    chains = [(rr, t) for rr in range(n_res) for t in range(n_sub)]
    heads = range(HEADS_PER_LANE_BLOCK)
    raw, vals = [], []
    for rr, t in chains:
        cols = slice(rr * LANES, (rr + 1) * LANES)
        sub = slice(t * WQ, (t + 1) * WQ)
        q_t = _query_t(q_ref[0, sub, cols])
        if t == 0:
            kp, vp = kp_ref[0, :, cols], vp_ref[0, :, cols]
        else:
            before = slice(t * WQ - WINDOW_KEYS, t * WQ)
            kp, vp = kc_ref[0, before, cols], vc_ref[0, before, cols]
        kd = kc_ref[0, sub, cols]
        vals.append((vc_ref[0, sub, cols], vp))
        q_m = [_head_rows(q_t, hl).astype(BF16) for hl in heads]
        raw.append([(_dot(kd, qm), _dot(kp, qm)) for qm in q_m])

    probs, stats = [], []
    for n, (rr, t) in enumerate(chains):
        for hl in heads:
            s_d = jnp.where(valid_d, raw[n][hl][0], NEG_INF)
            s_p = jnp.where(valid_first if t == 0 else valid_p, raw[n][hl][1], NEG_INF)
            m = jnp.maximum(_key_max(s_d), _key_max(s_p))
            p_d = jnp.exp2(s_d - m)
            p_p = jnp.exp2(s_p - m)
            l = _reduce_keys(p_d, jnp.add, jnp.sum) + _reduce_keys(p_p, jnp.add, jnp.sum)
            probs.append((p_d.astype(BF16), p_p.astype(BF16)))
            stats.append((m, l))

    accs = [_dot_tn(vals[n][0], probs[n * len(heads) + hl][0]) + _dot_tn(vals[n][1], probs[n * len(heads) + hl][1])
            for n in range(len(chains)) for hl in heads]

    for n, (rr, t) in enumerate(chains):
        outs, lses = [], []
        for hl in heads:
            m, l = stats[n * len(heads) + hl]
            outs.append(accs[n * len(heads) + hl][hl * HEAD_DIM:(hl + 1) * HEAD_DIM] / l)
            lses.append(jnp.broadcast_to(m + jnp.log2(l), (HEAD_DIM, WQ)))
        r = rg * n_res + rr
        rows = pl.ds(t * WQ * dil + r, WQ, stride=dil) if dil > 1 else pl.ds(t * WQ, WQ)
        o_ref[0, rows, :] = jnp.concatenate(outs, axis=0).T
        lse_ref[0, rows, :] = jnp.concatenate(lses, axis=0).T


def _window(q, k, v, group):
    dil = DIL_PATTERNS[group][1]
    B, rows, width = q.shape
    S = rows * dil
    n_sub = min(WINDOW_CHAINS, rows // WQ)
    n_res = min(dil, WINDOW_CHAINS // n_sub)
    tile = n_sub * WQ
    assert rows % tile == 0 and tile % WINDOW_KEYS == 0 and dil % n_res == 0
    col0, col_step = (FOX_COLS // LANES + group, 0) if dil == 1 else (0, 1)
    assert width == (MIX_WIDTH if dil == 1 else dil * LANES)
    cur = pl.BlockSpec((1, tile, n_res * LANES), lambda b, a, rg: (b, a, col0 + col_step * rg))
    per = tile // WINDOW_KEYS
    prev = pl.BlockSpec((1, WINDOW_KEYS, n_res * LANES),
                        lambda b, a, rg: (b, jnp.maximum(a * per - 1, 0), col0 + col_step * rg))
    out = pl.BlockSpec((1, dil * tile, LANES), lambda b, a, rg: (b, a, 0))
    o, lse = pl.pallas_call(
        functools.partial(_window_kernel, dil=dil, n_res=n_res, n_sub=n_sub),
        grid=(B, rows // tile, dil // n_res),
        in_specs=[cur, cur, prev, cur, prev],
        out_specs=[out, out],
        out_shape=[jax.ShapeDtypeStruct((B, S, LANES), F32)] * 2,
        compiler_params=_params("parallel", "arbitrary", "arbitrary"),
        name=f"window{group}",
    )(q, k, k, v, v)
    return o.reshape(B * S, LANES), lse.reshape(B * S, LANES)


MOBA_PAIRS = MOBA_COLS // LANES


def _moba_kernel(*refs):
    q_refs, k_refs, vt_refs = refs[:MOBA_PAIRS], refs[MOBA_PAIRS:2 * MOBA_PAIRS], refs[2 * MOBA_PAIRS:3 * MOBA_PAIRS]
    km_ref, o_ref, s_sc, p_sc = refs[3 * MOBA_PAIRS:]
    i = pl.program_id(1)
    n_blocks = km_ref.shape[1]
    blk = lax.broadcasted_iota(jnp.int32, (n_blocks, TQ), 0)
    own = i * (TQ // MOBA_BLOCK) + lax.broadcasted_iota(jnp.int32, (n_blocks, TQ), 1) // MOBA_BLOCK
    past = blk < own
    never = jnp.full((LANES - n_blocks, TQ), NEG_INF, F32)
    q_aug = []
    for p in range(MOBA_PAIRS):
        q_t = _query_t(q_refs[p][0])
        km = km_ref[0, :, p * LANES:(p + 1) * LANES]
        for hl in range(HEADS_PER_LANE_BLOCK):
            qm = _head_rows(q_t, hl)
            gate = jnp.where(past, _dot(km, qm.astype(BF16)), NEG_INF)
            sel = blk == own
            for _ in range(MOBA_TOPK):
                top = jnp.max(gate, axis=0, keepdims=True)
                idx = jnp.min(jnp.where(gate == top, blk, n_blocks), axis=0, keepdims=True)
                hit = blk == idx
                sel = sel | (hit & past)
                gate = jnp.where(hit, -jnp.inf, gate)
            sel_bias = jnp.where(sel, 0.0, NEG_INF)
            q_aug.append(jnp.concatenate([qm, sel_bias, never], axis=0).astype(BF16))

    row = lax.broadcasted_iota(jnp.int32, (CHUNK, LANES), 0)
    lane = lax.broadcasted_iota(jnp.int32, (CHUNK, LANES), 1)
    blocks_per_chunk = CHUNK // MOBA_BLOCK

    def scores(c):
        off = pl.multiple_of(c * CHUNK, CHUNK)
        onehot = jnp.where(lane == c * blocks_per_chunk + row // MOBA_BLOCK, 1.0, 0.0).astype(BF16)
        out = []
        for p in range(MOBA_PAIRS):
            k_aug = jnp.concatenate([k_refs[p][0, pl.ds(off, CHUNK), :], onehot], axis=1)
            out += [functools.partial(_dot, k_aug, qa)
                    for qa in q_aug[p * HEADS_PER_LANE_BLOCK:(p + 1) * HEADS_PER_LANE_BLOCK]]
        return out

    def values(c, h):
        off = pl.multiple_of(c * CHUNK, CHUNK)
        p, hl = divmod(h, HEADS_PER_LANE_BLOCK)
        return vt_refs[p][0, hl * HEAD_DIM:(hl + 1) * HEAD_DIM, pl.ds(off, CHUNK)]

    o_ref[0] = _finish(_chunked_attention(i, N_HEADS_MOBA, scores, values, s_sc, p_sc)).astype(BF16)


def _moba(q, k, vt, km):
    B, S, _ = q.shape
    n_blocks = S // MOBA_BLOCK
    assert TQ % MOBA_BLOCK == 0 and CHUNK % MOBA_BLOCK == 0 and S % CHUNK == 0
    assert n_blocks <= LANES and n_blocks % (2 * SUBLANES) == 0
    first = (FOX_COLS + DIL_COLS) // LANES
    at = lambda f: [pl.BlockSpec(*f(first + p)) for p in range(MOBA_PAIRS)]
    return pl.pallas_call(
        _moba_kernel,
        grid=(B, S // TQ),
        in_specs=at(lambda c: ((1, TQ, LANES), lambda b, i: (b, i, c)))
        + at(lambda c: ((1, S, LANES), lambda b, i: (b, 0, c)))
        + at(lambda c: ((1, LANES, S), lambda b, i: (b, c, 0)))
        + [pl.BlockSpec((1, n_blocks, MOBA_COLS), lambda b, i: (b, 0, 0))],
        out_specs=pl.BlockSpec((1, TQ, MOBA_COLS), lambda b, i: (b, i, 0)),
        out_shape=jax.ShapeDtypeStruct((B, S, MOBA_COLS), BF16),
        scratch_shapes=_chunk_scratch(N_HEADS_MOBA),
        compiler_params=_params("parallel", "arbitrary"),
        name="moba",
    )(*([q] * MOBA_PAIRS + [k] * MOBA_PAIRS + [vt] * MOBA_PAIRS + [km]))


MXU_TILE = 256
assert D_FF % MXU_TILE == 0
FF_SPLITS = (0, (D_FF // MXU_TILE + 1) // 2 * MXU_TILE, D_FF)


MIX_SPLIT = 2


def _mix_slab(rows, ya_ref, o_refs, l_refs, yc_ref, gates_ref, x_ref, p_ref,
              wa_ref, wb_ref, wc_ref, wo_ref, g_mix_ref,
              g_pre_ref, wgate_ref, wup_ref, wdown_ref, g_post_ref, wple_ref, wpg_ref, g_ple_ref, o_ref):
    lses = [l_ref[rows, :] for l_ref in l_refs]
    top = jnp.maximum(jnp.maximum(lses[0], lses[1]), lses[2])
    wts = [jnp.exp2(t - top) for t in lses]
    den = wts[0] + wts[1] + wts[2]
    yb = (wts[0] * o_refs[0][rows, :] + wts[1] * o_refs[1][rows, :] + wts[2] * o_refs[2][rows, :]) / den
    merged = None
    for c, (y, w_ref) in enumerate(((ya_ref[rows, :], wa_ref), (yb.astype(BF16), wb_ref), (yc_ref[rows, :], wc_ref))):
        branch = _dot(y, w_ref[...])
        yield
        term = gates_ref[rows, c * D_MODEL:(c + 1) * D_MODEL].astype(F32) * branch
        merged = term if merged is None else merged + term
    out = _dot(merged.astype(BF16), wo_ref[...])
    yield
    x = x_ref[rows, :] + _rms(out, g_mix_ref[...])

    hb = _rms(x, g_pre_ref[...]).astype(BF16)
    down = None
    for lo, hi in zip(FF_SPLITS[:-1], FF_SPLITS[1:]):
        cols = slice(lo, hi)
        gate = _dot(hb, wgate_ref[:, cols])
        yield
        up = _dot(hb, wup_ref[:, cols])
        yield
        ff = (gate * jax.nn.sigmoid(gate) * up).astype(BF16)
        part = _dot(ff, wdown_ref[cols, :])
        yield
        down = part if down is None else down + part
    x = x + _rms(down, g_post_ref[...])

    emb = _dot(p_ref[0, rows, :].astype(BF16), wple_ref[...])
    yield
    ple = emb * jax.nn.sigmoid(_dot(x.astype(BF16), wpg_ref[...]))
    yield
    o_ref[rows, :] = x + _rms(ple, g_ple_ref[...])


def _mix_kernel(ya_ref, o0_ref, o1_ref, o2_ref, l0_ref, l1_ref, l2_ref, *rest):
    slab = TM // MIX_SPLIT
    chains = [_mix_slab(pl.ds(n * slab, slab), ya_ref, (o0_ref, o1_ref, o2_ref), (l0_ref, l1_ref, l2_ref), *rest)
              for n in range(MIX_SPLIT)]
    while chains:
        chains = [c for c in chains if next(c, StopIteration) is not StopIteration]


def _mix(ya, ob, lb, yc, gates, x, p, layer, mix_weights, g_mix, g_pre, ffn_weights, g_post, ple_weights, g_ple):
    T, D = x.shape
    tok = lambda w: pl.BlockSpec((TM, w), lambda i: (i, 0))
    vec = _resident((1, D))
    whole = lambda ws: [_resident(w.shape) for w in ws]
    return pl.pallas_call(
        _mix_kernel,
        grid=(T // TM,),
        in_specs=[tok(FOX_COLS)] + [tok(LANES)] * 6 + [tok(MOBA_COLS), tok(N_GATE_COLS), tok(D),
                  pl.BlockSpec((1, TM, PLE_DIM), lambda i: (layer, i, 0))]
        + whole(mix_weights) + [vec, vec] + whole(ffn_weights) + [vec] + whole(ple_weights) + [vec],
        out_specs=tok(D),
        out_shape=jax.ShapeDtypeStruct((T, D), F32),
        compiler_params=_params("parallel"),
        name="mix",
    )(ya, *ob, *lb, yc, gates, x, p, *mix_weights, g_mix, g_pre, *ffn_weights, g_post, *ple_weights, g_ple)


def _rope_tables(seq):
    f32 = np.float32
    inv = (f32(1.0) / (f32(ROPE_THETA) ** (np.arange(0, HEAD_DIM, 2, dtype=f32) / f32(HEAD_DIM)))).astype(f32)
    ang = np.arange(seq, dtype=f32)[:, None] * inv[None, :]
    cos, sin = np.cos(ang).astype(f32), np.sin(ang).astype(f32)
    reps = LANES // HEAD_DIM
    cos_t = np.tile(np.concatenate([cos, cos], axis=1), (1, reps))
    sin_t = np.tile(np.concatenate([-sin, sin], axis=1), (1, reps))
    return jnp.asarray(cos_t), jnp.asarray(sin_t)


def kernel(x, p, g_mix_pre, w_in, b_f, w_br_a, w_br_b, w_br_c, w_out, g_mix_post, g_ffn_pre,
           w_ffn_gate, w_ffn_up, w_ffn_down, g_ffn_post, w_ple, w_ple_gate, g_ple_post):
    B, S, D = x.shape
    T = B * S
    depth = w_in.shape[0]
    cos_t, sin_t = _rope_tables(S)
    row = lambda g: g.reshape(1, -1)
    f0 = 3 * MIX_WIDTH
    for i in range(depth):
        wq = w_in[i, :, :MIX_WIDTH].astype(BF16)
        wk = w_in[i, :, MIX_WIDTH:2 * MIX_WIDTH].astype(BF16)
        wv = w_in[i, :, 2 * MIX_WIDTH:f0].astype(BF16)
        wf = jnp.pad(w_in[i, :, f0:f0 + N_HEADS_FOX], ((0, 0), (0, LANES - N_HEADS_FOX))).astype(BF16)
        bf = jnp.pad(b_f[i], (0, LANES - N_HEADS_FOX)).reshape(1, LANES)
        wg = w_in[i, :, f0 + N_HEADS_FOX:].astype(BF16)

        (q, k, v, vt, kx, gates, kmean), residue_views = _proj(
            x, row(g_mix_pre[i]), wq, wk, wv, wf, bf, wg, cos_t, sin_t)
        km = kmean.reshape(B, S // MOBA_BLOCK, MOBA_COLS).astype(BF16)
        y_a = _fox(q, k, kx, vt)
        win_in = {g: residue_views[3 * n:3 * n + 3] for n, (g, _) in enumerate(STRIDED_GROUPS)}
        win = [_window(*win_in.get(g, (q, k, v)), g) for g in range(len(DIL_PATTERNS))]
        y_c = _moba(q, k, vt, km)

        bf16 = lambda *ws: [w[i].astype(BF16) for w in ws]
        x = _mix(y_a.reshape(T, -1), [o for o, _ in win], [l for _, l in win], y_c.reshape(T, -1),
                 gates.reshape(T, -1), x.reshape(T, D), p.reshape(depth, T, PLE_DIM), i,
                 bf16(w_br_a, w_br_b, w_br_c, w_out), row(g_mix_post[i]),
                 row(g_ffn_pre[i]), bf16(w_ffn_gate, w_ffn_up, w_ffn_down), row(g_ffn_post[i]),
                 bf16(w_ple, w_ple_gate), row(g_ple_post[i])).reshape(B, S, D)
    return x
```

```python
import functools
import math

import numpy as np
import jax
import jax.numpy as jnp
from jax import lax
from jax.experimental import pallas as pl
from jax.experimental.pallas import tpu as pltpu

D_MODEL = 1024
HEAD_DIM = 64
N_HEADS_FOX = 4
DIL_PATTERNS = ((128, 1), (512, 4), (2048, 16))
N_HEADS_MOBA = 6
MIX_WIDTH = 1024
MOBA_BLOCK = 256
MOBA_TOPK = 3
PLE_DIM = 256
D_FF = 2816
ROPE_THETA = 10000.0
RMS_EPS = 1e-6
NEG_INF = -1e30
LOG2E = math.log2(math.e)
Q_SCALE = HEAD_DIM ** -0.5 * LOG2E

LANES = 128
SUBLANES = 8
HEADS_PER_LANE_BLOCK = LANES // HEAD_DIM
N_LANE_BLOCKS = MIX_WIDTH // LANES
FOX_COLS = N_HEADS_FOX * HEAD_DIM
DIL_COLS = len(DIL_PATTERNS) * 2 * HEAD_DIM
MOBA_COLS = N_HEADS_MOBA * HEAD_DIM
N_GATE_COLS = 3 * D_MODEL
FORGET_PIECES = 3

TM = 512
TQ = 512
CHUNK = 512
VMEM_LIMIT = 56 * 1024 * 1024

F32 = jnp.float32
BF16 = jnp.bfloat16


def _dot(a, b):
    return jnp.dot(a, b, preferred_element_type=F32)


def _dot_tn(a, b):
    return lax.dot_general(a, b, (((0,), (0,)), ((), ())), preferred_element_type=F32)


def _rms(x, g):
    var = jnp.mean(x * x, axis=-1, keepdims=True)
    return x * lax.rsqrt(var + RMS_EPS) * g


def _split3(x):
    hi = x.astype(BF16).astype(F32)
    r1 = x - hi
    mid = r1.astype(BF16).astype(F32)
    lo = (r1 - mid).astype(BF16).astype(F32)
    return hi, mid, lo


def _resident(shape):
    nd = len(shape)
    return pl.BlockSpec(shape, lambda *_: (0,) * nd, pipeline_mode=pl.Buffered(1))


def _params(*sem):
    return pltpu.CompilerParams(dimension_semantics=sem, vmem_limit_bytes=VMEM_LIMIT)


STRIDED_GROUPS = tuple((g, dil) for g, (_, dil) in enumerate(DIL_PATTERNS) if dil > 1)


PROJ_SLAB = MOBA_BLOCK


def _proj_slab(n, carry, x_ref, g_ref, wq_ref, wk_ref, wv_ref, wf_ref, bf_ref, wg_ref, cos_ref, sin_ref,
               q_ref, k_ref, v_ref, vt_ref, kx_ref, gates_ref, kmean_ref, res_refs, perm_sc):
    rows = pl.ds(n * PROJ_SLAB, PROJ_SLAB)

    def scatter_residues(t, which, blk):
        for m, (g, dil) in enumerate(STRIDED_GROUPS):
            if blk == FOX_COLS // LANES + g:
                slot = 3 * m + which
                per = PROJ_SLAB // dil
                perm_sc[slot, rows, :] = t
                for r in range(dil):
                    res_refs[slot][0, n * per:(n + 1) * per, r * LANES:(r + 1) * LANES] = perm_sc[
                        slot, pl.ds(n * PROJ_SLAB + r, per, stride=dil), :].astype(BF16)

    hb = _rms(x_ref[0, rows, :], g_ref[...]).astype(BF16)
    cos = cos_ref[rows, :]
    sin = sin_ref[rows, :]
    lane = lax.broadcasted_iota(jnp.int32, (PROJ_SLAB, LANES), 1)
    first_half = (lane % HEAD_DIM) < (HEAD_DIM // 2)

    def rope(t):
        rot = jnp.where(first_half, pltpu.roll(t, LANES - HEAD_DIM // 2, 1),
                        pltpu.roll(t, HEAD_DIM // 2, 1))
        return t * cos + rot * sin

    first_rope_blk = FOX_COLS // LANES
    first_moba_blk = (FOX_COLS + DIL_COLS) // LANES

    f = _dot(hb, wf_ref[...]) + bf_ref[...]
    ls = jnp.minimum(f, 0.0) - jnp.log1p(jnp.exp(-jnp.abs(f)))
    ls = jnp.where(lane < N_HEADS_FOX, ls, 0.0)
    row = lax.broadcasted_iota(jnp.int32, (PROJ_SLAB, PROJ_SLAB), 0)
    col = lax.broadcasted_iota(jnp.int32, (PROJ_SLAB, PROJ_SLAB), 1)
    tri = jnp.where(col <= row, 1.0, 0.0).astype(BF16)
    ls_hi, ls_mid, ls_lo = _split3(ls)

    q = _dot(hb, wq_ref[...])
    yield
    cs = _dot(tri, ls_hi.astype(BF16)) + _dot(tri, ls_mid.astype(BF16)) + _dot(tri, ls_lo.astype(BF16))
    for c in range(N_LANE_BLOCKS):
        t = q[:, c * LANES:(c + 1) * LANES]
        if c >= first_rope_blk:
            t = rope(t)
        t = t * Q_SCALE
        q_ref[0, rows, c * LANES:(c + 1) * LANES] = t.astype(BF16)
        scatter_residues(t, 0, c)

    k = _dot(hb, wk_ref[...])
    yield
    cs = cs + carry[n]
    carry.append(cs[PROJ_SLAB - 1:PROJ_SLAB, :])
    hi, mid, lo = _split3(-LOG2E * cs)
    pieces = jnp.where(lane < N_HEADS_FOX, hi,
                       jnp.where(lane < 2 * N_HEADS_FOX, pltpu.roll(mid, N_HEADS_FOX, 1),
                                 pltpu.roll(lo, 2 * N_HEADS_FOX, 1)))
    kx_ref[0, rows, :] = pieces.astype(BF16)

    for c in range(N_LANE_BLOCKS):
        t = k[:, c * LANES:(c + 1) * LANES]
        if c >= first_rope_blk:
            t = rope(t)
        k_ref[0, rows, c * LANES:(c + 1) * LANES] = t.astype(BF16)
        scatter_residues(t, 1, c)
        if c >= first_moba_blk:
            cm = c - first_moba_blk
            kmean_ref[0, 0, n:n + 1, cm * LANES:(cm + 1) * LANES] = jnp.mean(t, axis=0, keepdims=True)

    v = _dot(hb, wv_ref[...])
    yield
    v_ref[0, rows, :] = v.astype(BF16)
    for c in range(N_LANE_BLOCKS):
        scatter_residues(v[:, c * LANES:(c + 1) * LANES], 2, c)
    vt_ref[0, :, rows] = v.T.astype(BF16)

    for c in range(N_GATE_COLS // D_MODEL):
        z = _dot(hb, wg_ref[:, c * D_MODEL:(c + 1) * D_MODEL])
        yield
        gates_ref[0, rows, c * D_MODEL:(c + 1) * D_MODEL] = jax.nn.sigmoid(z).astype(BF16)


def _proj_kernel(*refs):
    *io_refs, carry_ref, perm_sc = refs
    n_res = 3 * len(STRIDED_GROUPS)
    main_refs, res_refs = io_refs[:len(io_refs) - n_res], io_refs[len(io_refs) - n_res:]

    @pl.when(pl.program_id(1) == 0)
    def _():
        carry_ref[...] = jnp.zeros_like(carry_ref)

    carry = [carry_ref[...]]
    chains = [_proj_slab(n, carry, *main_refs, res_refs, perm_sc) for n in range(TM // PROJ_SLAB)]
    while chains:
        chains = [c for c in chains if next(c, StopIteration) is not StopIteration]
    carry_ref[...] = carry[-1]


def _proj(x, g, wq, wk, wv, wf, bf, wg, cos_t, sin_t):
    B, S, D = x.shape
    n_t = S // TM
    tok = lambda w: pl.BlockSpec((1, TM, w), lambda b, i: (b, i, 0))
    res_specs, res_shapes = [], []
    for _, dil in STRIDED_GROUPS:
        assert TM % (dil * 2 * SUBLANES) == 0 and S % dil == 0
        res_specs += [pl.BlockSpec((1, TM // dil, dil * LANES), lambda b, i: (b, i, 0))] * 3
        res_shapes += [jax.ShapeDtypeStruct((B, S // dil, dil * LANES), BF16)] * 3
    outs = pl.pallas_call(
        _proj_kernel,
        grid=(B, n_t),
        in_specs=[tok(D), _resident((1, D)), _resident((D, MIX_WIDTH)), _resident((D, MIX_WIDTH)),
                  _resident((D, MIX_WIDTH)),
                  _resident((D, LANES)), _resident((1, LANES)), _resident((D, N_GATE_COLS)),
                  pl.BlockSpec((TM, LANES), lambda b, i: (i, 0)),
                  pl.BlockSpec((TM, LANES), lambda b, i: (i, 0))],
        out_specs=[tok(MIX_WIDTH), tok(MIX_WIDTH), tok(MIX_WIDTH),
                   pl.BlockSpec((1, MIX_WIDTH, TM), lambda b, i: (b, 0, i)),
                   tok(LANES), tok(N_GATE_COLS),
                   pl.BlockSpec((1, 1, TM // MOBA_BLOCK, MOBA_COLS), lambda b, i: (b, i, 0, 0))] + res_specs,
        out_shape=[jax.ShapeDtypeStruct((B, S, MIX_WIDTH), BF16)] * 3
        + [jax.ShapeDtypeStruct((B, MIX_WIDTH, S), BF16),
           jax.ShapeDtypeStruct((B, S, LANES), BF16),
           jax.ShapeDtypeStruct((B, S, N_GATE_COLS), BF16),
           jax.ShapeDtypeStruct((B, n_t, TM // MOBA_BLOCK, MOBA_COLS), F32)] + res_shapes,
        scratch_shapes=[pltpu.VMEM((1, LANES), F32), pltpu.VMEM((len(res_specs), TM, LANES), F32)],
        compiler_params=_params("arbitrary", "arbitrary"),
        name="proj",
    )(x, g, wq, wk, wv, wf, bf, wg, cos_t, sin_t)
    return outs[:7], outs[7:]


def _reduce_keys(x, op, reduce):
    n = x.shape[0]
    while n > SUBLANES and n % (2 * SUBLANES) == 0:
        n //= 2
        x = op(x[:n], x[n:])
    return reduce(x, axis=0, keepdims=True)


def _key_max(s):
    return _reduce_keys(s, jnp.maximum, jnp.max)


def _probabilities(s, m):
    return jnp.exp2((s - m).astype(BF16))


ACC_ROWS = HEAD_DIM + 2 * SUBLANES


def _values_with_ones(vt):
    return jnp.concatenate([vt, jnp.ones((ACC_ROWS - HEAD_DIM, vt.shape[1]), BF16)], axis=0)


def _chunked_attention(i, n_heads, scores, values, s_sc, p_sc):
    heads = range(n_heads)
    n_full = (i * TQ) // CHUNK

    causal = _chunk_causal(n_full * CHUNK, i * TQ)
    s_diag = [score() for score in scores(n_full)]
    s_first = [score() for score in scores(0)]
    m, alpha = [], []
    for hl in heads:
        sd = jnp.where(causal, s_diag[hl], NEG_INF)
        m_diag = _key_max(sd)
        p_sc[hl] = _probabilities(sd, m_diag)
        s_sc[hl] = s_first[hl]
        m.append(jnp.where(n_full > 0, jnp.maximum(m_diag, _key_max(s_first[hl])), m_diag))
        alpha.append(jnp.exp2(m_diag - m[hl]))

    def product(c, p):
        return [_dot(_values_with_ones(values(c, hl)), p[hl]) for hl in heads]

    def before(c):
        return jnp.where(c <= 0, n_full, c - 1)

    def body(c, carry):
        m, alpha, acc = carry
        pv = product(before(c), [p_sc[hl] for hl in heads])
        s_next = [score() for score in scores(c + 1)]
        m_next = tuple(jnp.maximum(m[hl], _key_max(s_next[hl])) for hl in heads)
        alpha_next = tuple(jnp.exp2(m[hl] - m_next[hl]) for hl in heads)
        for hl in heads:
            p_sc[hl] = _probabilities(s_sc[hl], m[hl])
            s_sc[hl] = s_next[hl]
        acc = tuple(alpha[hl] * (acc[hl] + pv[hl]) for hl in heads)
        return m_next, alpha_next, acc

    init = (tuple(m), tuple(alpha), tuple(jnp.zeros((ACC_ROWS, TQ), F32) for _ in heads))
    m, alpha, acc = lax.fori_loop(0, n_full - 1, body, init)

    last = n_full - 1
    pv = product(before(last), [p_sc[hl] for hl in heads])
    m_tail = [jnp.where(n_full > 0, m[hl], -NEG_INF) for hl in heads]
    tail = product(jnp.maximum(last, 0), [_probabilities(s_sc[hl], m_tail[hl]) for hl in heads])
    return [alpha[hl] * (acc[hl] + pv[hl]) + tail[hl] for hl in heads]


def _query_t(q):
    return q.astype(F32).T


def _head_rows(q_t, hl):
    sub = lax.broadcasted_iota(jnp.int32, q_t.shape, 0)
    return jnp.where((sub >= hl * HEAD_DIM) & (sub < (hl + 1) * HEAD_DIM), q_t, 0.0)


def _finish(accs):
    blocks = []
    for p in range(len(accs) // HEADS_PER_LANE_BLOCK):
        pair = accs[p * HEADS_PER_LANE_BLOCK:(p + 1) * HEADS_PER_LANE_BLOCK]
        out_t = jnp.concatenate([acc[:HEAD_DIM] / acc[HEAD_DIM:HEAD_DIM + 1] for acc in pair], axis=0)
        blocks.append(out_t.T)
    return jnp.concatenate(blocks, axis=1)


def _chunk_scratch(n_heads):
    return [pltpu.VMEM((n_heads, CHUNK, TQ), F32), pltpu.VMEM((n_heads, CHUNK, TQ), BF16)]


def _chunk_causal(first_key, first_query):
    kpos = first_key + lax.broadcasted_iota(jnp.int32, (CHUNK, TQ), 0)
    qpos = first_query + lax.broadcasted_iota(jnp.int32, (CHUNK, TQ), 1)
    return kpos <= qpos


def _fox_kernel(q_ref, k_ref, kx_ref, vt_ref, o_ref, s_sc, p_sc):
    i = pl.program_id(1)
    n_pairs = FOX_COLS // LANES
    sub = lax.broadcasted_iota(jnp.int32, (LANES, TQ), 0)
    q_aug = []
    for p in range(n_pairs):
        q_t = _query_t(q_ref[0, :, p * LANES:(p + 1) * LANES])
        for hl in range(HEADS_PER_LANE_BLOCK):
            head = HEADS_PER_LANE_BLOCK * p + hl
            ones = jnp.where((sub % N_HEADS_FOX == head) & (sub < FORGET_PIECES * N_HEADS_FOX), 1.0, 0.0)
            q_aug.append(jnp.concatenate([_head_rows(q_t, hl), ones], axis=0).astype(BF16))

    def scores(c):
        off = pl.multiple_of(c * CHUNK, CHUNK)
        kx = kx_ref[0, pl.ds(off, CHUNK), :]
        out = []
        for p in range(n_pairs):
            k_aug = jnp.concatenate([k_ref[0, pl.ds(off, CHUNK), p * LANES:(p + 1) * LANES], kx], axis=1)
            out += [functools.partial(_dot, k_aug, qa)
                    for qa in q_aug[p * HEADS_PER_LANE_BLOCK:(p + 1) * HEADS_PER_LANE_BLOCK]]
        return out

    def values(c, h):
        off = pl.multiple_of(c * CHUNK, CHUNK)
        return vt_ref[0, h * HEAD_DIM:(h + 1) * HEAD_DIM, pl.ds(off, CHUNK)]

    o_ref[0] = _finish(_chunked_attention(i, N_HEADS_FOX, scores, values, s_sc, p_sc)).astype(BF16)


def _fox(q, k, kx, vt):
    B, S, _ = q.shape
    assert S % CHUNK == 0
    return pl.pallas_call(
        _fox_kernel,
        grid=(B, S // TQ),
        in_specs=[pl.BlockSpec((1, TQ, FOX_COLS), lambda b, i: (b, i, 0)),
                  pl.BlockSpec((1, S, FOX_COLS), lambda b, i: (b, 0, 0)),
                  pl.BlockSpec((1, S, LANES), lambda b, i: (b, 0, 0)),
                  pl.BlockSpec((1, FOX_COLS, S), lambda b, i: (b, 0, 0))],
        out_specs=pl.BlockSpec((1, TQ, FOX_COLS), lambda b, i: (b, i, 0)),
        out_shape=jax.ShapeDtypeStruct((B, S, FOX_COLS), BF16),
        scratch_shapes=_chunk_scratch(N_HEADS_FOX),
        compiler_params=_params("parallel", "arbitrary"),
        name="fox",
    )(q, k, kx, vt)


WINDOW_KEYS = DIL_PATTERNS[0][0] // DIL_PATTERNS[0][1]
WQ = 256
assert all(w // d == WINDOW_KEYS for w, d in DIL_PATTERNS) and WINDOW_KEYS <= WQ


WINDOW_CHAINS = 8


def _window_kernel(q_ref, kc_ref, kp_ref, vc_ref, vp_ref, o_ref, lse_ref, *, dil, n_res, n_sub):
    a = pl.program_id(1)
    rg = pl.program_id(2)
    kr = lax.broadcasted_iota(jnp.int32, (WQ, WQ), 0)
    qc = lax.broadcasted_iota(jnp.int32, (WQ, WQ), 1)
    valid_d = (kr <= qc) & (qc - kr <= WINDOW_KEYS)
    krp = lax.broadcasted_iota(jnp.int32, (WINDOW_KEYS, WQ), 0)
    qcp = lax.broadcasted_iota(jnp.int32, (WINDOW_KEYS, WQ), 1)
    valid_p = krp >= qcp
    valid_first = valid_p & (a > 0)

    chains = [(rr, t) for rr in range(n_res) for t in range(n_sub)]
    heads = range(HEADS_PER_LANE_BLOCK)
    raw, vals = [], []
    for rr, t in chains:
        cols = slice(rr * LANES, (rr + 1) * LANES)
        sub = slice(t * WQ, (t + 1) * WQ)
        q_t = _query_t(q_ref[0, sub, cols])
        if t == 0:
            kp, vp = kp_ref[0, :, cols], vp_ref[0, :, cols]
        else:
            before = slice(t * WQ - WINDOW_KEYS, t * WQ)
            kp, vp = kc_ref[0, before, cols], vc_ref[0, before, cols]
        kd = kc_ref[0, sub, cols]
        vals.append((vc_ref[0, sub, cols], vp))
        q_m = [_head_rows(q_t, hl).astype(BF16) for hl in heads]
        raw.append([(_dot(kd, qm), _dot(kp, qm)) for qm in q_m])

    probs, stats = [], []
    for n, (rr, t) in enumerate(chains):
        for hl in heads:
            s_d = jnp.where(valid_d, raw[n][hl][0], NEG_INF)
            s_p = jnp.where(valid_first if t == 0 else valid_p, raw[n][hl][1], NEG_INF)
            m = jnp.maximum(_key_max(s_d), _key_max(s_p))
            p_d = jnp.exp2(s_d - m)
            p_p = jnp.exp2(s_p - m)
            l = _reduce_keys(p_d, jnp.add, jnp.sum) + _reduce_keys(p_p, jnp.add, jnp.sum)
            probs.append((p_d.astype(BF16), p_p.astype(BF16)))
            stats.append((m, l))

    accs = [_dot_tn(vals[n][0], probs[n * len(heads) + hl][0]) + _dot_tn(vals[n][1], probs[n * len(heads) + hl][1])
            for n in range(len(chains)) for hl in heads]

    for n, (rr, t) in enumerate(chains):
        outs, lses = [], []
        for hl in heads:
            m, l = stats[n * len(heads) + hl]
            outs.append(accs[n * len(heads) + hl][hl * HEAD_DIM:(hl + 1) * HEAD_DIM] / l)
            lses.append(jnp.broadcast_to(m + jnp.log2(l), (HEAD_DIM, WQ)))
        r = rg * n_res + rr
        rows = pl.ds(t * WQ * dil + r, WQ, stride=dil) if dil > 1 else pl.ds(t * WQ, WQ)
        o_ref[0, rows, :] = jnp.concatenate(outs, axis=0).T
        lse_ref[0, rows, :] = jnp.concatenate(lses, axis=0).T


def _window(q, k, v, group):
    dil = DIL_PATTERNS[group][1]
    B, rows, width = q.shape
    S = rows * dil
    n_sub = min(WINDOW_CHAINS, rows // WQ)
    n_res = min(dil, WINDOW_CHAINS // n_sub)
    tile = n_sub * WQ
    assert rows % tile == 0 and tile % WINDOW_KEYS == 0 and dil % n_res == 0
    col0, col_step = (FOX_COLS // LANES + group, 0) if dil == 1 else (0, 1)
    assert width == (MIX_WIDTH if dil == 1 else dil * LANES)
    cur = pl.BlockSpec((1, tile, n_res * LANES), lambda b, a, rg: (b, a, col0 + col_step * rg))
    per = tile // WINDOW_KEYS
    prev = pl.BlockSpec((1, WINDOW_KEYS, n_res * LANES),
                        lambda b, a, rg: (b, jnp.maximum(a * per - 1, 0), col0 + col_step * rg))
    out = pl.BlockSpec((1, dil * tile, LANES), lambda b, a, rg: (b, a, 0))
    o, lse = pl.pallas_call(
        functools.partial(_window_kernel, dil=dil, n_res=n_res, n_sub=n_sub),
        grid=(B, rows // tile, dil // n_res),
        in_specs=[cur, cur, prev, cur, prev],
        out_specs=[out, out],
        out_shape=[jax.ShapeDtypeStruct((B, S, LANES), F32)] * 2,
        compiler_params=_params("parallel", "arbitrary", "arbitrary"),
        name=f"window{group}",
    )(q, k, k, v, v)
    return o.reshape(B * S, LANES), lse.reshape(B * S, LANES)


MOBA_PAIRS = MOBA_COLS // LANES


def _moba_kernel(*refs):
    q_refs, k_refs, vt_refs = refs[:MOBA_PAIRS], refs[MOBA_PAIRS:2 * MOBA_PAIRS], refs[2 * MOBA_PAIRS:3 * MOBA_PAIRS]
    km_ref, o_ref, s_sc, p_sc = refs[3 * MOBA_PAIRS:]
    i = pl.program_id(1)
    n_blocks = km_ref.shape[1]
    blk = lax.broadcasted_iota(jnp.int32, (n_blocks, TQ), 0)
    own = i * (TQ // MOBA_BLOCK) + lax.broadcasted_iota(jnp.int32, (n_blocks, TQ), 1) // MOBA_BLOCK
    past = blk < own
    never = jnp.full((LANES - n_blocks, TQ), NEG_INF, F32)
    q_aug = []
    for p in range(MOBA_PAIRS):
        q_t = _query_t(q_refs[p][0])
        km = km_ref[0, :, p * LANES:(p + 1) * LANES]
        for hl in range(HEADS_PER_LANE_BLOCK):
            qm = _head_rows(q_t, hl)
            gate = jnp.where(past, _dot(km, qm.astype(BF16)), NEG_INF)
            sel = blk == own
            for _ in range(MOBA_TOPK):
                top = jnp.max(gate, axis=0, keepdims=True)
                idx = jnp.min(jnp.where(gate == top, blk, n_blocks), axis=0, keepdims=True)
                hit = blk == idx
                sel = sel | (hit & past)
                gate = jnp.where(hit, -jnp.inf, gate)
            sel_bias = jnp.where(sel, 0.0, NEG_INF)
            q_aug.append(jnp.concatenate([qm, sel_bias, never], axis=0).astype(BF16))

    row = lax.broadcasted_iota(jnp.int32, (CHUNK, LANES), 0)
    lane = lax.broadcasted_iota(jnp.int32, (CHUNK, LANES), 1)
    blocks_per_chunk = CHUNK // MOBA_BLOCK

    def scores(c):
        off = pl.multiple_of(c * CHUNK, CHUNK)
        onehot = jnp.where(lane == c * blocks_per_chunk + row // MOBA_BLOCK, 1.0, 0.0).astype(BF16)
        out = []
        for p in range(MOBA_PAIRS):
            k_aug = jnp.concatenate([k_refs[p][0, pl.ds(off, CHUNK), :], onehot], axis=1)
            out += [functools.partial(_dot, k_aug, qa)
                    for qa in q_aug[p * HEADS_PER_LANE_BLOCK:(p + 1) * HEADS_PER_LANE_BLOCK]]
        return out

    def values(c, h):
        off = pl.multiple_of(c * CHUNK, CHUNK)
        p, hl = divmod(h, HEADS_PER_LANE_BLOCK)
        return vt_refs[p][0, hl * HEAD_DIM:(hl + 1) * HEAD_DIM, pl.ds(off, CHUNK)]

    o_ref[0] = _finish(_chunked_attention(i, N_HEADS_MOBA, scores, values, s_sc, p_sc)).astype(BF16)


def _moba(q, k, vt, km):
    B, S, _ = q.shape
    n_blocks = S // MOBA_BLOCK
    assert TQ % MOBA_BLOCK == 0 and CHUNK % MOBA_BLOCK == 0 and S % CHUNK == 0
    assert n_blocks <= LANES and n_blocks % (2 * SUBLANES) == 0
    first = (FOX_COLS + DIL_COLS) // LANES
    at = lambda f: [pl.BlockSpec(*f(first + p)) for p in range(MOBA_PAIRS)]
    return pl.pallas_call(
        _moba_kernel,
        grid=(B, S // TQ),
        in_specs=at(lambda c: ((1, TQ, LANES), lambda b, i: (b, i, c)))
        + at(lambda c: ((1, S, LANES), lambda b, i: (b, 0, c)))
        + at(lambda c: ((1, LANES, S), lambda b, i: (b, c, 0)))
        + [pl.BlockSpec((1, n_blocks, MOBA_COLS), lambda b, i: (b, 0, 0))],
        out_specs=pl.BlockSpec((1, TQ, MOBA_COLS), lambda b, i: (b, i, 0)),
        out_shape=jax.ShapeDtypeStruct((B, S, MOBA_COLS), BF16),
        scratch_shapes=_chunk_scratch(N_HEADS_MOBA),
        compiler_params=_params("parallel", "arbitrary"),
        name="moba",
    )(*([q] * MOBA_PAIRS + [k] * MOBA_PAIRS + [vt] * MOBA_PAIRS + [km]))


MXU_TILE = 256
assert D_FF % MXU_TILE == 0
FF_SPLITS = (0, (D_FF // MXU_TILE + 1) // 2 * MXU_TILE, D_FF)


MIX_SPLIT = 2


def _mix_slab(rows, ya_ref, o_refs, l_refs, yc_ref, gates_ref, x_ref, p_ref,
              wa_ref, wb_ref, wc_ref, wo_ref, g_mix_ref,
              g_pre_ref, wgate_ref, wup_ref, wdown_ref, g_post_ref, wple_ref, wpg_ref, g_ple_ref, o_ref):
    lses = [l_ref[rows, :] for l_ref in l_refs]
    top = jnp.maximum(jnp.maximum(lses[0], lses[1]), lses[2])
    wts = [jnp.exp2(t - top) for t in lses]
    den = wts[0] + wts[1] + wts[2]
    yb = (wts[0] * o_refs[0][rows, :] + wts[1] * o_refs[1][rows, :] + wts[2] * o_refs[2][rows, :]) / den
    merged = None
    for c, (y, w_ref) in enumerate(((ya_ref[rows, :], wa_ref), (yb.astype(BF16), wb_ref), (yc_ref[rows, :], wc_ref))):
        branch = _dot(y, w_ref[...])
        yield
        term = gates_ref[rows, c * D_MODEL:(c + 1) * D_MODEL].astype(F32) * branch
        merged = term if merged is None else merged + term
    out = _dot(merged.astype(BF16), wo_ref[...])
    yield
    x = x_ref[rows, :] + _rms(out, g_mix_ref[...])

    hb = _rms(x, g_pre_ref[...]).astype(BF16)
    down = None
    for lo, hi in zip(FF_SPLITS[:-1], FF_SPLITS[1:]):
        cols = slice(lo, hi)
        gate = _dot(hb, wgate_ref[:, cols])
        yield
        up = _dot(hb, wup_ref[:, cols])
        yield
        ff = (gate * jax.nn.sigmoid(gate) * up).astype(BF16)
        part = _dot(ff, wdown_ref[cols, :])
        yield
        down = part if down is None else down + part
    x = x + _rms(down, g_post_ref[...])

    emb = _dot(p_ref[0, rows, :].astype(BF16), wple_ref[...])
    yield
    ple = emb * jax.nn.sigmoid(_dot(x.astype(BF16), wpg_ref[...]))
    yield
    o_ref[rows, :] = x + _rms(ple, g_ple_ref[...])


def _mix_kernel(ya_ref, o0_ref, o1_ref, o2_ref, l0_ref, l1_ref, l2_ref, *rest):
    slab = TM // MIX_SPLIT
    chains = [_mix_slab(pl.ds(n * slab, slab), ya_ref, (o0_ref, o1_ref, o2_ref), (l0_ref, l1_ref, l2_ref), *rest)
              for n in range(MIX_SPLIT)]
    while chains:
        chains = [c for c in chains if next(c, StopIteration) is not StopIteration]


def _mix(ya, ob, lb, yc, gates, x, p, layer, mix_weights, g_mix, g_pre, ffn_weights, g_post, ple_weights, g_ple):
    T, D = x.shape
    tok = lambda w: pl.BlockSpec((TM, w), lambda i: (i, 0))
    vec = _resident((1, D))
    whole = lambda ws: [_resident(w.shape) for w in ws]
    return pl.pallas_call(
        _mix_kernel,
        grid=(T // TM,),
        in_specs=[tok(FOX_COLS)] + [tok(LANES)] * 6 + [tok(MOBA_COLS), tok(N_GATE_COLS), tok(D),
                  pl.BlockSpec((1, TM, PLE_DIM), lambda i: (layer, i, 0))]
        + whole(mix_weights) + [vec, vec] + whole(ffn_weights) + [vec] + whole(ple_weights) + [vec],
        out_specs=tok(D),
        out_shape=jax.ShapeDtypeStruct((T, D), F32),
        compiler_params=_params("parallel"),
        name="mix",
    )(ya, *ob, *lb, yc, gates, x, p, *mix_weights, g_mix, g_pre, *ffn_weights, g_post, *ple_weights, g_ple)


def _rope_tables(seq):
    f32 = np.float32
    inv = (f32(1.0) / (f32(ROPE_THETA) ** (np.arange(0, HEAD_DIM, 2, dtype=f32) / f32(HEAD_DIM)))).astype(f32)
    ang = np.arange(seq, dtype=f32)[:, None] * inv[None, :]
    cos, sin = np.cos(ang).astype(f32), np.sin(ang).astype(f32)
    reps = LANES // HEAD_DIM
    cos_t = np.tile(np.concatenate([cos, cos], axis=1), (1, reps))
    sin_t = np.tile(np.concatenate([-sin, sin], axis=1), (1, reps))
    return jnp.asarray(cos_t), jnp.asarray(sin_t)


def kernel(x, p, g_mix_pre, w_in, b_f, w_br_a, w_br_b, w_br_c, w_out, g_mix_post, g_ffn_pre,
           w_ffn_gate, w_ffn_up, w_ffn_down, g_ffn_post, w_ple, w_ple_gate, g_ple_post):
    B, S, D = x.shape
    T = B * S
    depth = w_in.shape[0]
    assert D == D_MODEL and S % TM == 0 and S % TQ == 0 and TM % PROJ_SLAB == 0 and TM % MIX_SPLIT == 0
    assert w_in.shape[1:] == (D, 3 * MIX_WIDTH + N_HEADS_FOX + N_GATE_COLS) and p.shape == (depth, B, S, PLE_DIM)
    cos_t, sin_t = _rope_tables(S)
    row = lambda g: g.reshape(1, -1)
    f0 = 3 * MIX_WIDTH
    for i in range(depth):
        wq = w_in[i, :, :MIX_WIDTH].astype(BF16)
        wk = w_in[i, :, MIX_WIDTH:2 * MIX_WIDTH].astype(BF16)
        wv = w_in[i, :, 2 * MIX_WIDTH:f0].astype(BF16)
        wf = jnp.pad(w_in[i, :, f0:f0 + N_HEADS_FOX], ((0, 0), (0, LANES - N_HEADS_FOX))).astype(BF16)
        bf = jnp.pad(b_f[i], (0, LANES - N_HEADS_FOX)).reshape(1, LANES)
        wg = w_in[i, :, f0 + N_HEADS_FOX:].astype(BF16)

        (q, k, v, vt, kx, gates, kmean), residue_views = _proj(
            x, row(g_mix_pre[i]), wq, wk, wv, wf, bf, wg, cos_t, sin_t)
        km = kmean.reshape(B, S // MOBA_BLOCK, MOBA_COLS).astype(BF16)
        y_a = _fox(q, k, kx, vt)
        win_in = {g: residue_views[3 * n:3 * n + 3] for n, (g, _) in enumerate(STRIDED_GROUPS)}
        win = [_window(*win_in.get(g, (q, k, v)), g) for g in range(len(DIL_PATTERNS))]
        y_c = _moba(q, k, vt, km)

        bf16 = lambda *ws: [w[i].astype(BF16) for w in ws]
        x = _mix(y_a.reshape(T, -1), [o for o, _ in win], [l for _, l in win], y_c.reshape(T, -1),
                 gates.reshape(T, -1), x.reshape(T, D), p.reshape(depth, T, PLE_DIM), i,
                 bf16(w_br_a, w_br_b, w_br_c, w_out), row(g_mix_post[i]),
                 row(g_ffn_pre[i]), bf16(w_ffn_gate, w_ffn_up, w_ffn_down), row(g_ffn_post[i]),
                 bf16(w_ple, w_ple_gate), row(g_ple_post[i])).reshape(B, S, D)
    return x
```

```python
import functools
import math

import numpy as np
import jax
import jax.numpy as jnp
from jax import lax
from jax.experimental import pallas as pl
from jax.experimental.pallas import tpu as pltpu

D_MODEL = 1024
HEAD_DIM = 64
N_HEADS_FOX = 4
DIL_PATTERNS = ((128, 1), (512, 4), (2048, 16))
N_HEADS_MOBA = 6
MIX_WIDTH = 1024
MOBA_BLOCK = 256
MOBA_TOPK = 3
PLE_DIM = 256
D_FF = 2816
ROPE_THETA = 10000.0
RMS_EPS = 1e-6
NEG_INF = -1e30
LOG2E = math.log2(math.e)
Q_SCALE = HEAD_DIM ** -0.5 * LOG2E

LANES = 128
SUBLANES = 8
HEADS_PER_LANE_BLOCK = LANES // HEAD_DIM
N_LANE_BLOCKS = MIX_WIDTH // LANES
FOX_COLS = N_HEADS_FOX * HEAD_DIM
DIL_COLS = len(DIL_PATTERNS) * 2 * HEAD_DIM
MOBA_COLS = N_HEADS_MOBA * HEAD_DIM
N_GATE_COLS = 3 * D_MODEL
FORGET_PIECES = 3

TM = 512
TQ = 512
CHUNK = 512
VMEM_LIMIT = 56 * 1024 * 1024

F32 = jnp.float32
BF16 = jnp.bfloat16


def _dot(a, b):
    return jnp.dot(a, b, preferred_element_type=F32)


def _dot_tn(a, b):
    return lax.dot_general(a, b, (((0,), (0,)), ((), ())), preferred_element_type=F32)


def _rms(x, g):
    var = jnp.mean(x * x, axis=-1, keepdims=True)
    return x * lax.rsqrt(var + RMS_EPS) * g


def _split3(x):
    hi = x.astype(BF16).astype(F32)
    r1 = x - hi
    mid = r1.astype(BF16).astype(F32)
    lo = (r1 - mid).astype(BF16).astype(F32)
    return hi, mid, lo


def _resident(shape):
    nd = len(shape)
    return pl.BlockSpec(shape, lambda *_: (0,) * nd, pipeline_mode=pl.Buffered(1))


def _params(*sem):
    return pltpu.CompilerParams(dimension_semantics=sem, vmem_limit_bytes=VMEM_LIMIT)


STRIDED_GROUPS = tuple((g, dil) for g, (_, dil) in enumerate(DIL_PATTERNS) if dil > 1)


PROJ_SLAB = MOBA_BLOCK


def _proj_slab(n, carry, x_ref, g_ref, wq_ref, wk_ref, wv_ref, wf_ref, bf_ref, wg_ref, cos_ref, sin_ref,
               q_ref, k_ref, v_ref, vt_ref, kx_ref, gates_ref, kmean_ref, res_refs, perm_sc):
    rows = pl.ds(n * PROJ_SLAB, PROJ_SLAB)

    def scatter_residues(t, which, blk):
        for m, (g, dil) in enumerate(STRIDED_GROUPS):
            if blk == FOX_COLS // LANES + g:
                slot = 3 * m + which
                per = PROJ_SLAB // dil
                perm_sc[slot, rows, :] = t
                for r in range(dil):
                    res_refs[slot][0, n * per:(n + 1) * per, r * LANES:(r + 1) * LANES] = perm_sc[
                        slot, pl.ds(n * PROJ_SLAB + r, per, stride=dil), :].astype(BF16)

    hb = _rms(x_ref[0, rows, :], g_ref[...]).astype(BF16)
    cos = cos_ref[rows, :]
    sin = sin_ref[rows, :]
    lane = lax.broadcasted_iota(jnp.int32, (PROJ_SLAB, LANES), 1)
    first_half = (lane % HEAD_DIM) < (HEAD_DIM // 2)

    def rope(t):
        rot = jnp.where(first_half, pltpu.roll(t, LANES - HEAD_DIM // 2, 1),
                        pltpu.roll(t, HEAD_DIM // 2, 1))
        return t * cos + rot * sin

    first_rope_blk = FOX_COLS // LANES
    first_moba_blk = (FOX_COLS + DIL_COLS) // LANES

    f = _dot(hb, wf_ref[...]) + bf_ref[...]
    ls = jnp.minimum(f, 0.0) - jnp.log1p(jnp.exp(-jnp.abs(f)))
    ls = jnp.where(lane < N_HEADS_FOX, ls, 0.0)
    row = lax.broadcasted_iota(jnp.int32, (PROJ_SLAB, PROJ_SLAB), 0)
    col = lax.broadcasted_iota(jnp.int32, (PROJ_SLAB, PROJ_SLAB), 1)
    tri = jnp.where(col <= row, 1.0, 0.0).astype(BF16)
    ls_hi, ls_mid, ls_lo = _split3(ls)

    q = _dot(hb, wq_ref[...])
    yield
    cs = _dot(tri, ls_hi.astype(BF16)) + _dot(tri, ls_mid.astype(BF16)) + _dot(tri, ls_lo.astype(BF16))
    for c in range(N_LANE_BLOCKS):
        t = q[:, c * LANES:(c + 1) * LANES]
        if c >= first_rope_blk:
            t = rope(t)
        t = t * Q_SCALE
        q_ref[0, rows, c * LANES:(c + 1) * LANES] = t.astype(BF16)
        scatter_residues(t, 0, c)

    k = _dot(hb, wk_ref[...])
    yield
    cs = cs + carry[n]
    carry.append(cs[PROJ_SLAB - 1:PROJ_SLAB, :])
    hi, mid, lo = _split3(-LOG2E * cs)
    pieces = jnp.where(lane < N_HEADS_FOX, hi,
                       jnp.where(lane < 2 * N_HEADS_FOX, pltpu.roll(mid, N_HEADS_FOX, 1),
                                 pltpu.roll(lo, 2 * N_HEADS_FOX, 1)))
    kx_ref[0, rows, :] = pieces.astype(BF16)

    for c in range(N_LANE_BLOCKS):
        t = k[:, c * LANES:(c + 1) * LANES]
        if c >= first_rope_blk:
            t = rope(t)
        k_ref[0, rows, c * LANES:(c + 1) * LANES] = t.astype(BF16)
        scatter_residues(t, 1, c)
        if c >= first_moba_blk:
            cm = c - first_moba_blk
            kmean_ref[0, 0, n:n + 1, cm * LANES:(cm + 1) * LANES] = jnp.mean(t, axis=0, keepdims=True)

    v = _dot(hb, wv_ref[...])
    yield
    v_ref[0, rows, :] = v.astype(BF16)
    for c in range(N_LANE_BLOCKS):
        scatter_residues(v[:, c * LANES:(c + 1) * LANES], 2, c)
    vt_ref[0, :, rows] = v.T.astype(BF16)

    for c in range(N_GATE_COLS // D_MODEL):
        z = _dot(hb, wg_ref[:, c * D_MODEL:(c + 1) * D_MODEL])
        yield
        gates_ref[0, rows, c * D_MODEL:(c + 1) * D_MODEL] = jax.nn.sigmoid(z).astype(BF16)


def _proj_kernel(*refs):
    *io_refs, carry_ref, perm_sc = refs
    n_res = 3 * len(STRIDED_GROUPS)
    main_refs, res_refs = io_refs[:len(io_refs) - n_res], io_refs[len(io_refs) - n_res:]

    @pl.when(pl.program_id(1) == 0)
    def _():
        carry_ref[...] = jnp.zeros_like(carry_ref)

    carry = [carry_ref[...]]
    chains = [_proj_slab(n, carry, *main_refs, res_refs, perm_sc) for n in range(TM // PROJ_SLAB)]
    while chains:
        chains = [c for c in chains if next(c, StopIteration) is not StopIteration]
    carry_ref[...] = carry[-1]


def _proj(x, g, wq, wk, wv, wf, bf, wg, cos_t, sin_t):
    B, S, D = x.shape
    n_t = S // TM
    tok = lambda w: pl.BlockSpec((1, TM, w), lambda b, i: (b, i, 0))
    res_specs, res_shapes = [], []
    for _, dil in STRIDED_GROUPS:
        assert TM % (dil * 2 * SUBLANES) == 0 and S % dil == 0
        res_specs += [pl.BlockSpec((1, TM // dil, dil * LANES), lambda b, i: (b, i, 0))] * 3
        res_shapes += [jax.ShapeDtypeStruct((B, S // dil, dil * LANES), BF16)] * 3
    outs = pl.pallas_call(
        _proj_kernel,
        grid=(B, n_t),
        in_specs=[tok(D), _resident((1, D)), _resident((D, MIX_WIDTH)), _resident((D, MIX_WIDTH)),
                  _resident((D, MIX_WIDTH)),
                  _resident((D, LANES)), _resident((1, LANES)), _resident((D, N_GATE_COLS)),
                  pl.BlockSpec((TM, LANES), lambda b, i: (i, 0)),
                  pl.BlockSpec((TM, LANES), lambda b, i: (i, 0))],
        out_specs=[tok(MIX_WIDTH), tok(MIX_WIDTH), tok(MIX_WIDTH),
                   pl.BlockSpec((1, MIX_WIDTH, TM), lambda b, i: (b, 0, i)),
                   tok(LANES), tok(N_GATE_COLS),
                   pl.BlockSpec((1, 1, TM // MOBA_BLOCK, MOBA_COLS), lambda b, i: (b, i, 0, 0))] + res_specs,
        out_shape=[jax.ShapeDtypeStruct((B, S, MIX_WIDTH), BF16)] * 3
        + [jax.ShapeDtypeStruct((B, MIX_WIDTH, S), BF16),
           jax.ShapeDtypeStruct((B, S, LANES), BF16),
           jax.ShapeDtypeStruct((B, S, N_GATE_COLS), BF16),
           jax.ShapeDtypeStruct((B, n_t, TM // MOBA_BLOCK, MOBA_COLS), F32)] + res_shapes,
        scratch_shapes=[pltpu.VMEM((1, LANES), F32), pltpu.VMEM((len(res_specs), TM, LANES), F32)],
        compiler_params=_params("arbitrary", "arbitrary"),
        name="proj",
    )(x, g, wq, wk, wv, wf, bf, wg, cos_t, sin_t)
    return outs[:7], outs[7:]


def _reduce_keys(x, op, reduce):
    n = x.shape[0]
    while n > SUBLANES and n % (2 * SUBLANES) == 0:
        n //= 2
        x = op(x[:n], x[n:])
    return reduce(x, axis=0, keepdims=True)


def _key_max(s):
    return _reduce_keys(s, jnp.maximum, jnp.max)


def _probabilities(s, m):
    return jnp.exp2((s - m).astype(BF16))


ACC_ROWS = HEAD_DIM + 2 * SUBLANES


def _values_with_ones(vt):
    return jnp.concatenate([vt, jnp.ones((ACC_ROWS - HEAD_DIM, vt.shape[1]), BF16)], axis=0)


def _score(k_aug, q_aug, keys=slice(None), queries=slice(None)):
    return _dot(k_aug[keys], q_aug[:, queries])


def _chunked_attention(i, n_heads, scores, values, s_sc, p_sc):
    heads = range(n_heads)
    n_full = (i * TQ) // CHUNK

    assert TQ == CHUNK and CHUNK % (2 * LANES) == 0
    half = CHUNK // 2
    early, late = slice(0, half), slice(half, CHUNK)
    kr = lax.broadcasted_iota(jnp.int32, (half, half), 0)
    qc = lax.broadcasted_iota(jnp.int32, (half, half), 1)
    causal = kr <= qc
    diag = scores(n_full)
    s_early = [score(keys=early) for score in diag]
    s_late = [score(keys=late, queries=late) for score in diag]
    s_first = [score() for score in scores(0)]
    m, alpha = [], []
    for hl in heads:
        se = jnp.concatenate([jnp.where(causal, s_early[hl][:, early], NEG_INF), s_early[hl][:, late]], axis=1)
        sl = jnp.where(causal, s_late[hl], NEG_INF)
        m_diag = jnp.maximum(_key_max(se),
                             jnp.concatenate([jnp.full((1, half), NEG_INF, F32), _key_max(sl)], axis=1))
        p_sc[hl, early, :] = _probabilities(se, m_diag)
        p_sc[hl, late, :] = jnp.concatenate([jnp.zeros((half, half), BF16), _probabilities(sl, m_diag[:, late])], axis=1)
        s_sc[hl] = s_first[hl]
        m.append(jnp.where(n_full > 0, jnp.maximum(m_diag, _key_max(s_first[hl])), m_diag))
        alpha.append(jnp.exp2(m_diag - m[hl]))

    def product(c, p):
        return [_dot(_values_with_ones(values(c, hl)), p[hl]) for hl in heads]

    def before(c):
        return jnp.where(c <= 0, n_full, c - 1)

    def body(c, carry):
        m, alpha, acc = carry
        pv = product(before(c), [p_sc[hl] for hl in heads])
        s_next = [score() for score in scores(c + 1)]
        m_next = tuple(jnp.maximum(m[hl], _key_max(s_next[hl])) for hl in heads)
        alpha_next = tuple(jnp.exp2(m[hl] - m_next[hl]) for hl in heads)
        for hl in heads:
            p_sc[hl] = _probabilities(s_sc[hl], m[hl])
            s_sc[hl] = s_next[hl]
        acc = tuple(alpha[hl] * (acc[hl] + pv[hl]) for hl in heads)
        return m_next, alpha_next, acc

    init = (tuple(m), tuple(alpha), tuple(jnp.zeros((ACC_ROWS, TQ), F32) for _ in heads))
    m, alpha, acc = lax.fori_loop(0, n_full - 1, body, init)

    last = n_full - 1
    pv = product(before(last), [p_sc[hl] for hl in heads])
    m_tail = [jnp.where(n_full > 0, m[hl], -NEG_INF) for hl in heads]
    tail = product(jnp.maximum(last, 0), [_probabilities(s_sc[hl], m_tail[hl]) for hl in heads])
    return [alpha[hl] * (acc[hl] + pv[hl]) + tail[hl] for hl in heads]


def _query_t(q):
    return q.astype(F32).T


def _head_rows(q_t, hl):
    sub = lax.broadcasted_iota(jnp.int32, q_t.shape, 0)
    return jnp.where((sub >= hl * HEAD_DIM) & (sub < (hl + 1) * HEAD_DIM), q_t, 0.0)


def _finish(accs):
    blocks = []
    for p in range(len(accs) // HEADS_PER_LANE_BLOCK):
        pair = accs[p * HEADS_PER_LANE_BLOCK:(p + 1) * HEADS_PER_LANE_BLOCK]
        out_t = jnp.concatenate([acc[:HEAD_DIM] / acc[HEAD_DIM:HEAD_DIM + 1] for acc in pair], axis=0)
        blocks.append(out_t.T)
    return jnp.concatenate(blocks, axis=1)


def _chunk_scratch(n_heads):
    return [pltpu.VMEM((n_heads, CHUNK, TQ), F32), pltpu.VMEM((n_heads, CHUNK, TQ), BF16)]


def _fox_kernel(q_ref, k_ref, kx_ref, vt_ref, o_ref, s_sc, p_sc):
    i = pl.program_id(1)
    n_pairs = FOX_COLS // LANES
    sub = lax.broadcasted_iota(jnp.int32, (LANES, TQ), 0)
    q_aug = []
    for p in range(n_pairs):
        q_t = _query_t(q_ref[0, :, p * LANES:(p + 1) * LANES])
        for hl in range(HEADS_PER_LANE_BLOCK):
            head = HEADS_PER_LANE_BLOCK * p + hl
            ones = jnp.where((sub % N_HEADS_FOX == head) & (sub < FORGET_PIECES * N_HEADS_FOX), 1.0, 0.0)
            q_aug.append(jnp.concatenate([_head_rows(q_t, hl), ones], axis=0).astype(BF16))

    def scores(c):
        off = pl.multiple_of(c * CHUNK, CHUNK)
        kx = kx_ref[0, pl.ds(off, CHUNK), :]
        out = []
        for p in range(n_pairs):
            k_aug = jnp.concatenate([k_ref[0, pl.ds(off, CHUNK), p * LANES:(p + 1) * LANES], kx], axis=1)
            out += [functools.partial(_score, k_aug, qa)
                    for qa in q_aug[p * HEADS_PER_LANE_BLOCK:(p + 1) * HEADS_PER_LANE_BLOCK]]
        return out

    def values(c, h):
        off = pl.multiple_of(c * CHUNK, CHUNK)
        return vt_ref[0, h * HEAD_DIM:(h + 1) * HEAD_DIM, pl.ds(off, CHUNK)]

    o_ref[0] = _finish(_chunked_attention(i, N_HEADS_FOX, scores, values, s_sc, p_sc)).astype(BF16)


def _fox(q, k, kx, vt):
    B, S, _ = q.shape
    assert S % CHUNK == 0
    return pl.pallas_call(
        _fox_kernel,
        grid=(B, S // TQ),
        in_specs=[pl.BlockSpec((1, TQ, FOX_COLS), lambda b, i: (b, i, 0)),
                  pl.BlockSpec((1, S, FOX_COLS), lambda b, i: (b, 0, 0)),
                  pl.BlockSpec((1, S, LANES), lambda b, i: (b, 0, 0)),
                  pl.BlockSpec((1, FOX_COLS, S), lambda b, i: (b, 0, 0))],
        out_specs=pl.BlockSpec((1, TQ, FOX_COLS), lambda b, i: (b, i, 0)),
        out_shape=jax.ShapeDtypeStruct((B, S, FOX_COLS), BF16),
        scratch_shapes=_chunk_scratch(N_HEADS_FOX),
        compiler_params=_params("parallel", "arbitrary"),
        name="fox",
    )(q, k, kx, vt)


WINDOW_KEYS = DIL_PATTERNS[0][0] // DIL_PATTERNS[0][1]
WQ = 256
assert all(w // d == WINDOW_KEYS for w, d in DIL_PATTERNS) and WINDOW_KEYS <= WQ


WINDOW_CHAINS = 8


def _window_kernel(q_ref, kc_ref, kp_ref, vc_ref, vp_ref, o_ref, lse_ref, *, dil, n_res, n_sub):
    a = pl.program_id(1)
    rg = pl.program_id(2)
    kr = lax.broadcasted_iota(jnp.int32, (WQ, WQ), 0)
    qc = lax.broadcasted_iota(jnp.int32, (WQ, WQ), 1)
    valid_d = (kr <= qc) & (qc - kr <= WINDOW_KEYS)
    krp = lax.broadcasted_iota(jnp.int32, (WINDOW_KEYS, WQ), 0)
    qcp = lax.broadcasted_iota(jnp.int32, (WINDOW_KEYS, WQ), 1)
    valid_p = krp >= qcp
    valid_first = valid_p & (a > 0)

    chains = [(rr, t) for rr in range(n_res) for t in range(n_sub)]
    heads = range(HEADS_PER_LANE_BLOCK)
    raw, vals = [], []
    for rr, t in chains:
        cols = slice(rr * LANES, (rr + 1) * LANES)
        sub = slice(t * WQ, (t + 1) * WQ)
        q_t = _query_t(q_ref[0, sub, cols])
        if t == 0:
            kp, vp = kp_ref[0, :, cols], vp_ref[0, :, cols]
        else:
            before = slice(t * WQ - WINDOW_KEYS, t * WQ)
            kp, vp = kc_ref[0, before, cols], vc_ref[0, before, cols]
        kd = kc_ref[0, sub, cols]
        vals.append((vc_ref[0, sub, cols], vp))
        q_m = [_head_rows(q_t, hl).astype(BF16) for hl in heads]
        raw.append([(_dot(kd, qm), _dot(kp, qm)) for qm in q_m])

    probs, stats = [], []
    for n, (rr, t) in enumerate(chains):
        for hl in heads:
            s_d = jnp.where(valid_d, raw[n][hl][0], NEG_INF)
            s_p = jnp.where(valid_first if t == 0 else valid_p, raw[n][hl][1], NEG_INF)
            m = jnp.maximum(_key_max(s_d), _key_max(s_p))
            p_d = jnp.exp2(s_d - m)
            p_p = jnp.exp2(s_p - m)
            l = _reduce_keys(p_d, jnp.add, jnp.sum) + _reduce_keys(p_p, jnp.add, jnp.sum)
            probs.append((p_d.astype(BF16), p_p.astype(BF16)))
            stats.append((m, l))

    accs = [_dot_tn(vals[n][0], probs[n * len(heads) + hl][0]) + _dot_tn(vals[n][1], probs[n * len(heads) + hl][1])
            for n in range(len(chains)) for hl in heads]

    for n, (rr, t) in enumerate(chains):
        outs, lses = [], []
        for hl in heads:
            m, l = stats[n * len(heads) + hl]
            outs.append(accs[n * len(heads) + hl][hl * HEAD_DIM:(hl + 1) * HEAD_DIM] / l)
            lses.append(jnp.broadcast_to(m + jnp.log2(l), (HEAD_DIM, WQ)))
        r = rg * n_res + rr
        rows = pl.ds(t * WQ * dil + r, WQ, stride=dil) if dil > 1 else pl.ds(t * WQ, WQ)
        o_ref[0, rows, :] = jnp.concatenate(outs, axis=0).T
        lse_ref[0, rows, :] = jnp.concatenate(lses, axis=0).T


def _window(q, k, v, group):
    dil = DIL_PATTERNS[group][1]
    B, rows, width = q.shape
    S = rows * dil
    n_sub = min(WINDOW_CHAINS, rows // WQ)
    n_res = min(dil, WINDOW_CHAINS // n_sub)
    tile = n_sub * WQ
    assert rows % tile == 0 and tile % WINDOW_KEYS == 0 and dil % n_res == 0
    col0, col_step = (FOX_COLS // LANES + group, 0) if dil == 1 else (0, 1)
    assert width == (MIX_WIDTH if dil == 1 else dil * LANES)
    cur = pl.BlockSpec((1, tile, n_res * LANES), lambda b, a, rg: (b, a, col0 + col_step * rg))
    per = tile // WINDOW_KEYS
    prev = pl.BlockSpec((1, WINDOW_KEYS, n_res * LANES),
                        lambda b, a, rg: (b, jnp.maximum(a * per - 1, 0), col0 + col_step * rg))
    out = pl.BlockSpec((1, dil * tile, LANES), lambda b, a, rg: (b, a, 0))
    o, lse = pl.pallas_call(
        functools.partial(_window_kernel, dil=dil, n_res=n_res, n_sub=n_sub),
        grid=(B, rows // tile, dil // n_res),
        in_specs=[cur, cur, prev, cur, prev],
        out_specs=[out, out],
        out_shape=[jax.ShapeDtypeStruct((B, S, LANES), F32)] * 2,
        compiler_params=_params("parallel", "arbitrary", "arbitrary"),
        name=f"window{group}",
    )(q, k, k, v, v)
    return o.reshape(B * S, LANES), lse.reshape(B * S, LANES)


MOBA_PAIRS = MOBA_COLS // LANES


def _moba_kernel(*refs):
    q_refs, k_refs, vt_refs = refs[:MOBA_PAIRS], refs[MOBA_PAIRS:2 * MOBA_PAIRS], refs[2 * MOBA_PAIRS:3 * MOBA_PAIRS]
    km_ref, o_ref, s_sc, p_sc = refs[3 * MOBA_PAIRS:]
    i = pl.program_id(1)
    n_blocks = km_ref.shape[1]
    blk = lax.broadcasted_iota(jnp.int32, (n_blocks, TQ), 0)
    own = i * (TQ // MOBA_BLOCK) + lax.broadcasted_iota(jnp.int32, (n_blocks, TQ), 1) // MOBA_BLOCK
    past = blk < own
    never = jnp.full((LANES - n_blocks, TQ), NEG_INF, F32)
    q_aug = []
    for p in range(MOBA_PAIRS):
        q_t = _query_t(q_refs[p][0])
        km = km_ref[0, :, p * LANES:(p + 1) * LANES]
        for hl in range(HEADS_PER_LANE_BLOCK):
            qm = _head_rows(q_t, hl)
            gate = jnp.where(past, _dot(km, qm.astype(BF16)), NEG_INF)
            sel = blk == own
            for _ in range(MOBA_TOPK):
                top = jnp.max(gate, axis=0, keepdims=True)
                idx = jnp.min(jnp.where(gate == top, blk, n_blocks), axis=0, keepdims=True)
                hit = blk == idx
                sel = sel | (hit & past)
                gate = jnp.where(hit, -jnp.inf, gate)
            sel_bias = jnp.where(sel, 0.0, NEG_INF)
            q_aug.append(jnp.concatenate([qm, sel_bias, never], axis=0).astype(BF16))

    row = lax.broadcasted_iota(jnp.int32, (CHUNK, LANES), 0)
    lane = lax.broadcasted_iota(jnp.int32, (CHUNK, LANES), 1)
    blocks_per_chunk = CHUNK // MOBA_BLOCK

    def scores(c):
        off = pl.multiple_of(c * CHUNK, CHUNK)
        onehot = jnp.where(lane == c * blocks_per_chunk + row // MOBA_BLOCK, 1.0, 0.0).astype(BF16)
        out = []
        for p in range(MOBA_PAIRS):
            k_aug = jnp.concatenate([k_refs[p][0, pl.ds(off, CHUNK), :], onehot], axis=1)
            out += [functools.partial(_score, k_aug, qa)
                    for qa in q_aug[p * HEADS_PER_LANE_BLOCK:(p + 1) * HEADS_PER_LANE_BLOCK]]
        return out

    def values(c, h):
        off = pl.multiple_of(c * CHUNK, CHUNK)
        p, hl = divmod(h, HEADS_PER_LANE_BLOCK)
        return vt_refs[p][0, hl * HEAD_DIM:(hl + 1) * HEAD_DIM, pl.ds(off, CHUNK)]

    o_ref[0] = _finish(_chunked_attention(i, N_HEADS_MOBA, scores, values, s_sc, p_sc)).astype(BF16)


def _moba(q, k, vt, km):
    B, S, _ = q.shape
    n_blocks = S // MOBA_BLOCK
    assert TQ % MOBA_BLOCK == 0 and CHUNK % MOBA_BLOCK == 0 and S % CHUNK == 0
    assert n_blocks <= LANES and n_blocks % (2 * SUBLANES) == 0
    first = (FOX_COLS + DIL_COLS) // LANES
    at = lambda f: [pl.BlockSpec(*f(first + p)) for p in range(MOBA_PAIRS)]
    return pl.pallas_call(
        _moba_kernel,
        grid=(B, S // TQ),
        in_specs=at(lambda c: ((1, TQ, LANES), lambda b, i: (b, i, c)))
        + at(lambda c: ((1, S, LANES), lambda b, i: (b, 0, c)))
        + at(lambda c: ((1, LANES, S), lambda b, i: (b, c, 0)))
        + [pl.BlockSpec((1, n_blocks, MOBA_COLS), lambda b, i: (b, 0, 0))],
        out_specs=pl.BlockSpec((1, TQ, MOBA_COLS), lambda b, i: (b, i, 0)),
        out_shape=jax.ShapeDtypeStruct((B, S, MOBA_COLS), BF16),
        scratch_shapes=_chunk_scratch(N_HEADS_MOBA),
        compiler_params=_params("parallel", "arbitrary"),
        name="moba",
    )(*([q] * MOBA_PAIRS + [k] * MOBA_PAIRS + [vt] * MOBA_PAIRS + [km]))


MXU_TILE = 256
assert D_FF % MXU_TILE == 0
FF_SPLITS = (0, (D_FF // MXU_TILE + 1) // 2 * MXU_TILE, D_FF)


MIX_SPLIT = 2


def _mix_slab(rows, ya_ref, o_refs, l_refs, yc_ref, gates_ref, x_ref, p_ref,
              wa_ref, wb_ref, wc_ref, wo_ref, g_mix_ref,
              g_pre_ref, wgate_ref, wup_ref, wdown_ref, g_post_ref, wple_ref, wpg_ref, g_ple_ref, o_ref):
    lses = [l_ref[rows, :] for l_ref in l_refs]
    top = jnp.maximum(jnp.maximum(lses[0], lses[1]), lses[2])
    wts = [jnp.exp2(t - top) for t in lses]
    den = wts[0] + wts[1] + wts[2]
    yb = (wts[0] * o_refs[0][rows, :] + wts[1] * o_refs[1][rows, :] + wts[2] * o_refs[2][rows, :]) / den
    merged = None
    for c, (y, w_ref) in enumerate(((ya_ref[rows, :], wa_ref), (yb.astype(BF16), wb_ref), (yc_ref[rows, :], wc_ref))):
        branch = _dot(y, w_ref[...])
        yield
        term = gates_ref[rows, c * D_MODEL:(c + 1) * D_MODEL].astype(F32) * branch
        merged = term if merged is None else merged + term
    out = _dot(merged.astype(BF16), wo_ref[...])
    yield
    x = x_ref[rows, :] + _rms(out, g_mix_ref[...])

    hb = _rms(x, g_pre_ref[...]).astype(BF16)
    down = None
    for lo, hi in zip(FF_SPLITS[:-1], FF_SPLITS[1:]):
        cols = slice(lo, hi)
        gate = _dot(hb, wgate_ref[:, cols])
        yield
        up = _dot(hb, wup_ref[:, cols])
        yield
        ff = (gate * jax.nn.sigmoid(gate) * up).astype(BF16)
        part = _dot(ff, wdown_ref[cols, :])
        yield
        down = part if down is None else down + part
    x = x + _rms(down, g_post_ref[...])

    emb = _dot(p_ref[0, rows, :].astype(BF16), wple_ref[...])
    yield
    ple = emb * jax.nn.sigmoid(_dot(x.astype(BF16), wpg_ref[...]))
    yield
    o_ref[rows, :] = x + _rms(ple, g_ple_ref[...])


def _mix_kernel(ya_ref, o0_ref, o1_ref, o2_ref, l0_ref, l1_ref, l2_ref, *rest):
    slab = TM // MIX_SPLIT
    chains = [_mix_slab(pl.ds(n * slab, slab), ya_ref, (o0_ref, o1_ref, o2_ref), (l0_ref, l1_ref, l2_ref), *rest)
              for n in range(MIX_SPLIT)]
    while chains:
        chains = [c for c in chains if next(c, StopIteration) is not StopIteration]


def _mix(ya, ob, lb, yc, gates, x, p, layer, mix_weights, g_mix, g_pre, ffn_weights, g_post, ple_weights, g_ple):
    T, D = x.shape
    tok = lambda w: pl.BlockSpec((TM, w), lambda i: (i, 0))
    vec = _resident((1, D))
    whole = lambda ws: [_resident(w.shape) for w in ws]
    return pl.pallas_call(
        _mix_kernel,
        grid=(T // TM,),
        in_specs=[tok(FOX_COLS)] + [tok(LANES)] * 6 + [tok(MOBA_COLS), tok(N_GATE_COLS), tok(D),
                  pl.BlockSpec((1, TM, PLE_DIM), lambda i: (layer, i, 0))]
        + whole(mix_weights) + [vec, vec] + whole(ffn_weights) + [vec] + whole(ple_weights) + [vec],
        out_specs=tok(D),
        out_shape=jax.ShapeDtypeStruct((T, D), F32),
        compiler_params=_params("parallel"),
        name="mix",
    )(ya, *ob, *lb, yc, gates, x, p, *mix_weights, g_mix, g_pre, *ffn_weights, g_post, *ple_weights, g_ple)


def _rope_tables(seq):
    f32 = np.float32
    inv = (f32(1.0) / (f32(ROPE_THETA) ** (np.arange(0, HEAD_DIM, 2, dtype=f32) / f32(HEAD_DIM)))).astype(f32)
    ang = np.arange(seq, dtype=f32)[:, None] * inv[None, :]
    cos, sin = np.cos(ang).astype(f32), np.sin(ang).astype(f32)
    reps = LANES // HEAD_DIM
    cos_t = np.tile(np.concatenate([cos, cos], axis=1), (1, reps))
    sin_t = np.tile(np.concatenate([-sin, sin], axis=1), (1, reps))
    return jnp.asarray(cos_t), jnp.asarray(sin_t)


def kernel(x, p, g_mix_pre, w_in, b_f, w_br_a, w_br_b, w_br_c, w_out, g_mix_post, g_ffn_pre,
           w_ffn_gate, w_ffn_up, w_ffn_down, g_ffn_post, w_ple, w_ple_gate, g_ple_post):
    B, S, D = x.shape
    T = B * S
    depth = w_in.shape[0]
    assert D == D_MODEL and S % TM == 0 and S % TQ == 0 and TM % PROJ_SLAB == 0 and TM % MIX_SPLIT == 0
    assert w_in.shape[1:] == (D, 3 * MIX_WIDTH + N_HEADS_FOX + N_GATE_COLS) and p.shape == (depth, B, S, PLE_DIM)
    cos_t, sin_t = _rope_tables(S)
    row = lambda g: g.reshape(1, -1)
    f0 = 3 * MIX_WIDTH
    for i in range(depth):
        wq = w_in[i, :, :MIX_WIDTH].astype(BF16)
        wk = w_in[i, :, MIX_WIDTH:2 * MIX_WIDTH].astype(BF16)
        wv = w_in[i, :, 2 * MIX_WIDTH:f0].astype(BF16)
        wf = jnp.pad(w_in[i, :, f0:f0 + N_HEADS_FOX], ((0, 0), (0, LANES - N_HEADS_FOX))).astype(BF16)
        bf = jnp.pad(b_f[i], (0, LANES - N_HEADS_FOX)).reshape(1, LANES)
        wg = w_in[i, :, f0 + N_HEADS_FOX:].astype(BF16)

        (q, k, v, vt, kx, gates, kmean), residue_views = _proj(
            x, row(g_mix_pre[i]), wq, wk, wv, wf, bf, wg, cos_t, sin_t)
        km = kmean.reshape(B, S // MOBA_BLOCK, MOBA_COLS).astype(BF16)
        y_a = _fox(q, k, kx, vt)
        win_in = {g: residue_views[3 * n:3 * n + 3] for n, (g, _) in enumerate(STRIDED_GROUPS)}
        win = [_window(*win_in.get(g, (q, k, v)), g) for g in range(len(DIL_PATTERNS))]
        y_c = _moba(q, k, vt, km)

        bf16 = lambda *ws: [w[i].astype(BF16) for w in ws]
        x = _mix(y_a.reshape(T, -1), [o for o, _ in win], [l for _, l in win], y_c.reshape(T, -1),
                 gates.reshape(T, -1), x.reshape(T, D), p.reshape(depth, T, PLE_DIM), i,
                 bf16(w_br_a, w_br_b, w_br_c, w_out), row(g_mix_post[i]),
                 row(g_ffn_pre[i]), bf16(w_ffn_gate, w_ffn_up, w_ffn_down), row(g_ffn_post[i]),
                 bf16(w_ple, w_ple_gate), row(g_ple_post[i])).reshape(B, S, D)
    return x
```

```python
import functools
import math

import numpy as np
import jax
import jax.numpy as jnp
from jax import lax
from jax.experimental import pallas as pl
from jax.experimental.pallas import tpu as pltpu

D_MODEL = 1024
HEAD_DIM = 64
N_HEADS_FOX = 4
DIL_PATTERNS = ((128, 1), (512, 4), (2048, 16))
N_HEADS_MOBA = 6
MIX_WIDTH = 1024
MOBA_BLOCK = 256
MOBA_TOPK = 3
PLE_DIM = 256
D_FF = 2816
ROPE_THETA = 10000.0
RMS_EPS = 1e-6
NEG_INF = -1e30
LOG2E = math.log2(math.e)
Q_SCALE = HEAD_DIM ** -0.5 * LOG2E

LANES = 128
SUBLANES = 8
HEADS_PER_LANE_BLOCK = LANES // HEAD_DIM
N_LANE_BLOCKS = MIX_WIDTH // LANES
FOX_COLS = N_HEADS_FOX * HEAD_DIM
DIL_COLS = len(DIL_PATTERNS) * 2 * HEAD_DIM
MOBA_COLS = N_HEADS_MOBA * HEAD_DIM
N_GATE_COLS = 3 * D_MODEL
FORGET_PIECES = 3

TM = 512
TQ = 512
CHUNK = 512
VMEM_LIMIT = 56 * 1024 * 1024

F32 = jnp.float32
BF16 = jnp.bfloat16


def _dot(a, b):
    return jnp.dot(a, b, preferred_element_type=F32)


def _dot_tn(a, b):
    return lax.dot_general(a, b, (((0,), (0,)), ((), ())), preferred_element_type=F32)


def _rms(x, g):
    var = jnp.mean(x * x, axis=-1, keepdims=True)
    return x * lax.rsqrt(var + RMS_EPS) * g


def _split3(x):
    hi = x.astype(BF16).astype(F32)
    r1 = x - hi
    mid = r1.astype(BF16).astype(F32)
    lo = (r1 - mid).astype(BF16).astype(F32)
    return hi, mid, lo


def _resident(shape):
    nd = len(shape)
    return pl.BlockSpec(shape, lambda *_: (0,) * nd, pipeline_mode=pl.Buffered(1))


def _params(*sem):
    return pltpu.CompilerParams(dimension_semantics=sem, vmem_limit_bytes=VMEM_LIMIT)


STRIDED_GROUPS = tuple((g, dil) for g, (_, dil) in enumerate(DIL_PATTERNS) if dil > 1)


PROJ_SLAB = MOBA_BLOCK


def _proj_slab(n, carry, x_ref, g_ref, wq_ref, wk_ref, wv_ref, wf_ref, bf_ref, wg_ref, cos_ref, sin_ref,
               q_ref, k_ref, v_ref, vt_ref, kx_ref, gates_ref, kmean_ref, res_refs, perm_sc):
    rows = pl.ds(n * PROJ_SLAB, PROJ_SLAB)

    def scatter_residues(t, which, blk):
        for m, (g, dil) in enumerate(STRIDED_GROUPS):
            if blk == FOX_COLS // LANES + g:
                slot = 3 * m + which
                per = PROJ_SLAB // dil
                perm_sc[slot, rows, :] = t
                for r in range(dil):
                    res_refs[slot][0, n * per:(n + 1) * per, r * LANES:(r + 1) * LANES] = perm_sc[
                        slot, pl.ds(n * PROJ_SLAB + r, per, stride=dil), :].astype(BF16)

    hb = _rms(x_ref[0, rows, :], g_ref[...]).astype(BF16)
    cos = cos_ref[rows, :]
    sin = sin_ref[rows, :]
    lane = lax.broadcasted_iota(jnp.int32, (PROJ_SLAB, LANES), 1)
    first_half = (lane % HEAD_DIM) < (HEAD_DIM // 2)

    def rope(t):
        rot = jnp.where(first_half, pltpu.roll(t, LANES - HEAD_DIM // 2, 1),
                        pltpu.roll(t, HEAD_DIM // 2, 1))
        return t * cos + rot * sin

    first_rope_blk = FOX_COLS // LANES
    first_moba_blk = (FOX_COLS + DIL_COLS) // LANES

    f = _dot(hb, wf_ref[...]) + bf_ref[...]
    ls = jnp.minimum(f, 0.0) - jnp.log1p(jnp.exp(-jnp.abs(f)))
    ls = jnp.where(lane < N_HEADS_FOX, ls, 0.0)
    row = lax.broadcasted_iota(jnp.int32, (PROJ_SLAB, PROJ_SLAB), 0)
    col = lax.broadcasted_iota(jnp.int32, (PROJ_SLAB, PROJ_SLAB), 1)
    tri = jnp.where(col <= row, 1.0, 0.0).astype(BF16)
    ls_hi, ls_mid, ls_lo = _split3(ls)

    q = _dot(hb, wq_ref[...])
    yield
    cs = _dot(tri, ls_hi.astype(BF16)) + _dot(tri, ls_mid.astype(BF16)) + _dot(tri, ls_lo.astype(BF16))
    for c in range(N_LANE_BLOCKS):
        t = q[:, c * LANES:(c + 1) * LANES]
        if c >= first_rope_blk:
            t = rope(t)
        t = t * Q_SCALE
        q_ref[0, rows, c * LANES:(c + 1) * LANES] = t.astype(BF16)
        scatter_residues(t, 0, c)

    k = _dot(hb, wk_ref[...])
    yield
    cs = cs + carry[n]
    carry.append(cs[PROJ_SLAB - 1:PROJ_SLAB, :])
    hi, mid, lo = _split3(-LOG2E * cs)
    pieces = jnp.where(lane < N_HEADS_FOX, hi,
                       jnp.where(lane < 2 * N_HEADS_FOX, pltpu.roll(mid, N_HEADS_FOX, 1),
                                 pltpu.roll(lo, 2 * N_HEADS_FOX, 1)))
    kx_ref[0, rows, :] = pieces.astype(BF16)

    for c in range(N_LANE_BLOCKS):
        t = k[:, c * LANES:(c + 1) * LANES]
        if c >= first_rope_blk:
            t = rope(t)
        k_ref[0, rows, c * LANES:(c + 1) * LANES] = t.astype(BF16)
        scatter_residues(t, 1, c)
        if c >= first_moba_blk:
            cm = c - first_moba_blk
            kmean_ref[0, 0, n:n + 1, cm * LANES:(cm + 1) * LANES] = jnp.mean(t, axis=0, keepdims=True)

    v = _dot(hb, wv_ref[...])
    yield
    v_ref[0, rows, :] = v.astype(BF16)
    for c in range(N_LANE_BLOCKS):
        scatter_residues(v[:, c * LANES:(c + 1) * LANES], 2, c)
    vt_ref[0, :, rows] = v.T.astype(BF16)

    for c in range(N_GATE_COLS // D_MODEL):
        z = _dot(hb, wg_ref[:, c * D_MODEL:(c + 1) * D_MODEL])
        yield
        gates_ref[0, rows, c * D_MODEL:(c + 1) * D_MODEL] = jax.nn.sigmoid(z).astype(BF16)


def _proj_kernel(*refs):
    *io_refs, carry_ref, perm_sc = refs
    n_res = 3 * len(STRIDED_GROUPS)
    main_refs, res_refs = io_refs[:len(io_refs) - n_res], io_refs[len(io_refs) - n_res:]

    @pl.when(pl.program_id(1) == 0)
    def _():
        carry_ref[...] = jnp.zeros_like(carry_ref)

    carry = [carry_ref[...]]
    chains = [_proj_slab(n, carry, *main_refs, res_refs, perm_sc) for n in range(TM // PROJ_SLAB)]
    while chains:
        chains = [c for c in chains if next(c, StopIteration) is not StopIteration]
    carry_ref[...] = carry[-1]


def _proj(x, g, wq, wk, wv, wf, bf, wg, cos_t, sin_t):
    B, S, D = x.shape
    n_t = S // TM
    tok = lambda w: pl.BlockSpec((1, TM, w), lambda b, i: (b, i, 0))
    res_specs, res_shapes = [], []
    for _, dil in STRIDED_GROUPS:
        assert TM % (dil * 2 * SUBLANES) == 0 and S % dil == 0
        res_specs += [pl.BlockSpec((1, TM // dil, dil * LANES), lambda b, i: (b, i, 0))] * 3
        res_shapes += [jax.ShapeDtypeStruct((B, S // dil, dil * LANES), BF16)] * 3
    outs = pl.pallas_call(
        _proj_kernel,
        grid=(B, n_t),
        in_specs=[tok(D), _resident((1, D)), _resident((D, MIX_WIDTH)), _resident((D, MIX_WIDTH)),
                  _resident((D, MIX_WIDTH)),
                  _resident((D, LANES)), _resident((1, LANES)), _resident((D, N_GATE_COLS)),
                  pl.BlockSpec((TM, LANES), lambda b, i: (i, 0)),
                  pl.BlockSpec((TM, LANES), lambda b, i: (i, 0))],
        out_specs=[tok(MIX_WIDTH), tok(MIX_WIDTH), tok(MIX_WIDTH),
                   pl.BlockSpec((1, MIX_WIDTH, TM), lambda b, i: (b, 0, i)),
                   tok(LANES), tok(N_GATE_COLS),
                   pl.BlockSpec((1, 1, TM // MOBA_BLOCK, MOBA_COLS), lambda b, i: (b, i, 0, 0))] + res_specs,
        out_shape=[jax.ShapeDtypeStruct((B, S, MIX_WIDTH), BF16)] * 3
        + [jax.ShapeDtypeStruct((B, MIX_WIDTH, S), BF16),
           jax.ShapeDtypeStruct((B, S, LANES), BF16),
           jax.ShapeDtypeStruct((B, S, N_GATE_COLS), BF16),
           jax.ShapeDtypeStruct((B, n_t, TM // MOBA_BLOCK, MOBA_COLS), F32)] + res_shapes,
        scratch_shapes=[pltpu.VMEM((1, LANES), F32), pltpu.VMEM((len(res_specs), TM, LANES), F32)],
        compiler_params=_params("arbitrary", "arbitrary"),
        name="proj",
    )(x, g, wq, wk, wv, wf, bf, wg, cos_t, sin_t)
    return outs[:7], outs[7:]


def _reduce_keys(x, op, reduce):
    n = x.shape[0]
    while n > SUBLANES and n % (2 * SUBLANES) == 0:
        n //= 2
        x = op(x[:n], x[n:])
    return reduce(x, axis=0, keepdims=True)


def _key_max(s):
    return _reduce_keys(s, jnp.maximum, jnp.max)


def _probabilities(s, m):
    return jnp.exp2((s - m).astype(BF16))


ACC_ROWS = HEAD_DIM + 2 * SUBLANES


def _values_with_ones(vt):
    return jnp.concatenate([vt, jnp.ones((ACC_ROWS - HEAD_DIM, vt.shape[1]), BF16)], axis=0)


def _score(k_aug, q_aug, keys=slice(None), queries=slice(None)):
    return _dot(k_aug[keys], q_aug[:, queries])


def _chunked_attention(i, n_heads, scores, values, s_sc, p_sc):
    heads = range(n_heads)
    n_full = (i * TQ) // CHUNK

    assert TQ == CHUNK and CHUNK % (2 * LANES) == 0
    half = CHUNK // 2
    early, late = slice(0, half), slice(half, CHUNK)
    kr = lax.broadcasted_iota(jnp.int32, (half, half), 0)
    qc = lax.broadcasted_iota(jnp.int32, (half, half), 1)
    causal = kr <= qc
    diag = scores(n_full)
    s_early = [score(keys=early) for score in diag]
    s_late = [score(keys=late, queries=late) for score in diag]
    s_first = [score() for score in scores(0)]
    m, alpha = [], []
    for hl in heads:
        se = jnp.concatenate([jnp.where(causal, s_early[hl][:, early], NEG_INF), s_early[hl][:, late]], axis=1)
        sl = jnp.where(causal, s_late[hl], NEG_INF)
        m_diag = jnp.maximum(_key_max(se),
                             jnp.concatenate([jnp.full((1, half), NEG_INF, F32), _key_max(sl)], axis=1))
        p_sc[hl, early, :] = _probabilities(se, m_diag)
        p_sc[hl, late, :] = jnp.concatenate([jnp.zeros((half, half), BF16), _probabilities(sl, m_diag[:, late])], axis=1)
        s_sc[hl] = s_first[hl]
        m.append(jnp.where(n_full > 0, jnp.maximum(m_diag, _key_max(s_first[hl])), m_diag))
        alpha.append(jnp.exp2(m_diag - m[hl]))

    def product(c, p):
        return [_dot(_values_with_ones(values(c, hl)), p[hl]) for hl in heads]

    def before(c):
        return jnp.where(c <= 0, n_full, c - 1)

    def body(c, carry):
        m, alpha, acc = carry
        pv = product(before(c), [p_sc[hl] for hl in heads])
        s_next = [score() for score in scores(c + 1)]
        m_next = tuple(jnp.maximum(m[hl], _key_max(s_next[hl])) for hl in heads)
        alpha_next = tuple(jnp.exp2(m[hl] - m_next[hl]) for hl in heads)
        for hl in heads:
            p_sc[hl] = _probabilities(s_sc[hl], m[hl])
            s_sc[hl] = s_next[hl]
        acc = tuple(alpha[hl] * (acc[hl] + pv[hl]) for hl in heads)
        return m_next, alpha_next, acc

    init = (tuple(m), tuple(alpha), tuple(jnp.zeros((ACC_ROWS, TQ), F32) for _ in heads))
    m, alpha, acc = lax.fori_loop(0, n_full - 1, body, init)

    last = n_full - 1
    pv = product(before(last), [p_sc[hl] for hl in heads])
    m_tail = [jnp.where(n_full > 0, m[hl], -NEG_INF) for hl in heads]
    tail = product(jnp.maximum(last, 0), [_probabilities(s_sc[hl], m_tail[hl]) for hl in heads])
    return [alpha[hl] * (acc[hl] + pv[hl]) + tail[hl] for hl in heads]


def _query_t(q):
    return q.astype(F32).T


def _head_rows(q_t, hl):
    sub = lax.broadcasted_iota(jnp.int32, q_t.shape, 0)
    return jnp.where((sub >= hl * HEAD_DIM) & (sub < (hl + 1) * HEAD_DIM), q_t, 0.0)


def _finish(accs):
    blocks = []
    for p in range(len(accs) // HEADS_PER_LANE_BLOCK):
        pair = accs[p * HEADS_PER_LANE_BLOCK:(p + 1) * HEADS_PER_LANE_BLOCK]
        out_t = jnp.concatenate([acc[:HEAD_DIM] / acc[HEAD_DIM:HEAD_DIM + 1] for acc in pair], axis=0)
        blocks.append(out_t.T)
    return jnp.concatenate(blocks, axis=1)


def _chunk_scratch(n_heads):
    return [pltpu.VMEM((n_heads, CHUNK, TQ), F32), pltpu.VMEM((n_heads, CHUNK, TQ), BF16)]


def _fox_kernel(q_ref, k_ref, kx_ref, vt_ref, o_ref, s_sc, p_sc):
    i = pl.program_id(1)
    n_pairs = FOX_COLS // LANES
    sub = lax.broadcasted_iota(jnp.int32, (LANES, TQ), 0)
    q_aug = []
    for p in range(n_pairs):
        q_t = _query_t(q_ref[0, :, p * LANES:(p + 1) * LANES])
        for hl in range(HEADS_PER_LANE_BLOCK):
            head = HEADS_PER_LANE_BLOCK * p + hl
            ones = jnp.where((sub % N_HEADS_FOX == head) & (sub < FORGET_PIECES * N_HEADS_FOX), 1.0, 0.0)
            q_aug.append(jnp.concatenate([_head_rows(q_t, hl), ones], axis=0).astype(BF16))

    def scores(c):
        off = pl.multiple_of(c * CHUNK, CHUNK)
        kx = kx_ref[0, pl.ds(off, CHUNK), :]
        out = []
        for p in range(n_pairs):
            k_aug = jnp.concatenate([k_ref[0, pl.ds(off, CHUNK), p * LANES:(p + 1) * LANES], kx], axis=1)
            out += [functools.partial(_score, k_aug, qa)
                    for qa in q_aug[p * HEADS_PER_LANE_BLOCK:(p + 1) * HEADS_PER_LANE_BLOCK]]
        return out

    def values(c, h):
        off = pl.multiple_of(c * CHUNK, CHUNK)
        return vt_ref[0, h * HEAD_DIM:(h + 1) * HEAD_DIM, pl.ds(off, CHUNK)]

    o_ref[0] = _finish(_chunked_attention(i, N_HEADS_FOX, scores, values, s_sc, p_sc)).astype(BF16)


def _fox(q, k, kx, vt):
    B, S, _ = q.shape
    assert S % CHUNK == 0
    return pl.pallas_call(
        _fox_kernel,
        grid=(B, S // TQ),
        in_specs=[pl.BlockSpec((1, TQ, FOX_COLS), lambda b, i: (b, i, 0)),
                  pl.BlockSpec((1, S, FOX_COLS), lambda b, i: (b, 0, 0)),
                  pl.BlockSpec((1, S, LANES), lambda b, i: (b, 0, 0)),
                  pl.BlockSpec((1, FOX_COLS, S), lambda b, i: (b, 0, 0))],
        out_specs=pl.BlockSpec((1, TQ, FOX_COLS), lambda b, i: (b, i, 0)),
        out_shape=jax.ShapeDtypeStruct((B, S, FOX_COLS), BF16),
        scratch_shapes=_chunk_scratch(N_HEADS_FOX),
        compiler_params=_params("parallel", "arbitrary"),
        name="fox",
    )(q, k, kx, vt)


WINDOW_KEYS = DIL_PATTERNS[0][0] // DIL_PATTERNS[0][1]
WQ = 256
assert all(w // d == WINDOW_KEYS for w, d in DIL_PATTERNS) and WINDOW_KEYS <= WQ


WINDOW_CHAINS = 16


def _window_kernel(q_ref, kc_ref, kp_ref, vc_ref, vp_ref, o_ref, lse_ref, *, dil, n_res, n_sub):
    a = pl.program_id(1)
    rg = pl.program_id(2)
    kr = lax.broadcasted_iota(jnp.int32, (WQ, WQ), 0)
    qc = lax.broadcasted_iota(jnp.int32, (WQ, WQ), 1)
    valid_d = (kr <= qc) & (qc - kr <= WINDOW_KEYS)
    krp = lax.broadcasted_iota(jnp.int32, (WINDOW_KEYS, WQ), 0)
    qcp = lax.broadcasted_iota(jnp.int32, (WINDOW_KEYS, WQ), 1)
    valid_p = krp >= qcp
    valid_first = valid_p & (a > 0)

    chains = [(rr, t) for rr in range(n_res) for t in range(n_sub)]
    heads = range(HEADS_PER_LANE_BLOCK)
    raw, vals = [], []
    for rr, t in chains:
        cols = slice(rr * LANES, (rr + 1) * LANES)
        sub = slice(t * WQ, (t + 1) * WQ)
        q_t = _query_t(q_ref[0, sub, cols])
        if t == 0:
            kp, vp = kp_ref[0, :, cols], vp_ref[0, :, cols]
        else:
            before = slice(t * WQ - WINDOW_KEYS, t * WQ)
            kp, vp = kc_ref[0, before, cols], vc_ref[0, before, cols]
        kd = kc_ref[0, sub, cols]
        vals.append((vc_ref[0, sub, cols], vp))
        q_m = [_head_rows(q_t, hl).astype(BF16) for hl in heads]
        raw.append([(_dot(kd, qm), _dot(kp, qm)) for qm in q_m])

    probs, stats = [], []
    for n, (rr, t) in enumerate(chains):
        for hl in heads:
            s_d = jnp.where(valid_d, raw[n][hl][0], NEG_INF)
            s_p = jnp.where(valid_first if t == 0 else valid_p, raw[n][hl][1], NEG_INF)
            m = jnp.maximum(_key_max(s_d), _key_max(s_p))
            p_d = jnp.exp2(s_d - m)
            p_p = jnp.exp2(s_p - m)
            l = _reduce_keys(p_d, jnp.add, jnp.sum) + _reduce_keys(p_p, jnp.add, jnp.sum)
            probs.append((p_d.astype(BF16), p_p.astype(BF16)))
            stats.append((m, l))

    accs = [_dot_tn(vals[n][0], probs[n * len(heads) + hl][0]) + _dot_tn(vals[n][1], probs[n * len(heads) + hl][1])
            for n in range(len(chains)) for hl in heads]

    for n, (rr, t) in enumerate(chains):
        outs, lses = [], []
        for hl in heads:
            m, l = stats[n * len(heads) + hl]
            outs.append(accs[n * len(heads) + hl][hl * HEAD_DIM:(hl + 1) * HEAD_DIM] / l)
            lses.append(jnp.broadcast_to(m + jnp.log2(l), (HEAD_DIM, WQ)))
        r = rg * n_res + rr
        rows = pl.ds(t * WQ * dil + r, WQ, stride=dil) if dil > 1 else pl.ds(t * WQ, WQ)
        o_ref[0, rows, :] = jnp.concatenate(outs, axis=0).T
        lse_ref[0, rows, :] = jnp.concatenate(lses, axis=0).T


def _window(q, k, v, group):
    dil = DIL_PATTERNS[group][1]
    B, rows, width = q.shape
    S = rows * dil
    n_sub = min(WINDOW_CHAINS, rows // WQ)
    n_res = min(dil, WINDOW_CHAINS // n_sub)
    tile = n_sub * WQ
    assert rows % tile == 0 and tile % WINDOW_KEYS == 0 and dil % n_res == 0
    col0, col_step = (FOX_COLS // LANES + group, 0) if dil == 1 else (0, 1)
    assert width == (MIX_WIDTH if dil == 1 else dil * LANES)
    cur = pl.BlockSpec((1, tile, n_res * LANES), lambda b, a, rg: (b, a, col0 + col_step * rg))
    per = tile // WINDOW_KEYS
    prev = pl.BlockSpec((1, WINDOW_KEYS, n_res * LANES),
                        lambda b, a, rg: (b, jnp.maximum(a * per - 1, 0), col0 + col_step * rg))
    out = pl.BlockSpec((1, dil * tile, LANES), lambda b, a, rg: (b, a, 0))
    o, lse = pl.pallas_call(
        functools.partial(_window_kernel, dil=dil, n_res=n_res, n_sub=n_sub),
        grid=(B, rows // tile, dil // n_res),
        in_specs=[cur, cur, prev, cur, prev],
        out_specs=[out, out],
        out_shape=[jax.ShapeDtypeStruct((B, S, LANES), F32)] * 2,
        compiler_params=_params("parallel", "arbitrary", "arbitrary"),
        name=f"window{group}",
    )(q, k, k, v, v)
    return o.reshape(B * S, LANES), lse.reshape(B * S, LANES)


MOBA_PAIRS = MOBA_COLS // LANES


def _moba_kernel(*refs):
    q_refs, k_refs, vt_refs = refs[:MOBA_PAIRS], refs[MOBA_PAIRS:2 * MOBA_PAIRS], refs[2 * MOBA_PAIRS:3 * MOBA_PAIRS]
    km_ref, o_ref, s_sc, p_sc = refs[3 * MOBA_PAIRS:]
    i = pl.program_id(1)
    n_blocks = km_ref.shape[1]
    blk = lax.broadcasted_iota(jnp.int32, (n_blocks, TQ), 0)
    own = i * (TQ // MOBA_BLOCK) + lax.broadcasted_iota(jnp.int32, (n_blocks, TQ), 1) // MOBA_BLOCK
    past = blk < own
    never = jnp.full((LANES - n_blocks, TQ), NEG_INF, F32)
    q_aug = []
    for p in range(MOBA_PAIRS):
        q_t = _query_t(q_refs[p][0])
        km = km_ref[0, :, p * LANES:(p + 1) * LANES]
        for hl in range(HEADS_PER_LANE_BLOCK):
            qm = _head_rows(q_t, hl)
            gate = jnp.where(past, _dot(km, qm.astype(BF16)), NEG_INF)
            sel = blk == own
            for _ in range(MOBA_TOPK):
                top = jnp.max(gate, axis=0, keepdims=True)
                idx = jnp.min(jnp.where(gate == top, blk, n_blocks), axis=0, keepdims=True)
                hit = blk == idx
                sel = sel | (hit & past)
                gate = jnp.where(hit, -jnp.inf, gate)
            sel_bias = jnp.where(sel, 0.0, NEG_INF)
            q_aug.append(jnp.concatenate([qm, sel_bias, never], axis=0).astype(BF16))

    row = lax.broadcasted_iota(jnp.int32, (CHUNK, LANES), 0)
    lane = lax.broadcasted_iota(jnp.int32, (CHUNK, LANES), 1)
    blocks_per_chunk = CHUNK // MOBA_BLOCK

    def scores(c):
        off = pl.multiple_of(c * CHUNK, CHUNK)
        onehot = jnp.where(lane == c * blocks_per_chunk + row // MOBA_BLOCK, 1.0, 0.0).astype(BF16)
        out = []
        for p in range(MOBA_PAIRS):
            k_aug = jnp.concatenate([k_refs[p][0, pl.ds(off, CHUNK), :], onehot], axis=1)
            out += [functools.partial(_score, k_aug, qa)
                    for qa in q_aug[p * HEADS_PER_LANE_BLOCK:(p + 1) * HEADS_PER_LANE_BLOCK]]
        return out

    def values(c, h):
        off = pl.multiple_of(c * CHUNK, CHUNK)
        p, hl = divmod(h, HEADS_PER_LANE_BLOCK)
        return vt_refs[p][0, hl * HEAD_DIM:(hl + 1) * HEAD_DIM, pl.ds(off, CHUNK)]

    o_ref[0] = _finish(_chunked_attention(i, N_HEADS_MOBA, scores, values, s_sc, p_sc)).astype(BF16)


def _moba(q, k, vt, km):
    B, S, _ = q.shape
    n_blocks = S // MOBA_BLOCK
    assert TQ % MOBA_BLOCK == 0 and CHUNK % MOBA_BLOCK == 0 and S % CHUNK == 0
    assert n_blocks <= LANES and n_blocks % (2 * SUBLANES) == 0
    first = (FOX_COLS + DIL_COLS) // LANES
    at = lambda f: [pl.BlockSpec(*f(first + p)) for p in range(MOBA_PAIRS)]
    return pl.pallas_call(
        _moba_kernel,
        grid=(B, S // TQ),
        in_specs=at(lambda c: ((1, TQ, LANES), lambda b, i: (b, i, c)))
        + at(lambda c: ((1, S, LANES), lambda b, i: (b, 0, c)))
        + at(lambda c: ((1, LANES, S), lambda b, i: (b, c, 0)))
        + [pl.BlockSpec((1, n_blocks, MOBA_COLS), lambda b, i: (b, 0, 0))],
        out_specs=pl.BlockSpec((1, TQ, MOBA_COLS), lambda b, i: (b, i, 0)),
        out_shape=jax.ShapeDtypeStruct((B, S, MOBA_COLS), BF16),
        scratch_shapes=_chunk_scratch(N_HEADS_MOBA),
        compiler_params=_params("parallel", "arbitrary"),
        name="moba",
    )(*([q] * MOBA_PAIRS + [k] * MOBA_PAIRS + [vt] * MOBA_PAIRS + [km]))


MXU_TILE = 256
assert D_FF % MXU_TILE == 0
FF_SPLITS = (0, (D_FF // MXU_TILE + 1) // 2 * MXU_TILE, D_FF)


MIX_SPLIT = 2


def _mix_slab(rows, ya_ref, o_refs, l_refs, yc_ref, gates_ref, x_ref, p_ref,
              wa_ref, wb_ref, wc_ref, wo_ref, g_mix_ref,
              g_pre_ref, wgate_ref, wup_ref, wdown_ref, g_post_ref, wple_ref, wpg_ref, g_ple_ref, o_ref):
    lses = [l_ref[rows, :] for l_ref in l_refs]
    top = jnp.maximum(jnp.maximum(lses[0], lses[1]), lses[2])
    wts = [jnp.exp2(t - top) for t in lses]
    den = wts[0] + wts[1] + wts[2]
    yb = (wts[0] * o_refs[0][rows, :] + wts[1] * o_refs[1][rows, :] + wts[2] * o_refs[2][rows, :]) / den
    merged = None
    for c, (y, w_ref) in enumerate(((ya_ref[rows, :], wa_ref), (yb.astype(BF16), wb_ref), (yc_ref[rows, :], wc_ref))):
        branch = _dot(y, w_ref[...])
        yield
        term = gates_ref[rows, c * D_MODEL:(c + 1) * D_MODEL].astype(F32) * branch
        merged = term if merged is None else merged + term
    out = _dot(merged.astype(BF16), wo_ref[...])
    yield
    x = x_ref[rows, :] + _rms(out, g_mix_ref[...])

    hb = _rms(x, g_pre_ref[...]).astype(BF16)
    down = None
    for lo, hi in zip(FF_SPLITS[:-1], FF_SPLITS[1:]):
        cols = slice(lo, hi)
        gate = _dot(hb, wgate_ref[:, cols])
        yield
        up = _dot(hb, wup_ref[:, cols])
        yield
        ff = (gate * jax.nn.sigmoid(gate) * up).astype(BF16)
        part = _dot(ff, wdown_ref[cols, :])
        yield
        down = part if down is None else down + part
    x = x + _rms(down, g_post_ref[...])

    emb = _dot(p_ref[0, rows, :].astype(BF16), wple_ref[...])
    yield
    ple = emb * jax.nn.sigmoid(_dot(x.astype(BF16), wpg_ref[...]))
    yield
    o_ref[rows, :] = x + _rms(ple, g_ple_ref[...])


def _mix_kernel(ya_ref, o0_ref, o1_ref, o2_ref, l0_ref, l1_ref, l2_ref, *rest):
    slab = TM // MIX_SPLIT
    chains = [_mix_slab(pl.ds(n * slab, slab), ya_ref, (o0_ref, o1_ref, o2_ref), (l0_ref, l1_ref, l2_ref), *rest)
              for n in range(MIX_SPLIT)]
    while chains:
        chains = [c for c in chains if next(c, StopIteration) is not StopIteration]


def _mix(ya, ob, lb, yc, gates, x, p, layer, mix_weights, g_mix, g_pre, ffn_weights, g_post, ple_weights, g_ple):
    T, D = x.shape
    tok = lambda w: pl.BlockSpec((TM, w), lambda i: (i, 0))
    vec = _resident((1, D))
    whole = lambda ws: [_resident(w.shape) for w in ws]
    return pl.pallas_call(
        _mix_kernel,
        grid=(T // TM,),
        in_specs=[tok(FOX_COLS)] + [tok(LANES)] * 6 + [tok(MOBA_COLS), tok(N_GATE_COLS), tok(D),
                  pl.BlockSpec((1, TM, PLE_DIM), lambda i: (layer, i, 0))]
        + whole(mix_weights) + [vec, vec] + whole(ffn_weights) + [vec] + whole(ple_weights) + [vec],
        out_specs=tok(D),
        out_shape=jax.ShapeDtypeStruct((T, D), F32),
        compiler_params=_params("parallel"),
        name="mix",
    )(ya, *ob, *lb, yc, gates, x, p, *mix_weights, g_mix, g_pre, *ffn_weights, g_post, *ple_weights, g_ple)


def _rope_tables(seq):
    f32 = np.float32
    inv = (f32(1.0) / (f32(ROPE_THETA) ** (np.arange(0, HEAD_DIM, 2, dtype=f32) / f32(HEAD_DIM)))).astype(f32)
    ang = np.arange(seq, dtype=f32)[:, None] * inv[None, :]
    cos, sin = np.cos(ang).astype(f32), np.sin(ang).astype(f32)
    reps = LANES // HEAD_DIM
    cos_t = np.tile(np.concatenate([cos, cos], axis=1), (1, reps))
    sin_t = np.tile(np.concatenate([-sin, sin], axis=1), (1, reps))
    return jnp.asarray(cos_t), jnp.asarray(sin_t)


def kernel(x, p, g_mix_pre, w_in, b_f, w_br_a, w_br_b, w_br_c, w_out, g_mix_post, g_ffn_pre,
           w_ffn_gate, w_ffn_up, w_ffn_down, g_ffn_post, w_ple, w_ple_gate, g_ple_post):
    B, S, D = x.shape
    T = B * S
    depth = w_in.shape[0]
    assert D == D_MODEL and S % TM == 0 and S % TQ == 0 and TM % PROJ_SLAB == 0 and TM % MIX_SPLIT == 0
    assert w_in.shape[1:] == (D, 3 * MIX_WIDTH + N_HEADS_FOX + N_GATE_COLS) and p.shape == (depth, B, S, PLE_DIM)
    cos_t, sin_t = _rope_tables(S)
    row = lambda g: g.reshape(1, -1)
    f0 = 3 * MIX_WIDTH
    for i in range(depth):
        wq = w_in[i, :, :MIX_WIDTH].astype(BF16)
        wk = w_in[i, :, MIX_WIDTH:2 * MIX_WIDTH].astype(BF16)
        wv = w_in[i, :, 2 * MIX_WIDTH:f0].astype(BF16)
        wf = jnp.pad(w_in[i, :, f0:f0 + N_HEADS_FOX], ((0, 0), (0, LANES - N_HEADS_FOX))).astype(BF16)
        bf = jnp.pad(b_f[i], (0, LANES - N_HEADS_FOX)).reshape(1, LANES)
        wg = w_in[i, :, f0 + N_HEADS_FOX:].astype(BF16)

        (q, k, v, vt, kx, gates, kmean), residue_views = _proj(
            x, row(g_mix_pre[i]), wq, wk, wv, wf, bf, wg, cos_t, sin_t)
        km = kmean.reshape(B, S // MOBA_BLOCK, MOBA_COLS).astype(BF16)
        y_a = _fox(q, k, kx, vt)
        win_in = {g: residue_views[3 * n:3 * n + 3] for n, (g, _) in enumerate(STRIDED_GROUPS)}
        win = [_window(*win_in.get(g, (q, k, v)), g) for g in range(len(DIL_PATTERNS))]
        y_c = _moba(q, k, vt, km)

        bf16 = lambda *ws: [w[i].astype(BF16) for w in ws]
        x = _mix(y_a.reshape(T, -1), [o for o, _ in win], [l for _, l in win], y_c.reshape(T, -1),
                 gates.reshape(T, -1), x.reshape(T, D), p.reshape(depth, T, PLE_DIM), i,
                 bf16(w_br_a, w_br_b, w_br_c, w_out), row(g_mix_post[i]),
                 row(g_ffn_pre[i]), bf16(w_ffn_gate, w_ffn_up, w_ffn_down), row(g_ffn_post[i]),
                 bf16(w_ple, w_ple_gate), row(g_ple_post[i])).reshape(B, S, D)
    return x
```
